```python
import math
import jax, jax.numpy as jnp
from jax import lax
import numpy as np

D_MODEL = 1024
BATCH = 32
SEQ = 256
DEPTH = 2
DEC_BATCH = 4
DEC_SEQ = 4096
PAST_LEN = 512

GRID_W = 64
D_MIX = D_MODEL
LRU_W = D_MIX // 2
LRU_BLOCKS = 8
LRU_BW = LRU_W // LRU_BLOCKS
LRU_CONV = 4
LRU_C = 8.0
NA_HEADS = 4
NA_HD = 64
NA_W = NA_HEADS * NA_HD
NA_ROWS = 8
NA_COLS = 16
DF_HEADS = 4
DF_HD = 32
DF_VD = 2 * DF_HD
DF_QK_W = DF_HEADS * 2 * DF_HD
DF_V_W = DF_HEADS * DF_VD
D_IN = 2 * LRU_W + 3 * NA_W + 2 * DF_QK_W + DF_V_W
D_CAT = LRU_W + NA_W + DF_V_W
D_FF = 2816
FFN_CONV = 3
QBLOCK = 128
ROPE_BASE = 10000.0
EPS = 1e-6
NEG_INF = -1e30
N_MOD = 6

kernel_name = "hybrid_diffusion_prefix_trunk_step"


def rms_norm(x, g):
    xf = x.astype(jnp.float32)
    y = xf * lax.rsqrt(jnp.mean(xf * xf, axis=-1, keepdims=True) + EPS)
    return (y * g.astype(jnp.float32)).astype(x.dtype)


def modulate(x, shift, scale):
    return x * (1 + scale) + shift


def dwconv_centred(x, w, b):
    k = w.shape[0]
    left = (k - 1) // 2
    y = lax.conv_general_dilated(x, w[:, None, :].astype(x.dtype), window_strides=(1,),
                                 padding=[(left, k - 1 - left)],
                                 dimension_numbers=('NWC', 'WIO', 'NWC'),
                                 feature_group_count=x.shape[-1])
    return y + b.astype(x.dtype)


def in_projection(h, w_in):
    z = h @ w_in
    sizes = (LRU_W, LRU_W, NA_W, NA_W, NA_W, DF_QK_W, DF_QK_W, DF_V_W)
    offs = [int(o) for o in np.cumsum(sizes)[:-1]]
    return jnp.split(z, offs, axis=-1)


def merge_heads(y):
    b, hh, t, d = y.shape
    return y.transpose(0, 2, 1, 3).reshape(b, t, hh * d)


def rglru_coeffs(x, wa, ba, wx, bx, lam):
    b, t, w = x.shape
    xf = x.astype(jnp.float32)
    xb = xf.reshape(b, t, LRU_BLOCKS, LRU_BW)
    r = jax.nn.sigmoid(jnp.einsum('btni,nij->btnj', xb, wa.astype(jnp.float32)).reshape(b, t, w) + ba.astype(jnp.float32))
    i = jax.nn.sigmoid(jnp.einsum('btni,nij->btnj', xb, wx.astype(jnp.float32)).reshape(b, t, w) + bx.astype(jnp.float32))
    log_a = -LRU_C * r * jax.nn.softplus(-lam.astype(jnp.float32))
    a = jnp.exp(log_a)
    return a, jnp.sqrt(-jnp.expm1(2.0 * log_a)) * (i * xf)


def linear_scan(a, b, h0, reverse):
    end = -1 if reverse else 0
    b = b.at[:, end].add(a[:, end] * h0.astype(jnp.float32))

    def combine(e1, e2):
        a1, b1 = e1
        a2, b2 = e2
        return a1 * a2, a2 * b1 + b2

    _, h = lax.associative_scan(combine, (a, b), reverse=reverse, axis=1)
    return h


def lru_bidir(x, gate, p, h0f, h0b):
    af, bf = rglru_coeffs(x, p['lru_wa'][0], p['lru_ba'][0], p['lru_wx'][0], p['lru_bx'][0], p['lru_lambda'][0])
    hf = linear_scan(af, bf, h0f, False)
    ab, bb = rglru_coeffs(x, p['lru_wa'][1], p['lru_ba'][1], p['lru_wx'][1], p['lru_bx'][1], p['lru_lambda'][1])
    hb = linear_scan(ab, bb, h0b, True)
    y = ((hf + hb) * jax.nn.gelu(gate.astype(jnp.float32))).astype(x.dtype)
    return y, hf[:, -1], hb[:, 0]


def blocked_attention(q, k, v):
    nq, d = q.shape[-2], q.shape[-1]
    nb = nq // QBLOCK
    qb = jnp.moveaxis(q.reshape(q.shape[:-2] + (nb, QBLOCK, d)), -3, 0)
    kt = jnp.swapaxes(k, -1, -2)

    def one_block(qi):
        s = jnp.matmul(qi, kt).astype(jnp.float32)
        return jnp.matmul(jax.nn.softmax(s, axis=-1).astype(v.dtype), v)

    o = jnp.moveaxis(lax.map(one_block, qb), 0, -3)
    return o.reshape(o.shape[:-3] + (nq, o.shape[-1]))


def rope_axis(x, pos):
    half = x.shape[-1] // 2
    freqs = ROPE_BASE ** (-jnp.arange(half, dtype=jnp.float32) / half)
    ang = pos[:, None].astype(jnp.float32) * freqs[None, :]
    cos, sin = jnp.cos(ang).astype(x.dtype), jnp.sin(ang).astype(x.dtype)
    x1, x2 = x[..., :half], x[..., half:]
    return jnp.concatenate([x1 * cos - x2 * sin, x2 * cos + x1 * sin], axis=-1)


def rope_2d(x):
    t = jnp.arange(x.shape[-2])
    n = x.shape[-1] // 2
    return jnp.concatenate([rope_axis(x[..., :n], t // GRID_W), rope_axis(x[..., n:], t % GRID_W)], axis=-1)


def na_qkv(q, k, v, p):
    b, t, _ = q.shape
    q = rms_norm(q.reshape(b, t, NA_HEADS, NA_HD), p['na_q_norm']) * (NA_HD ** -0.5)
    k = rms_norm(k.reshape(b, t, NA_HEADS, NA_HD), p['na_k_norm'])
    v = v.reshape(b, t, NA_HEADS, NA_HD)
    return q.transpose(0, 2, 1, 3), k.transpose(0, 2, 1, 3), v.transpose(0, 2, 1, 3)


def df_qkv(q, k, v, p):
    b, t, _ = q.shape
    q = rms_norm(q.reshape(b, t, DF_HEADS, 2, DF_HD), p['df_q_norm']).transpose(0, 2, 3, 1, 4)
    k = rms_norm(k.reshape(b, t, DF_HEADS, 2, DF_HD), p['df_k_norm']).transpose(0, 2, 3, 1, 4)
    v = v.reshape(b, t, DF_HEADS, DF_VD).transpose(0, 2, 1, 3)
    return q, k, v


def diff_combine(o, p, lam_init):
    lv = p['df_lambda'].astype(jnp.float32)
    lam = jnp.exp(jnp.sum(lv[0] * lv[1])) - jnp.exp(jnp.sum(lv[2] * lv[3])) + lam_init
    y = o[:, :, 0] - lam.astype(o.dtype) * o[:, :, 1]
    return rms_norm(y, p['df_subln']) * (1.0 - lam_init)


def neighbourhood_attention(q, k, v, ck, cv, rpb):
    b, hh, t, d = q.shape
    rows_n = t // GRID_W
    kr = min(NA_ROWS, rows_n)
    ncb = GRID_W // NA_COLS
    kbw = 2 * NA_COLS
    rows = jnp.arange(rows_n)
    row_idx = jnp.clip(rows - kr // 2, 0, rows_n - kr)[:, None] + jnp.arange(kr)[None, :]
    cb = jnp.arange(ncb)
    col_idx = jnp.clip(cb * NA_COLS - NA_COLS // 2, 0, GRID_W - kbw)[:, None] + jnp.arange(kbw)[None, :]
    qcol = cb[:, None] * NA_COLS + jnp.arange(NA_COLS)[None, :]
    cstart = jnp.clip(qcol - NA_COLS // 2, 0, GRID_W - NA_COLS)
    kc = col_idx[:, None, :]
    in_win = (kc >= cstart[:, :, None]) & (kc < cstart[:, :, None] + NA_COLS)
    d_col = jnp.clip(kc - qcol[:, :, None] + NA_COLS - 1, 0, 2 * NA_COLS - 2)
    d_row = row_idx - rows[:, None] + NA_ROWS - 1
    bias = rpb[:, d_row[:, None, None, :, None], d_col[None, :, :, None, :]]
    qg = q.reshape(b, hh, rows_n, ncb, NA_COLS, d)
    gi_r, gi_c = row_idx[:, :, None, None], col_idx[None, None, :, :]
    kg = k.reshape(b, hh, rows_n, GRID_W, d)[:, :, gi_r, gi_c]
    vg = v.reshape(b, hh, rows_n, GRID_W, d)[:, :, gi_r, gi_c]
    s_loc = jnp.einsum('bhrcqd,bhrkcwd->bhrcqkw', qg, kg).astype(jnp.float32) + bias.astype(jnp.float32)
    s_loc = jnp.where(in_win[:, :, None, :], s_loc, NEG_INF)
    s_ctx = jnp.einsum('bhrcqd,bhld->bhrcql', qg, ck).astype(jnp.float32)
    n_loc = kr * kbw
    s = jnp.concatenate([s_loc.reshape(b, hh, rows_n, ncb, NA_COLS, n_loc), s_ctx], axis=-1)
    pr = jax.nn.softmax(s, axis=-1).astype(v.dtype)
    p_loc = pr[..., :n_loc].reshape(b, hh, rows_n, ncb, NA_COLS, kr, kbw)
    out = (jnp.einsum('bhrcqkw,bhrkcwd->bhrcqd', p_loc, vg)
           + jnp.einsum('bhrcql,bhld->bhrcqd', pr[..., n_loc:], cv))
    return out.reshape(b, hh, t, d)


def context_mixer(h, p, lam_init):
    b = h.shape[0]
    lx, lg, nq, nk, nv, dq, dk, dv = in_projection(h, p['w_in'])
    zeros = jnp.zeros((b, LRU_W), jnp.float32)
    ya, sf, sb = lru_bidir(dwconv_centred(lx, p['lru_conv_w'], p['lru_conv_b']), lg, p, zeros, zeros)
    q, k, v = na_qkv(nq, nk, nv, p)
    yb = blocked_attention(q, k, v)
    q2, k2, v2 = df_qkv(dq, dk, dv, p)
    o = blocked_attention(q2 * (DF_HD ** -0.5), k2, v2[:, :, None])
    yc = diff_combine(o, p, lam_init)
    y = jnp.concatenate([ya, merge_heads(yb), merge_heads(yc)], axis=-1) @ p['w_out']
    return y, (jnp.stack([sf, sb], axis=1), k, v, k2, v2)


def latent_mixer(h, st, ck, cv, dck, dcv, p, lam_init):
    lx, lg, nq, nk, nv, dq, dk, dv = in_projection(h, p['w_in'])
    ya, _, _ = lru_bidir(dwconv_centred(lx, p['lru_conv_w'], p['lru_conv_b']), lg, p, st[:, 0], st[:, 1])
    q, k, v = na_qkv(nq, nk, nv, p)
    yb = neighbourhood_attention(q, k, v, ck, cv, p['na_rpb'])
    q2, k2, v2 = df_qkv(dq, dk, dv, p)
    q2 = rope_2d(q2) * (DF_HD ** -0.5)
    keys = jnp.concatenate([rope_2d(k2), dck.astype(k2.dtype)], axis=-2)
    vals = jnp.concatenate([v2, dcv.astype(v2.dtype)], axis=-2)[:, :, None]
    yc = diff_combine(blocked_attention(q2, keys, vals), p, lam_init)
    y = jnp.concatenate([ya, merge_heads(yb), merge_heads(yc)], axis=-1) @ p['w_out']
    return y, None


def conv_ffn(h, p):
    u = dwconv_centred(h @ p['ffn_up'], p['ffn_conv_w'], p['ffn_conv_b'])
    val, gate = jnp.split(u, 2, axis=-1)
    return (jax.nn.gelu(gate) * val) @ p['ffn_down']


def trunk_layer(x, mod, mixer_fn, p):
    sh_a, sc_a, g_a, sh_f, sc_f, g_f = jnp.split(mod, N_MOD, axis=-1)
    y, extra = mixer_fn(modulate(rms_norm(x, p['norm_mix']), sh_a, sc_a))
    x = x + g_a * y
    x = x + g_f * conv_ffn(modulate(rms_norm(x, p['norm_ffn']), sh_f, sc_f), p)
    return x, extra


def setup_inputs(seed: int = 0) -> dict:
    key = jax.random.key(seed)
    ks = iter(jax.random.split(key, 40))

    def nrm(shape, s):
        return jax.random.normal(next(ks), shape, jnp.float32) * s

    u = jax.random.uniform(next(ks), (DEPTH, 2, LRU_W), jnp.float32, 0.9, 0.999)
    sa = u ** (1.0 / LRU_C)
    return {
        "x_prompt": nrm((BATCH, SEQ, D_MODEL), 1.0),
        "x_sample": nrm((DEC_BATCH, DEC_SEQ, D_MODEL), 1.0),
        "state_lru": nrm((DEC_BATCH, DEPTH, 2, LRU_W), 0.5),
        "cache_na_k": nrm((DEC_BATCH, DEPTH, NA_HEADS, PAST_LEN, NA_HD), 1.0),
        "cache_na_v": nrm((DEC_BATCH, DEPTH, NA_HEADS, PAST_LEN, NA_HD), 1.0),
        "cache_df_k": nrm((DEC_BATCH, DEPTH, DF_HEADS, 2, PAST_LEN, DF_HD), 1.0),
        "cache_df_v": nrm((DEC_BATCH, DEPTH, DF_HEADS, PAST_LEN, DF_VD), 1.0),
        "c": nrm((DEC_BATCH, D_MODEL), 1.0),
        "c_ctx": nrm((D_MODEL,), 1.0),
        "ada_w": nrm((DEPTH, D_MODEL, N_MOD * D_MODEL), D_MODEL ** -0.5),
        "ada_b": nrm((DEPTH, N_MOD * D_MODEL), 0.02),
        "norm_mix": 1.0 + nrm((DEPTH, D_MODEL), 0.05),
        "norm_ffn": 1.0 + nrm((DEPTH, D_MODEL), 0.05),
        "w_in": nrm((DEPTH, D_MODEL, D_IN), D_MODEL ** -0.5),
        "lru_conv_w": nrm((DEPTH, LRU_CONV, LRU_W), LRU_CONV ** -0.5),
        "lru_conv_b": nrm((DEPTH, LRU_W), 0.01),
        "lru_wa": nrm((DEPTH, 2, LRU_BLOCKS, LRU_BW, LRU_BW), LRU_BW ** -0.5),
        "lru_ba": nrm((DEPTH, 2, LRU_W), 0.01),
        "lru_wx": nrm((DEPTH, 2, LRU_BLOCKS, LRU_BW, LRU_BW), LRU_BW ** -0.5),
        "lru_bx": nrm((DEPTH, 2, LRU_W), 0.01),
        "lru_lambda": jnp.log(sa) - jnp.log1p(-sa),
        "na_q_norm": 1.0 + nrm((DEPTH, NA_HD), 0.05),
        "na_k_norm": 1.0 + nrm((DEPTH, NA_HD), 0.05),
        "na_rpb": nrm((DEPTH, NA_HEADS, 2 * NA_ROWS - 1, 2 * NA_COLS - 1), 0.1),
        "df_q_norm": 1.0 + nrm((DEPTH, DF_HD), 0.05),
        "df_k_norm": 1.0 + nrm((DEPTH, DF_HD), 0.05),
        "df_lambda": nrm((DEPTH, 4, DF_HD), 0.1),
        "df_subln": 1.0 + nrm((DEPTH, DF_VD), 0.05),
        "w_out": nrm((DEPTH, D_CAT, D_MODEL), D_CAT ** -0.5),
        "ffn_up": nrm((DEPTH, D_MODEL, 2 * D_FF), D_MODEL ** -0.5),
        "ffn_conv_w": nrm((DEPTH, FFN_CONV, 2 * D_FF), FFN_CONV ** -0.5),
        "ffn_conv_b": nrm((DEPTH, 2 * D_FF), 0.01),
        "ffn_down": nrm((DEPTH, D_FF, D_MODEL), D_FF ** -0.5),
    }


def reference(x_prompt, x_sample, state_lru, cache_na_k, cache_na_v, cache_df_k, cache_df_v, c, c_ctx,
              ada_w, ada_b, norm_mix, norm_ffn, w_in, lru_conv_w, lru_conv_b, lru_wa, lru_ba, lru_wx, lru_bx,
              lru_lambda, na_q_norm, na_k_norm, na_rpb, df_q_norm, df_k_norm, df_lambda, df_subln, w_out,
              ffn_up, ffn_conv_w, ffn_conv_b, ffn_down):
    y_p = x_prompt
    y_s = x_sample
    s_lru, s_nk, s_nv, s_dk, s_dv = [], [], [], [], []
    for l in range(DEPTH):
        p = dict(w_in=w_in[l], lru_conv_w=lru_conv_w[l], lru_conv_b=lru_conv_b[l], lru_wa=lru_wa[l],
                 lru_ba=lru_ba[l], lru_wx=lru_wx[l], lru_bx=lru_bx[l], lru_lambda=lru_lambda[l],
                 na_q_norm=na_q_norm[l], na_k_norm=na_k_norm[l], na_rpb=na_rpb[l], df_q_norm=df_q_norm[l],
                 df_k_norm=df_k_norm[l], df_lambda=df_lambda[l], df_subln=df_subln[l], w_out=w_out[l],
                 norm_mix=norm_mix[l], norm_ffn=norm_ffn[l], ffn_up=ffn_up[l], ffn_conv_w=ffn_conv_w[l],
                 ffn_conv_b=ffn_conv_b[l], ffn_down=ffn_down[l])
        lam_init = 0.8 - 0.6 * math.exp(-0.3 * l)
        mod_ctx = (jax.nn.silu(c_ctx) @ ada_w[l] + ada_b[l])[None, None, :]
        y_p, (st, nk, nv, dk, dv) = trunk_layer(y_p, mod_ctx, lambda h: context_mixer(h, p, lam_init), p)
        s_lru.append(st)
        s_nk.append(nk)
        s_nv.append(nv)
        s_dk.append(dk)
        s_dv.append(dv)
        mod_lat = (jax.nn.silu(c) @ ada_w[l] + ada_b[l])[:, None, :]
        y_s, _ = trunk_layer(y_s, mod_lat,
                             lambda h: latent_mixer(h, state_lru[:, l], cache_na_k[:, l], cache_na_v[:, l],
                                                    cache_df_k[:, l], cache_df_v[:, l], p, lam_init), p)
    new_state_lru = jnp.stack(s_lru, axis=1)
    new_na_k = jnp.stack(s_nk, axis=1)
    new_na_v = jnp.stack(s_nv, axis=1)
    new_df_k = jnp.stack(s_dk, axis=1)
    new_df_v = jnp.stack(s_dv, axis=1)
    return (y_p, y_s, new_state_lru, new_na_k, new_na_v, new_df_k, new_df_v)
```

```python
import functools
import math

import numpy as np
import jax
import jax.numpy as jnp
from jax import lax
from jax.experimental import pallas as pl
from jax.experimental.pallas import tpu as pltpu

F32 = jnp.float32
BF16 = jnp.bfloat16

D_MODEL = 1024
DEPTH = 2
GRID_W = 64
LRU_W = 512
LRU_BLOCKS = 8
LRU_BW = LRU_W // LRU_BLOCKS
LRU_CONV = 4
LRU_C = 8.0
NA_HEADS = 4
NA_HD = 64
NA_W = NA_HEADS * NA_HD
NA_ROWS = 8
NA_COLS = 16
DF_HEADS = 4
DF_HD = 32
DF_VD = 2 * DF_HD
DF_QK_W = DF_HEADS * 2 * DF_HD
DF_V_W = DF_HEADS * DF_VD
D_IN = 2 * LRU_W + 3 * NA_W + 2 * DF_QK_W + DF_V_W
D_CAT = LRU_W + NA_W + DF_V_W
D_FF = 2816
ROPE_BASE = 10000.0
EPS = 1e-6
NEG_INF = -1e30
N_MOD = 6

OFF_LX = 0
OFF_LG = OFF_LX + LRU_W
OFF_NQ = OFF_LG + LRU_W
OFF_NK = OFF_NQ + NA_W
OFF_NV = OFF_NK + NA_W
OFF_DQ = OFF_NV + NA_W
OFF_DK = OFF_DQ + DF_QK_W
OFF_DV = OFF_DK + DF_QK_W

SUBLANES = 8
VMEM_LIMIT = 56 * 1024 * 1024

NA_QROWS = 4
NA_KROWS = NA_QROWS + NA_ROWS
DF_CK = 512
FFN_CF = 256
HALO = SUBLANES


def _cparams(*sem):
    return pltpu.CompilerParams(dimension_semantics=sem, vmem_limit_bytes=VMEM_LIMIT)


def _gelu(x):
    return 0.5 * x * (1.0 + jnp.tanh(math.sqrt(2.0 / math.pi) * (x + 0.044715 * (x * x * x))))


def _sigmoid(x):
    return 1.0 / (1.0 + jnp.exp(-x))


def _neg_expm1(x, u):
    safe = u != 1.0
    return jnp.where(safe, (1.0 - u) * x / jnp.log(jnp.where(safe, u, 0.5)), -x)


def _dot(a, b):
    return jnp.dot(a, b, preferred_element_type=F32)


def _dot_nt(a, b):
    return lax.dot_general(a, b, (((1,), (1,)), ((), ())), preferred_element_type=F32)


def _mod_kernel(c_ref, w_ref, b_ref, o_ref):
    c = c_ref[...]
    s = (c * _sigmoid(c)).astype(BF16)
    o_ref[0] = _dot(s, w_ref[0].astype(BF16)) + b_ref[0]


def _modulation(c_all, ada_w, ada_b):
    nb = 1536
    n = N_MOD * D_MODEL
    return pl.pallas_call(
        _mod_kernel,
        out_shape=jax.ShapeDtypeStruct((DEPTH, SUBLANES, n), F32),
        grid=(DEPTH, n // nb),
        in_specs=[pl.BlockSpec((SUBLANES, D_MODEL), lambda l, j: (0, 0)),
                  pl.BlockSpec((1, D_MODEL, nb), lambda l, j: (l, 0, j)),
                  pl.BlockSpec((1, 1, nb), lambda l, j: (l, 0, j))],
        out_specs=pl.BlockSpec((1, SUBLANES, nb), lambda l, j: (l, 0, j)),
        compiler_params=_cparams("arbitrary", "arbitrary"),
        name="modulation",
    )(c_all, ada_w, ada_b.reshape(DEPTH, 1, n))


def _group_rms(z, g_ref, gain):
    ms = _dot((z * z).astype(BF16), g_ref[...])
    return z * lax.rsqrt(ms + EPS) * gain


def _rope(x, cos, sin):
    w = x.shape[-1]
    lane = lax.broadcasted_iota(jnp.int32, x.shape, 1)
    partner = jnp.where((lane % 16) < 8, pltpu.roll(x, w - 8, 1), pltpu.roll(x, 8, 1))
    return x * cos + partner * sin


def _prep_kernel(latent, x_ref, sh_ref, sc_ref, g_ref, w_ref, qn_ref, kn_ref, dqn_ref, dkn_ref,
                 g64_ref, g32_ref, *rest):
    if latent:
        cos_ref, sin_ref, lx_ref, lg_ref, nq_ref, nk_ref, nv_ref, dq_ref, dk_ref, dv_ref = rest
    else:
        lx_ref, lg_ref, nq_ref, nk_ref, nv_ref, dq_ref, dk_ref, dv_ref = rest
    x = x_ref[0]
    ms = jnp.mean(x * x, axis=-1, keepdims=True)
    y = x * lax.rsqrt(ms + EPS) * g_ref[...]
    h = (y * (1.0 + sc_ref[0]) + sh_ref[0]).astype(BF16)

    def seg(off, size):
        return _dot(h, w_ref[:, off:off + size])

    lx_ref[0] = seg(OFF_LX, LRU_W)
    lg_ref[0] = seg(OFF_LG, LRU_W)

    nq = _group_rms(seg(OFF_NQ, NA_W), g64_ref, qn_ref[...]) * (NA_HD ** -0.5)
    nk = _group_rms(seg(OFF_NK, NA_W), g64_ref, kn_ref[...])
    nv = seg(OFF_NV, NA_W)
    dq = _group_rms(seg(OFF_DQ, DF_QK_W), g32_ref, dqn_ref[...])
    dk = _group_rms(seg(OFF_DK, DF_QK_W), g32_ref, dkn_ref[...])
    dv = seg(OFF_DV, DF_V_W)
    if latent:
        cos = cos_ref[...]
        sin = sin_ref[...]
        dq = _rope(dq, cos, sin)
        dk = _rope(dk, cos, sin)
    dq = dq * (DF_HD ** -0.5)

    for hh in range(NA_HEADS):
        sl = slice(hh * NA_HD, (hh + 1) * NA_HD)
        nq_ref[0, hh] = nq[:, sl].astype(nq_ref.dtype)
        nk_ref[0, hh] = nk[:, sl].astype(nk_ref.dtype)
        nv_ref[0, hh] = nv[:, sl].astype(nv_ref.dtype)
        dv_ref[0, hh] = dv[:, sl].astype(dv_ref.dtype)
    if latent:
        dq_ref[0] = dq.astype(dq_ref.dtype)
        dk_ref[0] = dk.astype(dk_ref.dtype)
    else:
        for hc in range(2 * DF_HEADS):
            sl = slice(hc * DF_HD, (hc + 1) * DF_HD)
            dq_ref[0, hc // 2, hc % 2] = dq[:, sl].astype(dq_ref.dtype)
            dk_ref[0, hc // 2, hc % 2] = dk[:, sl].astype(dk_ref.dtype)


def _group_matrix(width, group):
    idx = np.arange(width) // group
    return jnp.asarray((idx[:, None] == idx[None, :]).astype(np.float32) / group, dtype=BF16)


def _prep(latent, x, shift, scale, gain, w_in, qn, kn, dqn, dkn, rope_tabs, tm):
    b, t, _ = x.shape
    bm = shift.shape[0]
    mod_map = (lambda i, j: (i, 0, 0)) if bm > 1 else (lambda i, j: (0, 0, 0))
    const2 = lambda i, j: (0, 0)
    tile3 = lambda i, j: (i, j, 0)
    head4 = lambda i, j: (i, 0, j, 0)
    in_specs = [pl.BlockSpec((1, tm, D_MODEL), tile3),
                pl.BlockSpec((1, 1, D_MODEL), mod_map),
                pl.BlockSpec((1, 1, D_MODEL), mod_map),
                pl.BlockSpec((1, D_MODEL), const2),
                pl.BlockSpec((D_MODEL, D_IN), const2),
                pl.BlockSpec((1, NA_W), const2),
                pl.BlockSpec((1, NA_W), const2),
                pl.BlockSpec((1, DF_QK_W), const2),
                pl.BlockSpec((1, DF_QK_W), const2),
                pl.BlockSpec((NA_W, NA_W), const2),
                pl.BlockSpec((DF_QK_W, DF_QK_W), const2)]
    args = [x, shift, scale, gain.reshape(1, D_MODEL), w_in,
            jnp.tile(qn, NA_HEADS).reshape(1, NA_W), jnp.tile(kn, NA_HEADS).reshape(1, NA_W),
            jnp.tile(dqn, 2 * DF_HEADS).reshape(1, DF_QK_W), jnp.tile(dkn, 2 * DF_HEADS).reshape(1, DF_QK_W),
            _group_matrix(NA_W, NA_HD), _group_matrix(DF_QK_W, DF_HD)]
    head_shape = lambda dt: jax.ShapeDtypeStruct((b, NA_HEADS, t, NA_HD), dt)
    head_spec = pl.BlockSpec((1, NA_HEADS, tm, NA_HD), head4)
    out_shape = [jax.ShapeDtypeStruct((b, t, LRU_W), F32), jax.ShapeDtypeStruct((b, t, LRU_W), F32)]
    out_specs = [pl.BlockSpec((1, tm, LRU_W), tile3), pl.BlockSpec((1, tm, LRU_W), tile3)]
    if latent:
        in_specs += [pl.BlockSpec((tm, DF_QK_W), lambda i, j: (j, 0))] * 2
        args += list(rope_tabs)
        out_shape += [head_shape(BF16), head_shape(BF16), head_shape(BF16),
                      jax.ShapeDtypeStruct((b, t, DF_QK_W), BF16), jax.ShapeDtypeStruct((b, t, DF_QK_W), BF16),
                      head_shape(BF16)]
        out_specs += [head_spec, head_spec, head_spec,
                      pl.BlockSpec((1, tm, DF_QK_W), tile3), pl.BlockSpec((1, tm, DF_QK_W), tile3), head_spec]
    else:
        comp_shape = lambda dt: jax.ShapeDtypeStruct((b, DF_HEADS, 2, t, DF_HD), dt)
        comp_spec = pl.BlockSpec((1, DF_HEADS, 2, tm, DF_HD), lambda i, j: (i, 0, 0, j, 0))
        out_shape += [head_shape(BF16), head_shape(F32), head_shape(F32),
                      comp_shape(BF16), comp_shape(F32), head_shape(F32)]
        out_specs += [head_spec, head_spec, head_spec, comp_spec, comp_spec, head_spec]
    return pl.pallas_call(
        functools.partial(_prep_kernel, latent),
        out_shape=out_shape,
        grid=(b, t // tm),
        in_specs=in_specs,
        out_specs=out_specs,
        compiler_params=_cparams("arbitrary", "arbitrary"),
        name="prep_latent" if latent else "prep_context",
    )(*args)


def _rope_tables(t):
    half = DF_HD // 4
    freqs = ROPE_BASE ** (-jnp.arange(half, dtype=F32) / half)
    tok = jnp.arange(t)
    lane = np.arange(DF_QK_W)
    d = lane % DF_HD
    pos = jnp.where((d < DF_HD // 2)[None, :], (tok // GRID_W)[:, None], (tok % GRID_W)[:, None])
    ang = pos.astype(F32) * freqs[d % half][None, :]
    sign = np.where((d % (2 * half)) < half, -1.0, 1.0).astype(np.float32)
    return jnp.cos(ang), jnp.sin(ang) * sign[None, :]


def _block_scan(a, b, reverse):
    row = lax.broadcasted_iota(jnp.int32, a.shape, 0)
    for s in (1, 2, 4):
        if reverse:
            shift, m = SUBLANES - s, row < SUBLANES - s
        else:
            shift, m = s, row >= s
        a_s = pltpu.roll(a, shift, 0)
        b_s = pltpu.roll(b, shift, 0)
        b = jnp.where(m, a * b_s + b, b)
        a = jnp.where(m, a * a_s, a)
    return a, b


def _lru_kernel(t, rc, lx_ref, lg_ref, cw_ref, cb_ref, wg_ref, gb_ref, lam_ref, h0_ref,
                ya_ref, st_ref, xpad_ref, af_ref, bf_ref, ab_ref, bb_ref, hf_ref):
    ct = lx_ref.shape[-1]
    nblk = t // SUBLANES

    xpad_ref[0:HALO, :] = jnp.zeros((HALO, ct), F32)
    xpad_ref[HALO + t:, :] = jnp.zeros((HALO, ct), F32)
    xpad_ref[HALO:HALO + t, :] = lx_ref[0]

    lam = -lam_ref[...]
    sp = jnp.maximum(lam, 0.0) + jnp.log1p(jnp.exp(-jnp.abs(lam)))
    cw = cw_ref[...]
    cb = cb_ref[...]
    wg = wg_ref[0]
    gb = gb_ref[0]

    def gates(i, carry):
        r0 = pl.multiple_of(i * rc, rc)
        xw = xpad_ref[pl.ds(r0, rc + 2 * HALO), :]
        xc = cb
        for j in range(LRU_CONV):
            xc = xc + cw[j:j + 1, :] * xw[HALO - 1 + j:HALO - 1 + j + rc, :]
        g = _dot(xc.astype(BF16), wg) + gb
        for d, (a_ref, b_ref) in enumerate(((af_ref, bf_ref), (ab_ref, bb_ref))):
            r = _sigmoid(g[:, (2 * d) * ct:(2 * d + 1) * ct])
            ig = _sigmoid(g[:, (2 * d + 1) * ct:(2 * d + 2) * ct])
            log_a = (-LRU_C) * r * sp[d:d + 1, :]
            a = jnp.exp(log_a)
            a_ref[pl.ds(r0, rc), :] = a
            b_ref[pl.ds(r0, rc), :] = jnp.sqrt(_neg_expm1(2.0 * log_a, a * a)) * (ig * xc)
        return carry

    lax.fori_loop(0, t // rc, gates, 0)

    h0 = h0_ref[0]

    def fwd(i, carry):
        r0 = pl.multiple_of(i * SUBLANES, SUBLANES)
        a, b = _block_scan(af_ref[pl.ds(r0, SUBLANES), :], bf_ref[pl.ds(r0, SUBLANES), :], False)
        h = a * carry + b
        hf_ref[pl.ds(r0, SUBLANES), :] = h
        return h[SUBLANES - 1:SUBLANES, :]

    sf = lax.fori_loop(0, nblk, fwd, h0[0:1, :])

    def bwd(i, carry):
        r0 = pl.multiple_of((nblk - 1 - i) * SUBLANES, SUBLANES)
        a, b = _block_scan(ab_ref[pl.ds(r0, SUBLANES), :], bb_ref[pl.ds(r0, SUBLANES), :], True)
        h = a * carry + b
        y = (hf_ref[pl.ds(r0, SUBLANES), :] + h) * _gelu(lg_ref[0, pl.ds(r0, SUBLANES), :])
        ya_ref[0, pl.ds(r0, SUBLANES), :] = y.astype(ya_ref.dtype)
        return h[0:1, :]

    sb = lax.fori_loop(0, nblk, bwd, h0[1:2, :])
    st_ref[0, 0:1, :] = sf
    st_ref[0, 1:2, :] = sb


def _lru_gate_weights(wa, wx, ba, bx, ct):
    nct = LRU_W // ct
    per = ct // LRU_BW
    eye = jnp.eye(per, dtype=F32)

    def dense(w):
        wt = w.reshape(nct, per, LRU_BW, LRU_BW)
        return jnp.einsum('cpij,pq->cpiqj', wt, eye).reshape(nct, ct, ct)

    mats = [dense(wa[0]), dense(wx[0]), dense(wa[1]), dense(wx[1])]
    bias = [ba[0], bx[0], ba[1], bx[1]]
    wg = jnp.concatenate(mats, axis=-1).astype(BF16)
    gb = jnp.concatenate([v.reshape(nct, 1, ct) for v in bias], axis=-1)
    return wg, gb


def _lru(lx, lg, conv_w, conv_b, wa, wx, ba, bx, lam, h0, ct, rc):
    b, t, _ = lx.shape
    nct = LRU_W // ct
    wg, gb = _lru_gate_weights(wa, wx, ba, bx, ct)
    tile = lambda i, j: (i, 0, j)
    return pl.pallas_call(
        functools.partial(_lru_kernel, t, rc),
        out_shape=[jax.ShapeDtypeStruct((b, t, LRU_W), BF16), jax.ShapeDtypeStruct((b, 2, LRU_W), F32)],
        grid=(b, nct),
        in_specs=[pl.BlockSpec((1, t, ct), tile),
                  pl.BlockSpec((1, t, ct), tile),
                  pl.BlockSpec((LRU_CONV, ct), lambda i, j: (0, j)),
                  pl.BlockSpec((1, ct), lambda i, j: (0, j)),
                  pl.BlockSpec((1, ct, 4 * ct), lambda i, j: (j, 0, 0)),
                  pl.BlockSpec((1, 1, 4 * ct), lambda i, j: (j, 0, 0)),
                  pl.BlockSpec((2, ct), lambda i, j: (0, j)),
                  pl.BlockSpec((1, 2, ct), tile)],
        out_specs=[pl.BlockSpec((1, t, ct), tile), pl.BlockSpec((1, 2, ct), tile)],
        scratch_shapes=[pltpu.VMEM((t + 2 * HALO, ct), F32)] + [pltpu.VMEM((t, ct), F32)] * 5,
        compiler_params=_cparams("arbitrary", "arbitrary"),
        name="lru",
    )(lx, lg, conv_w, conv_b.reshape(1, LRU_W), wg, gb, lam, h0)


def _df_lambda(lv_ref, lam_init):
    lv = lv_ref[...]
    l1 = jnp.sum(lv[0:1, :] * lv[1:2, :], axis=-1, keepdims=True)
    l2 = jnp.sum(lv[2:3, :] * lv[3:4, :], axis=-1, keepdims=True)
    return jnp.exp(l1) - jnp.exp(l2) + lam_init


def _diff_combine(o1, o2, lam, subln, lam_init):
    y = o1 - lam * o2
    ms = jnp.mean(y * y, axis=-1, keepdims=True)
    return y * lax.rsqrt(ms + EPS) * subln * (1.0 - lam_init)


def _softmax_pv(s, v):
    m = jnp.max(s, axis=-1, keepdims=True)
    e = jnp.exp(s - m)
    l = jnp.sum(e, axis=-1, keepdims=True)
    return _dot(e.astype(BF16), v) / l


def _ctx_attn_kernel(lam_init, nq_ref, nk_ref, nv_ref, dq_ref, dk_ref, dv_ref, lv_ref, sub_ref, yb_ref, yc_ref):
    lam = _df_lambda(lv_ref, lam_init)
    sub = sub_ref[...]
    for h in range(NA_HEADS):
        s = _dot_nt(nq_ref[0, h], nk_ref[0, h].astype(BF16))
        o = _softmax_pv(s, nv_ref[0, h].astype(BF16))
        yb_ref[0, :, h * NA_HD:(h + 1) * NA_HD] = o.astype(yb_ref.dtype)
    for h in range(DF_HEADS):
        v = dv_ref[0, h].astype(BF16)
        o1 = _softmax_pv(_dot_nt(dq_ref[0, h, 0], dk_ref[0, h, 0].astype(BF16)), v)
        o2 = _softmax_pv(_dot_nt(dq_ref[0, h, 1], dk_ref[0, h, 1].astype(BF16)), v)
        y = _diff_combine(o1, o2, lam, sub, lam_init)
        yc_ref[0, :, h * DF_VD:(h + 1) * DF_VD] = y.astype(yc_ref.dtype)


def _ctx_attn(nq, nk, nv, dq, dk, dv, df_lambda, df_subln, lam_init):
    b, _, t, _ = nq.shape
    head = pl.BlockSpec((1, NA_HEADS, t, NA_HD), lambda i: (i, 0, 0, 0))
    comp = pl.BlockSpec((1, DF_HEADS, 2, t, DF_HD), lambda i: (i, 0, 0, 0, 0))
    out = pl.BlockSpec((1, t, NA_W), lambda i: (i, 0, 0))
    return pl.pallas_call(
        functools.partial(_ctx_attn_kernel, lam_init),
        out_shape=[jax.ShapeDtypeStruct((b, t, NA_W), BF16), jax.ShapeDtypeStruct((b, t, DF_V_W), BF16)],
        grid=(b,),
        in_specs=[head, head, head, comp, comp, head,
                  pl.BlockSpec((4, DF_HD), lambda i: (0, 0)),
                  pl.BlockSpec((1, DF_VD), lambda i: (0, 0))],
        out_specs=[out, out],
        compiler_params=_cparams("arbitrary"),
        name="ctx_attn",
    )(nq, nk, nv, dq, dk, dv, df_lambda, df_subln.reshape(1, DF_VD))


def _na_bias_tables(rpb, rows_n):
    nq = NA_QROWS * GRID_W
    nk = NA_KROWS * GRID_W
    tabs = []
    for r0 in (0, NA_QROWS, rows_n - NA_QROWS):
        kstart = int(np.clip(r0 - NA_ROWS // 2, 0, rows_n - NA_KROWS))
        r = r0 + np.arange(nq) // GRID_W
        qc = np.arange(nq) % GRID_W
        kr = kstart + np.arange(nk) // GRID_W
        kc = np.arange(nk) % GRID_W
        rs = np.clip(r - NA_ROWS // 2, 0, rows_n - NA_ROWS)
        ok_row = (kr[None, :] >= rs[:, None]) & (kr[None, :] < rs[:, None] + NA_ROWS)
        cs = np.clip(qc - NA_COLS // 2, 0, GRID_W - NA_COLS)
        ok_col = (kc[None, :] >= cs[:, None]) & (kc[None, :] < cs[:, None] + NA_COLS)
        d_row = np.clip(kr[None, :] - r[:, None] + NA_ROWS - 1, 0, 2 * NA_ROWS - 2)
        d_col = np.clip(kc[None, :] - qc[:, None] + NA_COLS - 1, 0, 2 * NA_COLS - 2)
        idx = d_row * (2 * NA_COLS - 1) + d_col
        ok = ok_row & ok_col
        flat = rpb.reshape(NA_HEADS, -1)
        tabs.append(jnp.where(jnp.asarray(ok)[None], jnp.take(flat, jnp.asarray(idx), axis=1), NEG_INF))
    return jnp.stack(tabs)


def _lat_na_kernel(rows_n, q_ref, k_ref, v_ref, ck_ref, cv_ref, bias_ref, o_ref):
    i = pl.program_id(1)
    kstart = jnp.clip(i * NA_QROWS - NA_ROWS // 2, 0, rows_n - NA_KROWS)
    k0 = pl.multiple_of(kstart * GRID_W, NA_QROWS * GRID_W)
    nk = NA_KROWS * GRID_W
    for h in range(NA_HEADS):
        q = q_ref[0, h]
        s_loc = _dot_nt(q, k_ref[0, h, pl.ds(k0, nk), :]) + bias_ref[0, h]
        s_ctx = _dot_nt(q, ck_ref[0, h])
        m = jnp.maximum(jnp.max(s_loc, axis=-1, keepdims=True), jnp.max(s_ctx, axis=-1, keepdims=True))
        e_loc = jnp.exp(s_loc - m)
        e_ctx = jnp.exp(s_ctx - m)
        l = jnp.sum(e_loc, axis=-1, keepdims=True) + jnp.sum(e_ctx, axis=-1, keepdims=True)
        o = _dot(e_loc.astype(BF16), v_ref[0, h, pl.ds(k0, nk), :]) + _dot(e_ctx.astype(BF16), cv_ref[0, h])
        o_ref[0, :, h * NA_HD:(h + 1) * NA_HD] = (o / l).astype(o_ref.dtype)


def _lat_na(q, k, v, ck, cv, bias_tabs):
    b, _, t, _ = q.shape
    rows_n = t // GRID_W
    nt = rows_n // NA_QROWS
    tq = NA_QROWS * GRID_W
    past = ck.shape[2]
    full = lambda n: pl.BlockSpec((1, NA_HEADS, n, NA_HD), lambda i, j: (i, 0, 0, 0))

    def kind(i, j):
        return (jnp.where(j == 0, 0, jnp.where(j == nt - 1, 2, 1)), 0, 0, 0)

    return pl.pallas_call(
        functools.partial(_lat_na_kernel, rows_n),
        out_shape=jax.ShapeDtypeStruct((b, t, NA_W), BF16),
        grid=(b, nt),
        in_specs=[pl.BlockSpec((1, NA_HEADS, tq, NA_HD), lambda i, j: (i, 0, j, 0)),
                  full(t), full(t), full(past), full(past),
                  pl.BlockSpec((1, NA_HEADS, tq, NA_KROWS * GRID_W), kind)],
        out_specs=pl.BlockSpec((1, tq, NA_W), lambda i, j: (i, j, 0)),
        compiler_params=_cparams("arbitrary", "arbitrary"),
        name="lat_na",
    )(q, k, v, ck, cv, bias_tabs)


def _lat_df_kernel(lam_init, nck, q_ref, kt_ref, v_ref, lv_ref, sub_ref, o_ref):
    tq = q_ref.shape[1]
    lam = _df_lambda(lv_ref, lam_init)
    sub = sub_ref[...]
    q = q_ref[0]
    for h in range(DF_HEADS):
        qs = [q[:, (2 * h + c) * DF_HD:(2 * h + c + 1) * DF_HD] for c in range(2)]

        def chunk(j, carry):
            v = v_ref[0, h, pl.ds(pl.multiple_of(j * DF_CK, DF_CK), DF_CK), :]
            new = []
            for c in range(2):
                m, l, acc = carry[c]
                s = _dot(qs[c], kt_ref[0, j, (2 * h + c) * DF_HD:(2 * h + c + 1) * DF_HD, :])
                m_new = jnp.maximum(m, jnp.max(s, axis=-1, keepdims=True))
                alpha = jnp.exp(m - m_new)
                e = jnp.exp(s - m_new)
                l = alpha * l + jnp.sum(e, axis=-1, keepdims=True)
                acc = alpha * acc + _dot(e.astype(BF16), v)
                new.append((m_new, l, acc))
            return tuple(new)

        init = tuple((jnp.full((tq, 1), NEG_INF, F32), jnp.zeros((tq, 1), F32), jnp.zeros((tq, DF_VD), F32))
                     for _ in range(2))
        (_, l1, a1), (_, l2, a2) = lax.fori_loop(0, nck, chunk, init)
        y = _diff_combine(a1 / l1, a2 / l2, lam, sub, lam_init)
        o_ref[0, :, h * DF_VD:(h + 1) * DF_VD] = y.astype(o_ref.dtype)


def _lat_df(q, kt, v, df_lambda, df_subln, lam_init, tq):
    b, t, _ = q.shape
    nck = kt.shape[1]
    tk = v.shape[2]
    return pl.pallas_call(
        functools.partial(_lat_df_kernel, lam_init, nck),
        out_shape=jax.ShapeDtypeStruct((b, t, DF_V_W), BF16),
        grid=(b, t // tq),
        in_specs=[pl.BlockSpec((1, tq, DF_QK_W), lambda i, j: (i, j, 0)),
                  pl.BlockSpec((1, nck, DF_QK_W, DF_CK), lambda i, j: (i, 0, 0, 0)),
                  pl.BlockSpec((1, DF_HEADS, tk, DF_VD), lambda i, j: (i, 0, 0, 0)),
                  pl.BlockSpec((4, DF_HD), lambda i, j: (0, 0)),
                  pl.BlockSpec((1, DF_VD), lambda i, j: (0, 0))],
        out_specs=pl.BlockSpec((1, tq, DF_V_W), lambda i, j: (i, j, 0)),
        compiler_params=_cparams("arbitrary", "arbitrary"),
        name="lat_df",
    )(q, kt, v, df_lambda, df_subln.reshape(1, DF_VD))


def _out_kernel(ya_ref, yb_ref, yc_ref, x_ref, ga_ref, w_ref, o_ref):
    y = (_dot(ya_ref[0], w_ref[0:LRU_W, :]) + _dot(yb_ref[0], w_ref[LRU_W:LRU_W + NA_W, :])
         + _dot(yc_ref[0], w_ref[LRU_W + NA_W:, :]))
    o_ref[0] = x_ref[0] + ga_ref[0] * y


def _out_proj(ya, yb, yc, x, g_a, w_out, tm):
    b, t, _ = x.shape
    mod_map = (lambda i, j: (i, 0, 0)) if g_a.shape[0] > 1 else (lambda i, j: (0, 0, 0))
    tile = lambda i, j: (i, j, 0)
    return pl.pallas_call(
        _out_kernel,
        out_shape=jax.ShapeDtypeStruct(x.shape, F32),
        grid=(b, t // tm),
        in_specs=[pl.BlockSpec((1, tm, LRU_W), tile), pl.BlockSpec((1, tm, NA_W), tile),
                  pl.BlockSpec((1, tm, DF_V_W), tile), pl.BlockSpec((1, tm, D_MODEL), tile),
                  pl.BlockSpec((1, 1, D_MODEL), mod_map),
                  pl.BlockSpec((D_CAT, D_MODEL), lambda i, j: (0, 0))],
        out_specs=pl.BlockSpec((1, tm, D_MODEL), tile),
        compiler_params=_cparams("arbitrary", "arbitrary"),
        name="out_proj",
    )(ya, yb, yc, x, g_a, w_out)


def _ffn_kernel(nt, xp_ref, x_ref, xn_ref, sh_ref, sc_ref, gf_ref, g_ref, up_ref, cw_ref, cb_ref, dn_ref,
                o_ref, acc_ref):
    j = pl.program_id(1)
    tm = x_ref.shape[1]
    x = x_ref[0]
    xw = jnp.concatenate([xp_ref[0], x, xn_ref[0]], axis=0)
    ms = jnp.mean(xw * xw, axis=-1, keepdims=True)
    y = xw * lax.rsqrt(ms + EPS) * g_ref[...]
    h = y * (1.0 + sc_ref[0]) + sh_ref[0]
    row = lax.broadcasted_iota(jnp.int32, (tm + 2 * HALO, 1), 0)
    valid = ((row >= HALO) | (j > 0)) & ((row < HALO + tm) | (j < nt - 1))
    h = jnp.where(valid, h, 0.0).astype(BF16)

    cw = cw_ref[...]
    cb = cb_ref[...]
    acc_ref[...] = jnp.zeros_like(acc_ref)
    for c in range(D_FF // FFN_CF):
        halves = []
        for off in (c * FFN_CF, D_FF + c * FFN_CF):
            u = _dot(h, up_ref[:, off:off + FFN_CF])
            conv = cb[:, off:off + FFN_CF]
            for k in range(3):
                conv = conv + cw[k:k + 1, off:off + FFN_CF] * u[HALO - 1 + k:HALO - 1 + k + tm, :]
            halves.append(conv)
        act = (_gelu(halves[1]) * halves[0]).astype(BF16)
        acc_ref[...] += _dot(act, dn_ref[c * FFN_CF:(c + 1) * FFN_CF, :])
    o_ref[0] = x + gf_ref[0] * acc_ref[...]


def _ffn(x, shift, scale, g_f, gain, up, conv_w, conv_b, down, tm):
    b, t, _ = x.shape
    nt = t // tm
    hb = tm // HALO
    mod_map = (lambda i, j: (i, 0, 0)) if shift.shape[0] > 1 else (lambda i, j: (0, 0, 0))
    const2 = lambda i, j: (0, 0)
    mod_spec = pl.BlockSpec((1, 1, D_MODEL), mod_map)
    return pl.pallas_call(
        functools.partial(_ffn_kernel, nt),
        out_shape=jax.ShapeDtypeStruct(x.shape, F32),
        grid=(b, nt),
        in_specs=[pl.BlockSpec((1, HALO, D_MODEL), lambda i, j: (i, jnp.maximum(j * hb - 1, 0), 0)),
                  pl.BlockSpec((1, tm, D_MODEL), lambda i, j: (i, j, 0)),
                  pl.BlockSpec((1, HALO, D_MODEL), lambda i, j: (i, jnp.minimum((j + 1) * hb, t // HALO - 1), 0)),
                  mod_spec, mod_spec, mod_spec,
                  pl.BlockSpec((1, D_MODEL), const2),
                  pl.BlockSpec((D_MODEL, 2 * D_FF), const2),
                  pl.BlockSpec((3, 2 * D_FF), const2),
                  pl.BlockSpec((1, 2 * D_FF), const2),
                  pl.BlockSpec((D_FF, D_MODEL), const2)],
        out_specs=pl.BlockSpec((1, tm, D_MODEL), lambda i, j: (i, j, 0)),
        scratch_shapes=[pltpu.VMEM((tm, D_MODEL), F32)],
        compiler_params=_cparams("arbitrary", "arbitrary"),
        name="ffn",
    )(x, x, x, shift, scale, g_f, gain.reshape(1, D_MODEL), up, conv_w, conv_b.reshape(1, 2 * D_FF), down)


def _split_mod(mod):
    return [mod[:, None, i * D_MODEL:(i + 1) * D_MODEL] for i in range(N_MOD)]


def kernel(x_prompt, x_sample, state_lru, cache_na_k, cache_na_v, cache_df_k, cache_df_v, c, c_ctx, ada_w, ada_b, norm_mix, norm_ffn, w_in, lru_conv_w, lru_conv_b, lru_wa, lru_ba, lru_wx, lru_bx, lru_lambda, na_q_norm, na_k_norm, na_rpb, df_q_norm, df_k_norm, df_lambda, df_subln, w_out, ffn_up, ffn_conv_w, ffn_conv_b, ffn_down):
    bp, tp, _ = x_prompt.shape
    bs, ts, _ = x_sample.shape
    past = cache_na_k.shape[3]
    rows_n = ts // GRID_W

    c_all = jnp.zeros((SUBLANES, D_MODEL), F32).at[0].set(c_ctx).at[1:1 + bs].set(c)
    mods = _modulation(c_all, ada_w, ada_b)
    rope_tabs = _rope_tables(ts)
    zeros_state = jnp.zeros((bp, 2, LRU_W), F32)

    y_p, y_s = x_prompt, x_sample
    s_lru, s_nk, s_nv, s_dk, s_dv = [], [], [], [], []
    for l in range(DEPTH):
        lam_init = 0.8 - 0.6 * math.exp(-0.3 * l)
        w_in_l = w_in[l].astype(BF16)
        w_out_l = w_out[l].astype(BF16)
        up_l = ffn_up[l].astype(BF16)
        down_l = ffn_down[l].astype(BF16)
        lru_args = (lru_conv_w[l], lru_conv_b[l], lru_wa[l], lru_wx[l], lru_ba[l], lru_bx[l], lru_lambda[l])
        norm_args = (na_q_norm[l], na_k_norm[l], df_q_norm[l], df_k_norm[l])

        sh_a, sc_a, g_a, sh_f, sc_f, g_f = _split_mod(mods[l, 0:1])
        lx, lg, nq, nk, nv, dq, dk, dv = _prep(False, y_p, sh_a, sc_a, norm_mix[l], w_in_l, *norm_args, None, tp)
        ya, st = _lru(lx, lg, *lru_args, zeros_state, LRU_W, tp)
        yb, yc = _ctx_attn(nq, nk, nv, dq, dk, dv, df_lambda[l], df_subln[l], lam_init)
        x1 = _out_proj(ya, yb, yc, y_p, g_a, w_out_l, tp)
        y_p = _ffn(x1, sh_f, sc_f, g_f, norm_ffn[l], up_l, ffn_conv_w[l], ffn_conv_b[l], down_l, tp)
        s_lru.append(st)
        s_nk.append(nk)
        s_nv.append(nv)
        s_dk.append(dk)
        s_dv.append(dv)

        sh_a, sc_a, g_a, sh_f, sc_f, g_f = _split_mod(mods[l, 1:1 + bs])
        lx, lg, nq, nk, nv, dq, dk, dv = _prep(True, y_s, sh_a, sc_a, norm_mix[l], w_in_l, *norm_args, rope_tabs, 512)
        ya, _ = _lru(lx, lg, *lru_args, state_lru[:, l], 128, 256)
        yb = _lat_na(nq, nk, nv, cache_na_k[:, l].astype(BF16), cache_na_v[:, l].astype(BF16),
                     _na_bias_tables(na_rpb[l], rows_n))
        kt_lat = jnp.swapaxes(dk, 1, 2)
        kt_ctx = jnp.swapaxes(cache_df_k[:, l].astype(BF16), -1, -2).reshape(bs, DF_QK_W, past)
        kt = jnp.concatenate([kt_lat, kt_ctx], axis=-1)
        nck = (ts + past) // DF_CK
        kt = kt.reshape(bs, DF_QK_W, nck, DF_CK).transpose(0, 2, 1, 3)
        vals = jnp.concatenate([dv, cache_df_v[:, l].astype(BF16)], axis=2)
        yc = _lat_df(dq, kt, vals, df_lambda[l], df_subln[l], lam_init, 256)
        x1 = _out_proj(ya, yb, yc, y_s, g_a, w_out_l, 512)
        y_s = _ffn(x1, sh_f, sc_f, g_f, norm_ffn[l], up_l, ffn_conv_w[l], ffn_conv_b[l], down_l, 512)

    return (y_p, y_s, jnp.stack(s_lru, axis=1), jnp.stack(s_nk, axis=1), jnp.stack(s_nv, axis=1),
            jnp.stack(s_dk, axis=1), jnp.stack(s_dv, axis=1))
```

```python
import functools
import math

import numpy as np
import jax
import jax.numpy as jnp
from jax import lax
from jax.experimental import pallas as pl
from jax.experimental.pallas import tpu as pltpu

F32 = jnp.float32
BF16 = jnp.bfloat16

D_MODEL = 1024
DEPTH = 2
GRID_W = 64
LRU_W = 512
LRU_BLOCKS = 8
LRU_BW = LRU_W // LRU_BLOCKS
LRU_CONV = 4
LRU_C = 8.0
NA_HEADS = 4
NA_HD = 64
NA_W = NA_HEADS * NA_HD
NA_ROWS = 8
NA_COLS = 16
DF_HEADS = 4
DF_HD = 32
DF_VD = 2 * DF_HD
DF_QK_W = DF_HEADS * 2 * DF_HD
DF_V_W = DF_HEADS * DF_VD
D_IN = 2 * LRU_W + 3 * NA_W + 2 * DF_QK_W + DF_V_W
D_CAT = LRU_W + NA_W + DF_V_W
D_FF = 2816
ROPE_BASE = 10000.0
EPS = 1e-6
NEG_INF = -1e30
N_MOD = 6

OFF_LX = 0
OFF_LG = OFF_LX + LRU_W
OFF_NQ = OFF_LG + LRU_W
OFF_NK = OFF_NQ + NA_W
OFF_NV = OFF_NK + NA_W
OFF_DQ = OFF_NV + NA_W
OFF_DK = OFF_DQ + DF_QK_W
OFF_DV = OFF_DK + DF_QK_W

SUBLANES = 8
VMEM_LIMIT = 56 * 1024 * 1024

NA_QROWS = 4
NA_KROWS = NA_QROWS + NA_ROWS
DF_CK = 512
DF_VA = 128
DF_SAFE_SCORE = 50.0
DF_UNROLL = 3
LRU_RC = 32
LRU_UNROLL = 8
FFN_CF = 256
HALO = SUBLANES


def _cparams(*sem):
    return pltpu.CompilerParams(dimension_semantics=sem, vmem_limit_bytes=VMEM_LIMIT)


def _gelu(x):
    return 0.5 * x * (1.0 + jnp.tanh(math.sqrt(2.0 / math.pi) * (x + 0.044715 * (x * x * x))))


def _sigmoid(x):
    return 1.0 / (1.0 + jnp.exp(-x))


def _neg_expm1(x, u):
    safe = u != 1.0
    return jnp.where(safe, (1.0 - u) * x / jnp.log(jnp.where(safe, u, 0.5)), -x)


def _dot(a, b):
    return jnp.dot(a, b, preferred_element_type=F32)


def _dot_nt(a, b):
    return lax.dot_general(a, b, (((1,), (1,)), ((), ())), preferred_element_type=F32)


def _mod_kernel(c_ref, w_ref, b_ref, o_ref):
    c = c_ref[...]
    s = (c * _sigmoid(c)).astype(BF16)
    o_ref[0] = _dot(s, w_ref[0].astype(BF16)) + b_ref[0]


def _modulation(c_all, ada_w, ada_b):
    nb = 1536
    n = N_MOD * D_MODEL
    return pl.pallas_call(
        _mod_kernel,
        out_shape=jax.ShapeDtypeStruct((DEPTH, SUBLANES, n), F32),
        grid=(DEPTH, n // nb),
        in_specs=[pl.BlockSpec((SUBLANES, D_MODEL), lambda l, j: (0, 0)),
                  pl.BlockSpec((1, D_MODEL, nb), lambda l, j: (l, 0, j)),
                  pl.BlockSpec((1, 1, nb), lambda l, j: (l, 0, j))],
        out_specs=pl.BlockSpec((1, SUBLANES, nb), lambda l, j: (l, 0, j)),
        compiler_params=_cparams("arbitrary", "arbitrary"),
        name="modulation",
    )(c_all, ada_w, ada_b.reshape(DEPTH, 1, n))


def _group_rms(z, g_ref, gain):
    ms = _dot((z * z).astype(BF16), g_ref[...])
    return z * lax.rsqrt(ms + EPS) * gain


def _rope(x, cos, sin):
    w = x.shape[-1]
    lane = lax.broadcasted_iota(jnp.int32, x.shape, 1)
    partner = jnp.where((lane % 16) < 8, pltpu.roll(x, w - 8, 1), pltpu.roll(x, 8, 1))
    return x * cos + partner * sin


def _prep_kernel(latent, x_ref, sh_ref, sc_ref, g_ref, w_ref, qn_ref, kn_ref, dqn_ref, dkn_ref,
                 g64_ref, g32_ref, *rest):
    if latent:
        cos_ref, sin_ref, lx_ref, lg_ref, nq_ref, nk_ref, nv_ref, dq_ref, dk_ref, dv_ref = rest
    else:
        lx_ref, lg_ref, nq_ref, nk_ref, nv_ref, dq_ref, dk_ref, dv_ref = rest
    x = x_ref[0]
    ms = jnp.mean(x * x, axis=-1, keepdims=True)
    y = x * lax.rsqrt(ms + EPS) * g_ref[...]
    h = (y * (1.0 + sc_ref[0]) + sh_ref[0]).astype(BF16)

    def seg(off, size):
        return _dot(h, w_ref[:, off:off + size])

    lx_ref[0] = seg(OFF_LX, LRU_W)
    lg_ref[0] = seg(OFF_LG, LRU_W)

    nq = _group_rms(seg(OFF_NQ, NA_W), g64_ref, qn_ref[...]) * (NA_HD ** -0.5)
    nk = _group_rms(seg(OFF_NK, NA_W), g64_ref, kn_ref[...])
    nv = seg(OFF_NV, NA_W)
    dq = _group_rms(seg(OFF_DQ, DF_QK_W), g32_ref, dqn_ref[...])
    dk = _group_rms(seg(OFF_DK, DF_QK_W), g32_ref, dkn_ref[...])
    dv = seg(OFF_DV, DF_V_W)
    if latent:
        cos = cos_ref[...]
        sin = sin_ref[...]
        dq = _rope(dq, cos, sin)
        dk = _rope(dk, cos, sin)
    dq = dq * (DF_HD ** -0.5)

    for hh in range(NA_HEADS):
        sl = slice(hh * NA_HD, (hh + 1) * NA_HD)
        nq_ref[0, hh] = nq[:, sl].astype(nq_ref.dtype)
        nk_ref[0, hh] = nk[:, sl].astype(nk_ref.dtype)
        nv_ref[0, hh] = nv[:, sl].astype(nv_ref.dtype)
        dv_ref[0, hh] = dv[:, sl].astype(dv_ref.dtype)
        if latent:
            dq_ref[0, hh] = dq[:, sl].astype(dq_ref.dtype)
    if latent:
        dk_ref[0] = dk.astype(dk_ref.dtype)
    else:
        for hc in range(2 * DF_HEADS):
            sl = slice(hc * DF_HD, (hc + 1) * DF_HD)
            dq_ref[0, hc // 2, hc % 2] = dq[:, sl].astype(dq_ref.dtype)
            dk_ref[0, hc // 2, hc % 2] = dk[:, sl].astype(dk_ref.dtype)


def _seq_split(latent, b, t, tm, dtype):
    if latent:
        assert tm * SUBLANES == t
        return (jax.ShapeDtypeStruct((b, tm, SUBLANES * LRU_W), dtype),
                pl.BlockSpec((1, tm, LRU_W), lambda i, j: (i, 0, j)))
    assert tm == t and b % SUBLANES == 0
    return (jax.ShapeDtypeStruct((b // SUBLANES, t, SUBLANES * LRU_W), dtype),
            pl.BlockSpec((1, tm, LRU_W), lambda i, j: (i // SUBLANES, 0, i % SUBLANES)))


def _group_matrix(width, group):
    idx = np.arange(width) // group
    return jnp.asarray((idx[:, None] == idx[None, :]).astype(np.float32) / group, dtype=BF16)


def _prep(latent, x, shift, scale, gain, w_in, qn, kn, dqn, dkn, rope_tabs, tm):
    b, t, _ = x.shape
    bm = shift.shape[0]
    mod_map = (lambda i, j: (i, 0, 0)) if bm > 1 else (lambda i, j: (0, 0, 0))
    const2 = lambda i, j: (0, 0)
    tile3 = lambda i, j: (i, j, 0)
    head4 = lambda i, j: (i, 0, j, 0)
    in_specs = [pl.BlockSpec((1, tm, D_MODEL), tile3),
                pl.BlockSpec((1, 1, D_MODEL), mod_map),
                pl.BlockSpec((1, 1, D_MODEL), mod_map),
                pl.BlockSpec((1, D_MODEL), const2),
                pl.BlockSpec((D_MODEL, D_IN), const2),
                pl.BlockSpec((1, NA_W), const2),
                pl.BlockSpec((1, NA_W), const2),
                pl.BlockSpec((1, DF_QK_W), const2),
                pl.BlockSpec((1, DF_QK_W), const2),
                pl.BlockSpec((NA_W, NA_W), const2),
                pl.BlockSpec((DF_QK_W, DF_QK_W), const2)]
    args = [x, shift, scale, gain.reshape(1, D_MODEL), w_in,
            jnp.tile(qn, NA_HEADS).reshape(1, NA_W), jnp.tile(kn, NA_HEADS).reshape(1, NA_W),
            jnp.tile(dqn, 2 * DF_HEADS).reshape(1, DF_QK_W), jnp.tile(dkn, 2 * DF_HEADS).reshape(1, DF_QK_W),
            _group_matrix(NA_W, NA_HD), _group_matrix(DF_QK_W, DF_HD)]
    head_shape = lambda dt: jax.ShapeDtypeStruct((b, NA_HEADS, t, NA_HD), dt)
    head_spec = pl.BlockSpec((1, NA_HEADS, tm, NA_HD), head4)
    seq_shape, seq_spec = _seq_split(latent, b, t, tm, F32)
    out_shape = [seq_shape, seq_shape]
    out_specs = [seq_spec, seq_spec]
    if latent:
        in_specs += [pl.BlockSpec((tm, DF_QK_W), lambda i, j: (j, 0))] * 2
        args += list(rope_tabs)
        out_shape += [head_shape(BF16), head_shape(BF16), head_shape(BF16),
                      head_shape(BF16), jax.ShapeDtypeStruct((b, t, DF_QK_W), BF16), head_shape(BF16)]
        out_specs += [head_spec, head_spec, head_spec,
                      head_spec, pl.BlockSpec((1, tm, DF_QK_W), tile3), head_spec]
    else:
        comp_shape = lambda dt: jax.ShapeDtypeStruct((b, DF_HEADS, 2, t, DF_HD), dt)
        comp_spec = pl.BlockSpec((1, DF_HEADS, 2, tm, DF_HD), lambda i, j: (i, 0, 0, j, 0))
        out_shape += [head_shape(BF16), head_shape(F32), head_shape(F32),
                      comp_shape(BF16), comp_shape(F32), head_shape(F32)]
        out_specs += [head_spec, head_spec, head_spec, comp_spec, comp_spec, head_spec]
    return pl.pallas_call(
        functools.partial(_prep_kernel, latent),
        out_shape=out_shape,
        grid=(b, t // tm),
        in_specs=in_specs,
        out_specs=out_specs,
        compiler_params=_cparams("arbitrary", "arbitrary"),
        name="prep_latent" if latent else "prep_context",
    )(*args)


def _rope_tables(t):
    half = DF_HD // 4
    freqs = ROPE_BASE ** (-jnp.arange(half, dtype=F32) / half)
    tok = jnp.arange(t)
    lane = np.arange(DF_QK_W)
    d = lane % DF_HD
    pos = jnp.where((d < DF_HD // 2)[None, :], (tok // GRID_W)[:, None], (tok % GRID_W)[:, None])
    ang = pos.astype(F32) * freqs[d % half][None, :]
    sign = np.where((d % (2 * half)) < half, -1.0, 1.0).astype(np.float32)
    return jnp.cos(ang), jnp.sin(ang) * sign[None, :]


def _lru_kernel(chained, jn, rc, lx_ref, lg_ref, cw_ref, cb_ref, wg_ref, gb_ref, lam_ref, h0_ref,
                ya_ref, st_ref, xpad_ref, af_ref, bf_ref, ab_ref, bb_ref, hf_ref):
    ct = lx_ref.shape[-1]
    row = lax.broadcasted_iota(jnp.int32, (SUBLANES, ct), 0)
    zero = jnp.zeros((SUBLANES, ct), F32)

    def from_prev_seq(v):
        return jnp.where(row >= 1, pltpu.roll(v, 1, 0), 0.0) if chained else zero

    def from_next_seq(v):
        return jnp.where(row < SUBLANES - 1, pltpu.roll(v, SUBLANES - 1, 0), 0.0) if chained else zero

    xpad_ref[0] = from_prev_seq(lx_ref[0, jn - 1])
    xpad_ref[1:jn + 1] = lx_ref[0]
    xpad_ref[jn + 1] = from_next_seq(lx_ref[0, 0])
    xpad_ref[jn + 2] = from_next_seq(lx_ref[0, 1])

    lam = -lam_ref[...]
    sp = jnp.maximum(lam, 0.0) + jnp.log1p(jnp.exp(-jnp.abs(lam)))
    cw = cw_ref[...]
    cb = cb_ref[...]
    wg = wg_ref[0]
    gb = gb_ref[0]

    def gates(i, carry):
        r0 = pl.multiple_of(i * rc, rc)
        xw = xpad_ref[pl.ds(r0, rc + LRU_CONV - 1)]
        xc = cb
        for j in range(LRU_CONV):
            xc = xc + cw[j:j + 1, :] * xw[j:j + rc]
        xc = xc.reshape(rc * SUBLANES, ct)
        g = _dot(xc.astype(BF16), wg) + gb
        for d, (a_ref, b_ref) in enumerate(((af_ref, bf_ref), (ab_ref, bb_ref))):
            r = _sigmoid(g[:, (2 * d) * ct:(2 * d + 1) * ct])
            ig = _sigmoid(g[:, (2 * d + 1) * ct:(2 * d + 2) * ct])
            log_a = (-LRU_C) * r * sp[d:d + 1, :]
            a = jnp.exp(log_a)
            b = jnp.sqrt(_neg_expm1(2.0 * log_a, a * a)) * (ig * xc)
            a_ref[pl.ds(r0, rc)] = a.reshape(rc, SUBLANES, ct)
            b_ref[pl.ds(r0, rc)] = b.reshape(rc, SUBLANES, ct)
        return carry

    lax.fori_loop(0, jn // rc, gates, 0)

    hf0 = h0_ref[0, 0]
    hb0 = h0_ref[0, 1]
    if chained:
        def summarise(j, carry):
            hf, pf, hb, pb = carry
            a = af_ref[j]
            jb = jn - 1 - j
            a2 = ab_ref[jb]
            return a * hf + bf_ref[j], a * pf, a2 * hb + bb_ref[jb], a2 * pb

        one = jnp.ones((SUBLANES, ct), F32)
        hf_end, pf_end, hb_end, pb_end = lax.fori_loop(0, jn, summarise, (zero, one, zero, one), unroll=LRU_UNROLL)
        for s in range(SUBLANES - 1):
            hf0 = jnp.where(row == s + 1, pltpu.roll(hf_end + pf_end * hf0, 1, 0), hf0)
            sb = SUBLANES - 2 - s
            hb0 = jnp.where(row == sb, pltpu.roll(hb_end + pb_end * hb0, SUBLANES - 1, 0), hb0)

    def fwd(j, h):
        h = af_ref[j] * h + bf_ref[j]
        hf_ref[j] = h
        return h

    st_ref[0, 0] = lax.fori_loop(0, jn, fwd, hf0, unroll=LRU_UNROLL)

    def bwd(j, h):
        jb = jn - 1 - j
        h = ab_ref[jb] * h + bb_ref[jb]
        hf_ref[jb] = (hf_ref[jb] + h) * _gelu(lg_ref[0, jb])
        return h

    st_ref[0, 1] = lax.fori_loop(0, jn, bwd, hb0, unroll=LRU_UNROLL)

    def emit(i, carry):
        r0 = pl.multiple_of(i * rc, rc)
        y = hf_ref[pl.ds(r0, rc)].reshape(rc * SUBLANES, ct)
        ya_ref[0, pl.ds(pl.multiple_of(r0 * SUBLANES, rc * SUBLANES), rc * SUBLANES), :] = y.astype(ya_ref.dtype)
        return carry

    lax.fori_loop(0, jn // rc, emit, 0)


def _lru_gate_weights(wa, wx, ba, bx, ct):
    nct = LRU_W // ct
    per = ct // LRU_BW
    eye = jnp.eye(per, dtype=F32)

    def dense(w):
        wt = w.reshape(nct, per, LRU_BW, LRU_BW)
        return jnp.einsum('cpij,pq->cpiqj', wt, eye).reshape(nct, ct, ct)

    mats = [dense(wa[0]), dense(wx[0]), dense(wa[1]), dense(wx[1])]
    bias = [ba[0], bx[0], ba[1], bx[1]]
    wg = jnp.concatenate(mats, axis=-1).astype(BF16)
    gb = jnp.concatenate([v.reshape(nct, 1, ct) for v in bias], axis=-1)
    return wg, gb


def _lru(chained, lx, lg, conv_w, conv_b, wa, wx, ba, bx, lam, h0, ct, rc):
    g, jn, _, _ = lx.shape
    nct = LRU_W // ct
    wg, gb = _lru_gate_weights(wa, wx, ba, bx, ct)
    tile = lambda i, j: (i, 0, 0, j)
    seq = (jn, SUBLANES, ct)
    return pl.pallas_call(
        functools.partial(_lru_kernel, chained, jn, rc),
        out_shape=[jax.ShapeDtypeStruct((g, jn * SUBLANES, LRU_W), BF16),
                   jax.ShapeDtypeStruct((g, 2, SUBLANES, LRU_W), F32)],
        grid=(g, nct),
        in_specs=[pl.BlockSpec((1,) + seq, tile),
                  pl.BlockSpec((1,) + seq, tile),
                  pl.BlockSpec((LRU_CONV, ct), lambda i, j: (0, j)),
                  pl.BlockSpec((1, ct), lambda i, j: (0, j)),
                  pl.BlockSpec((1, ct, 4 * ct), lambda i, j: (j, 0, 0)),
                  pl.BlockSpec((1, 1, 4 * ct), lambda i, j: (j, 0, 0)),
                  pl.BlockSpec((2, ct), lambda i, j: (0, j)),
                  pl.BlockSpec((1, 2, SUBLANES, ct), tile)],
        out_specs=[pl.BlockSpec((1, jn * SUBLANES, ct), lambda i, j: (i, 0, j)),
                   pl.BlockSpec((1, 2, SUBLANES, ct), tile)],
        scratch_shapes=[pltpu.VMEM((jn + LRU_CONV - 1, SUBLANES, ct), F32)] + [pltpu.VMEM(seq, F32)] * 5,
        compiler_params=_cparams("arbitrary", "arbitrary"),
        name="lru_latent" if chained else "lru_context",
    )(lx, lg, conv_w, conv_b.reshape(1, LRU_W), wg, gb, lam, h0)


def _df_lambda(lv_ref, lam_init):
    lv = lv_ref[...]
    l1 = jnp.sum(lv[0:1, :] * lv[1:2, :], axis=-1, keepdims=True)
    l2 = jnp.sum(lv[2:3, :] * lv[3:4, :], axis=-1, keepdims=True)
    return jnp.exp(l1) - jnp.exp(l2) + lam_init


def _diff_combine(o1, o2, lam, subln, lam_init):
    y = o1 - lam * o2
    ms = jnp.mean(y * y, axis=-1, keepdims=True)
    return y * lax.rsqrt(ms + EPS) * subln * (1.0 - lam_init)


def _softmax_pv(s, v):
    m = jnp.max(s, axis=-1, keepdims=True)
    e = jnp.exp(s - m)
    l = jnp.sum(e, axis=-1, keepdims=True)
    return _dot(e.astype(BF16), v) / l


def _ctx_attn_kernel(lam_init, nq_ref, nk_ref, nv_ref, dq_ref, dk_ref, dv_ref, lv_ref, sub_ref, yb_ref, yc_ref):
    lam = _df_lambda(lv_ref, lam_init)
    sub = sub_ref[...]
    for h in range(NA_HEADS):
        s = _dot_nt(nq_ref[0, h], nk_ref[0, h].astype(BF16))
        o = _softmax_pv(s, nv_ref[0, h].astype(BF16))
        yb_ref[0, :, h * NA_HD:(h + 1) * NA_HD] = o.astype(yb_ref.dtype)
    for h in range(DF_HEADS):
        v = dv_ref[0, h].astype(BF16)
        o1 = _softmax_pv(_dot_nt(dq_ref[0, h, 0], dk_ref[0, h, 0].astype(BF16)), v)
        o2 = _softmax_pv(_dot_nt(dq_ref[0, h, 1], dk_ref[0, h, 1].astype(BF16)), v)
        y = _diff_combine(o1, o2, lam, sub, lam_init)
        yc_ref[0, :, h * DF_VD:(h + 1) * DF_VD] = y.astype(yc_ref.dtype)


def _ctx_attn(nq, nk, nv, dq, dk, dv, df_lambda, df_subln, lam_init):
    b, _, t, _ = nq.shape
    head = pl.BlockSpec((1, NA_HEADS, t, NA_HD), lambda i: (i, 0, 0, 0))
    comp = pl.BlockSpec((1, DF_HEADS, 2, t, DF_HD), lambda i: (i, 0, 0, 0, 0))
    out = pl.BlockSpec((1, t, NA_W), lambda i: (i, 0, 0))
    return pl.pallas_call(
        functools.partial(_ctx_attn_kernel, lam_init),
        out_shape=[jax.ShapeDtypeStruct((b, t, NA_W), BF16), jax.ShapeDtypeStruct((b, t, DF_V_W), BF16)],
        grid=(b,),
        in_specs=[head, head, head, comp, comp, head,
                  pl.BlockSpec((4, DF_HD), lambda i: (0, 0)),
                  pl.BlockSpec((1, DF_VD), lambda i: (0, 0))],
        out_specs=[out, out],
        compiler_params=_cparams("arbitrary"),
        name="ctx_attn",
    )(nq, nk, nv, dq, dk, dv, df_lambda, df_subln.reshape(1, DF_VD))


def _na_bias_tables(rpb, rows_n):
    n_dr, n_dc = 2 * NA_ROWS - 1, 2 * NA_COLS - 1
    qc = np.arange(GRID_W)[:, None]
    kc = np.arange(GRID_W)[None, :]
    cs = np.clip(qc - NA_COLS // 2, 0, GRID_W - NA_COLS)
    ok_col = (kc >= cs) & (kc < cs + NA_COLS)
    d_col = np.clip(kc - qc + NA_COLS - 1, 0, n_dc - 1)
    sel_col = (d_col[None] == np.arange(n_dc)[:, None, None]) & ok_col[None]
    sel_row = np.zeros((3, NA_QROWS, NA_KROWS, n_dr), np.float32)
    for kind, r0 in enumerate((0, NA_QROWS, rows_n - NA_QROWS)):
        kstart = int(np.clip(r0 - NA_ROWS // 2, 0, rows_n - NA_KROWS))
        r = r0 + np.arange(NA_QROWS)[:, None]
        kr = kstart + np.arange(NA_KROWS)[None, :]
        rs = np.clip(r - NA_ROWS // 2, 0, rows_n - NA_ROWS)
        ok_row = (kr >= rs) & (kr < rs + NA_ROWS)
        d_row = np.clip(kr - r + NA_ROWS - 1, 0, n_dr - 1)
        sel_row[kind] = (d_row[..., None] == np.arange(n_dr)) & ok_row[..., None]
    ok = (sel_row.sum(-1) > 0)[:, :, None, :, None] & ok_col[None, None, :, None, :]
    hi = lax.Precision.HIGHEST
    cols = jnp.einsum('had,dqk->haqk', rpb, jnp.asarray(sel_col, F32), precision=hi)
    tabs = jnp.einsum('tija,haqk->thiqjk', jnp.asarray(sel_row), cols, precision=hi)
    tabs = jnp.where(jnp.asarray(ok)[:, None], tabs, NEG_INF)
    return tabs.reshape(3, NA_HEADS, NA_QROWS * GRID_W, NA_KROWS * GRID_W)


def _lat_na_kernel(rows_n, q_ref, k_ref, v_ref, ck_ref, cv_ref, bias_ref, o_ref):
    i = pl.program_id(1)
    kstart = jnp.clip(i * NA_QROWS - NA_ROWS // 2, 0, rows_n - NA_KROWS)
    k0 = pl.multiple_of(kstart * GRID_W, NA_QROWS * GRID_W)
    nk = NA_KROWS * GRID_W
    for h in range(NA_HEADS):
        q = q_ref[0, h]
        s_loc = _dot_nt(q, k_ref[0, h, pl.ds(k0, nk), :]) + bias_ref[0, h]
        s_ctx = _dot_nt(q, ck_ref[0, h])
        m = jnp.maximum(jnp.max(s_loc, axis=-1, keepdims=True), jnp.max(s_ctx, axis=-1, keepdims=True))
        e_loc = jnp.exp(s_loc - m)
        e_ctx = jnp.exp(s_ctx - m)
        l = jnp.sum(e_loc, axis=-1, keepdims=True) + jnp.sum(e_ctx, axis=-1, keepdims=True)
        o = _dot(e_loc.astype(BF16), v_ref[0, h, pl.ds(k0, nk), :]) + _dot(e_ctx.astype(BF16), cv_ref[0, h])
        o_ref[0, :, h * NA_HD:(h + 1) * NA_HD] = (o / l).astype(o_ref.dtype)


def _lat_na(q, k, v, ck, cv, bias_tabs):
    b, _, t, _ = q.shape
    rows_n = t // GRID_W
    nt = rows_n // NA_QROWS
    tq = NA_QROWS * GRID_W
    past = ck.shape[2]
    full = lambda n: pl.BlockSpec((1, NA_HEADS, n, NA_HD), lambda i, j: (i, 0, 0, 0))

    def kind(i, j):
        return (jnp.where(j == 0, 0, jnp.where(j == nt - 1, 2, 1)), 0, 0, 0)

    return pl.pallas_call(
        functools.partial(_lat_na_kernel, rows_n),
        out_shape=jax.ShapeDtypeStruct((b, t, NA_W), BF16),
        grid=(b, nt),
        in_specs=[pl.BlockSpec((1, NA_HEADS, tq, NA_HD), lambda i, j: (i, 0, j, 0)),
                  full(t), full(t), full(past), full(past),
                  pl.BlockSpec((1, NA_HEADS, tq, NA_KROWS * GRID_W), kind)],
        out_specs=pl.BlockSpec((1, tq, NA_W), lambda i, j: (i, j, 0)),
        compiler_params=_cparams("arbitrary", "arbitrary"),
        name="lat_na",
    )(q, k, v, ck, cv, bias_tabs)


def _lat_df_kernel(lam_init, nck, q_ref, kt_ref, v_ref, lv_ref, sub_ref, o_ref, kn_ref):
    tq = q_ref.shape[2]
    lam = _df_lambda(lv_ref, lam_init)
    q = q_ref[0, 0]
    qs = [q[:, c * DF_HD:(c + 1) * DF_HD] for c in range(2)]
    rows = [slice(c * DF_HD, (c + 1) * DF_HD) for c in range(2)]

    @pl.when(pl.program_id(2) == 0)
    def _():
        for c in range(2):
            def longest(j, mx):
                k = kt_ref[0, j, rows[c], :].astype(F32)
                return jnp.maximum(mx, jnp.sum(k * k, axis=0, keepdims=True))

            mx = lax.fori_loop(0, nck, longest, jnp.zeros((1, DF_CK), F32))
            kn_ref[c:c + 1, :] = jnp.broadcast_to(jnp.max(mx, axis=-1, keepdims=True), (1, kn_ref.shape[1]))

    bound = jnp.zeros((1, 1), F32)
    for c in range(2):
        qf = qs[c].astype(F32)
        qn = jnp.max(jnp.sum(qf * qf, axis=-1, keepdims=True), axis=0, keepdims=True)
        bound = jnp.maximum(bound, qn * kn_ref[c:c + 1, 0:1])
    safe = jnp.max(bound) <= DF_SAFE_SCORE ** 2

    def attend(fast):
        def trip(t, carry):
            carry = list(carry)
            for u in range(DF_UNROLL):
                j = t * DF_UNROLL + u
                v = v_ref[0, 0, pl.ds(pl.multiple_of(j * DF_CK, DF_CK), DF_CK), :]
                for c in range(2):
                    s = _dot(qs[c], kt_ref[0, j, rows[c], :])
                    if fast:
                        carry[c] = carry[c] + _dot(jnp.exp(s).astype(BF16), v)
                    else:
                        m, acc = carry[c]
                        m_new = jnp.maximum(m, jnp.max(s, axis=-1, keepdims=True))
                        acc = jnp.exp(m - m_new) * acc + _dot(jnp.exp(s - m_new).astype(BF16), v)
                        carry[c] = (m_new, acc)
            return tuple(carry)

        zero = jnp.zeros((tq, DF_VA), F32)
        if fast:
            a1, a2 = lax.fori_loop(0, nck // DF_UNROLL, trip, (zero, zero))
        else:
            m0 = jnp.full((tq, 1), NEG_INF, F32)
            (_, a1), (_, a2) = lax.fori_loop(0, nck // DF_UNROLL, trip, ((m0, zero), (m0, zero)))
        o1 = a1[:, 0:DF_VD] / a1[:, DF_VD:DF_VD + 1]
        o2 = a2[:, 0:DF_VD] / a2[:, DF_VD:DF_VD + 1]
        o_ref[0, 0] = _diff_combine(o1, o2, lam, sub_ref[...], lam_init).astype(o_ref.dtype)

    @pl.when(safe)
    def _():
        attend(True)

    @pl.when(jnp.logical_not(safe))
    def _():
        attend(False)


def _lat_df(q, kt, v, df_lambda, df_subln, lam_init, tq):
    b, _, t, _ = q.shape
    nck = kt.shape[1]
    tk = v.shape[2]
    assert nck % DF_UNROLL == 0
    return pl.pallas_call(
        functools.partial(_lat_df_kernel, lam_init, nck),
        out_shape=jax.ShapeDtypeStruct((b, DF_HEADS, t, DF_VD), BF16),
        grid=(b, DF_HEADS, t // tq),
        in_specs=[pl.BlockSpec((1, 1, tq, 2 * DF_HD), lambda i, h, j: (i, h, j, 0)),
                  pl.BlockSpec((1, nck, 2 * DF_HD, DF_CK), lambda i, h, j: (i, 0, h, 0)),
                  pl.BlockSpec((1, 1, tk, DF_VA), lambda i, h, j: (i, h, 0, 0)),
                  pl.BlockSpec((4, DF_HD), lambda i, h, j: (0, 0)),
                  pl.BlockSpec((1, DF_VD), lambda i, h, j: (0, 0))],
        out_specs=pl.BlockSpec((1, 1, tq, DF_VD), lambda i, h, j: (i, h, j, 0)),
        scratch_shapes=[pltpu.VMEM((SUBLANES, 128), F32)],
        compiler_params=_cparams("arbitrary", "arbitrary", "arbitrary"),
        name="lat_df",
    )(q, kt, v, df_lambda, df_subln.reshape(1, DF_VD))


def _with_ones_column(v, width):
    pad = width - v.shape[-1] - 1
    return jnp.concatenate([v, jnp.ones(v.shape[:-1] + (1,), v.dtype), jnp.zeros(v.shape[:-1] + (pad,), v.dtype)],
                           axis=-1)


def _out_kernel(ya_ref, yb_ref, yc_ref, x_ref, ga_ref, w_ref, o_ref):
    if len(yc_ref.shape) == 4:
        yc = jnp.concatenate([yc_ref[0, h] for h in range(DF_HEADS)], axis=-1)
    else:
        yc = yc_ref[0]
    y = (_dot(ya_ref[0], w_ref[0:LRU_W, :]) + _dot(yb_ref[0], w_ref[LRU_W:LRU_W + NA_W, :])
         + _dot(yc, w_ref[LRU_W + NA_W:, :]))
    o_ref[0] = x_ref[0] + ga_ref[0] * y


def _out_proj(latent, ya, yb, yc, x, g_a, w_out, tm):
    b, t, _ = x.shape
    _, ya_spec = _seq_split(latent, b, t, tm, BF16)
    ya = ya.reshape(ya.shape[0], ya.shape[1] // SUBLANES, SUBLANES * LRU_W)
    mod_map = (lambda i, j: (i, 0, 0)) if g_a.shape[0] > 1 else (lambda i, j: (0, 0, 0))
    tile = lambda i, j: (i, j, 0)
    if yc.ndim == 4:
        yc_spec = pl.BlockSpec((1, DF_HEADS, tm, DF_VD), lambda i, j: (i, 0, j, 0))
    else:
        yc_spec = pl.BlockSpec((1, tm, DF_V_W), tile)
    return pl.pallas_call(
        _out_kernel,
        out_shape=jax.ShapeDtypeStruct(x.shape, F32),
        grid=(b, t // tm),
        in_specs=[ya_spec, pl.BlockSpec((1, tm, NA_W), tile),
                  yc_spec, pl.BlockSpec((1, tm, D_MODEL), tile),
                  pl.BlockSpec((1, 1, D_MODEL), mod_map),
                  pl.BlockSpec((D_CAT, D_MODEL), lambda i, j: (0, 0))],
        out_specs=pl.BlockSpec((1, tm, D_MODEL), tile),
        compiler_params=_cparams("arbitrary", "arbitrary"),
        name="out_proj",
    )(ya, yb, yc, x, g_a, w_out)


def _ffn_kernel(nt, xp_ref, x_ref, xn_ref, sh_ref, sc_ref, gf_ref, g_ref, up_ref, cw_ref, cb_ref, dn_ref,
                o_ref, acc_ref):
    j = pl.program_id(1)
    tm = x_ref.shape[1]
    x = x_ref[0]
    xw = jnp.concatenate([xp_ref[0], x, xn_ref[0]], axis=0)
    ms = jnp.mean(xw * xw, axis=-1, keepdims=True)
    y = xw * lax.rsqrt(ms + EPS) * g_ref[...]
    h = y * (1.0 + sc_ref[0]) + sh_ref[0]
    row = lax.broadcasted_iota(jnp.int32, (tm + 2 * HALO, 1), 0)
    valid = ((row >= HALO) | (j > 0)) & ((row < HALO + tm) | (j < nt - 1))
    h = jnp.where(valid, h, 0.0).astype(BF16)

    cw = cw_ref[...]
    cb = cb_ref[...]
    acc_ref[...] = jnp.zeros_like(acc_ref)
    for c in range(D_FF // FFN_CF):
        halves = []
        for off in (c * FFN_CF, D_FF + c * FFN_CF):
            u = _dot(h, up_ref[:, off:off + FFN_CF])
            conv = cb[:, off:off + FFN_CF]
            for k in range(3):
                conv = conv + cw[k:k + 1, off:off + FFN_CF] * u[HALO - 1 + k:HALO - 1 + k + tm, :]
            halves.append(conv)
        act = (_gelu(halves[1]) * halves[0]).astype(BF16)
        acc_ref[...] += _dot(act, dn_ref[c * FFN_CF:(c + 1) * FFN_CF, :])
    o_ref[0] = x + gf_ref[0] * acc_ref[...]


def _ffn(x, shift, scale, g_f, gain, up, conv_w, conv_b, down, tm):
    b, t, _ = x.shape
    nt = t // tm
    hb = tm // HALO
    mod_map = (lambda i, j: (i, 0, 0)) if shift.shape[0] > 1 else (lambda i, j: (0, 0, 0))
    const2 = lambda i, j: (0, 0)
    mod_spec = pl.BlockSpec((1, 1, D_MODEL), mod_map)
    return pl.pallas_call(
        functools.partial(_ffn_kernel, nt),
        out_shape=jax.ShapeDtypeStruct(x.shape, F32),
        grid=(b, nt),
        in_specs=[pl.BlockSpec((1, HALO, D_MODEL), lambda i, j: (i, jnp.maximum(j * hb - 1, 0), 0)),
                  pl.BlockSpec((1, tm, D_MODEL), lambda i, j: (i, j, 0)),
                  pl.BlockSpec((1, HALO, D_MODEL), lambda i, j: (i, jnp.minimum((j + 1) * hb, t // HALO - 1), 0)),
                  mod_spec, mod_spec, mod_spec,
                  pl.BlockSpec((1, D_MODEL), const2),
                  pl.BlockSpec((D_MODEL, 2 * D_FF), const2),
                  pl.BlockSpec((3, 2 * D_FF), const2),
                  pl.BlockSpec((1, 2 * D_FF), const2),
                  pl.BlockSpec((D_FF, D_MODEL), const2)],
        out_specs=pl.BlockSpec((1, tm, D_MODEL), lambda i, j: (i, j, 0)),
        scratch_shapes=[pltpu.VMEM((tm, D_MODEL), F32)],
        compiler_params=_cparams("arbitrary", "arbitrary"),
        name="ffn",
    )(x, x, x, shift, scale, g_f, gain.reshape(1, D_MODEL), up, conv_w, conv_b.reshape(1, 2 * D_FF), down)


def _split_mod(mod):
    return [mod[:, None, i * D_MODEL:(i + 1) * D_MODEL] for i in range(N_MOD)]


def kernel(x_prompt, x_sample, state_lru, cache_na_k, cache_na_v, cache_df_k, cache_df_v, c, c_ctx, ada_w, ada_b, norm_mix, norm_ffn, w_in, lru_conv_w, lru_conv_b, lru_wa, lru_ba, lru_wx, lru_bx, lru_lambda, na_q_norm, na_k_norm, na_rpb, df_q_norm, df_k_norm, df_lambda, df_subln, w_out, ffn_up, ffn_conv_w, ffn_conv_b, ffn_down):
    bp, tp, _ = x_prompt.shape
    bs, ts, _ = x_sample.shape
    past = cache_na_k.shape[3]
    rows_n = ts // GRID_W

    c_all = jnp.zeros((SUBLANES, D_MODEL), F32).at[0].set(c_ctx).at[1:1 + bs].set(c)
    mods = _modulation(c_all, ada_w, ada_b)
    rope_tabs = _rope_tables(ts)
    zeros_state = jnp.zeros((bp // SUBLANES, 2, SUBLANES, LRU_W), F32)

    y_p, y_s = x_prompt, x_sample
    s_lru, s_nk, s_nv, s_dk, s_dv = [], [], [], [], []
    for l in range(DEPTH):
        lam_init = 0.8 - 0.6 * math.exp(-0.3 * l)
        w_in_l = w_in[l].astype(BF16)
        w_out_l = w_out[l].astype(BF16)
        up_l = ffn_up[l].astype(BF16)
        down_l = ffn_down[l].astype(BF16)
        lru_args = (lru_conv_w[l], lru_conv_b[l], lru_wa[l], lru_wx[l], lru_ba[l], lru_bx[l], lru_lambda[l])
        norm_args = (na_q_norm[l], na_k_norm[l], df_q_norm[l], df_k_norm[l])

        sh_a, sc_a, g_a, sh_f, sc_f, g_f = _split_mod(mods[l, 0:1])
        lx, lg, nq, nk, nv, dq, dk, dv = _prep(False, y_p, sh_a, sc_a, norm_mix[l], w_in_l, *norm_args, None, tp)
        seq4 = lambda v: v.reshape(v.shape[0], v.shape[1], SUBLANES, LRU_W)
        ya, st = _lru(False, seq4(lx), seq4(lg), *lru_args, zeros_state, 256, LRU_RC)
        yb, yc = _ctx_attn(nq, nk, nv, dq, dk, dv, df_lambda[l], df_subln[l], lam_init)
        x1 = _out_proj(False, ya, yb, yc, y_p, g_a, w_out_l, tp)
        y_p = _ffn(x1, sh_f, sc_f, g_f, norm_ffn[l], up_l, ffn_conv_w[l], ffn_conv_b[l], down_l, tp)
        s_lru.append(st.transpose(0, 2, 1, 3).reshape(bp, 2, LRU_W))
        s_nk.append(nk)
        s_nv.append(nv)
        s_dk.append(dk)
        s_dv.append(dv)

        sh_a, sc_a, g_a, sh_f, sc_f, g_f = _split_mod(mods[l, 1:1 + bs])
        lx, lg, nq, nk, nv, dq, dk, dv = _prep(True, y_s, sh_a, sc_a, norm_mix[l], w_in_l, *norm_args, rope_tabs,
                                               ts // SUBLANES)
        h0 = jnp.broadcast_to(state_lru[:, l, :, None, :], (bs, 2, SUBLANES, LRU_W))
        ya, _ = _lru(True, seq4(lx), seq4(lg), *lru_args, h0, 128, LRU_RC)
        yb = _lat_na(nq, nk, nv, cache_na_k[:, l].astype(BF16), cache_na_v[:, l].astype(BF16),
                     _na_bias_tables(na_rpb[l], rows_n))
        kt_lat = jnp.swapaxes(dk, 1, 2)
        kt_ctx = jnp.swapaxes(cache_df_k[:, l].astype(BF16), -1, -2).reshape(bs, DF_QK_W, past)
        kt = jnp.concatenate([kt_lat, kt_ctx], axis=-1)
        nck = (ts + past) // DF_CK
        kt = kt.reshape(bs, DF_QK_W, nck, DF_CK).transpose(0, 2, 1, 3)
        vals = _with_ones_column(jnp.concatenate([dv, cache_df_v[:, l].astype(BF16)], axis=2), DF_VA)
        yc = _lat_df(dq, kt, vals, df_lambda[l], df_subln[l], lam_init, 256)
        x1 = _out_proj(True, ya, yb, yc, y_s, g_a, w_out_l, ts // SUBLANES)
        y_s = _ffn(x1, sh_f, sc_f, g_f, norm_ffn[l], up_l, ffn_conv_w[l], ffn_conv_b[l], down_l, 512)

    return (y_p, y_s, jnp.stack(s_lru, axis=1), jnp.stack(s_nk, axis=1), jnp.stack(s_nv, axis=1),
            jnp.stack(s_dk, axis=1), jnp.stack(s_dv, axis=1))
```

```python
import functools
import math

import numpy as np
import jax
import jax.numpy as jnp
from jax import lax
from jax.experimental import pallas as pl
from jax.experimental.pallas import tpu as pltpu

F32 = jnp.float32
BF16 = jnp.bfloat16

D_MODEL = 1024
DEPTH = 2
GRID_W = 64
LRU_W = 512
LRU_BLOCKS = 8
LRU_BW = LRU_W // LRU_BLOCKS
LRU_CONV = 4
LRU_C = 8.0
NA_HEADS = 4
NA_HD = 64
NA_W = NA_HEADS * NA_HD
NA_ROWS = 8
NA_COLS = 16
DF_HEADS = 4
DF_HD = 32
DF_VD = 2 * DF_HD
DF_QK_W = DF_HEADS * 2 * DF_HD
DF_V_W = DF_HEADS * DF_VD
D_IN = 2 * LRU_W + 3 * NA_W + 2 * DF_QK_W + DF_V_W
D_CAT = LRU_W + NA_W + DF_V_W
D_FF = 2816
ROPE_BASE = 10000.0
EPS = 1e-6
NEG_INF = -1e30
N_MOD = 6

OFF_LX = 0
OFF_LG = OFF_LX + LRU_W
OFF_NQ = OFF_LG + LRU_W
OFF_NK = OFF_NQ + NA_W
OFF_NV = OFF_NK + NA_W
OFF_DQ = OFF_NV + NA_W
OFF_DK = OFF_DQ + DF_QK_W
OFF_DV = OFF_DK + DF_QK_W

SUBLANES = 8
VMEM_LIMIT = 56 * 1024 * 1024

NA_QROWS = 4
NA_KROWS = NA_QROWS + NA_ROWS
DF_CK = 512
DF_TQ = 512
DF_VA = 128
DF_SAFE_SCORE = 50.0
DF_UNROLL = 4
LRU_RC = 32
LRU_UNROLL = 8
FFN_CF = 256
HALO = SUBLANES


def _cparams(*sem):
    return pltpu.CompilerParams(dimension_semantics=sem, vmem_limit_bytes=VMEM_LIMIT)


def _gelu(x):
    return 0.5 * x * (1.0 + jnp.tanh(math.sqrt(2.0 / math.pi) * (x + 0.044715 * (x * x * x))))


def _sigmoid(x):
    return 1.0 / (1.0 + jnp.exp(-x))


def _neg_expm1(x, u):
    safe = u != 1.0
    return jnp.where(safe, (1.0 - u) * x / jnp.log(jnp.where(safe, u, 0.5)), -x)


def _dot(a, b):
    return jnp.dot(a, b, preferred_element_type=F32)


def _dot_nt(a, b):
    return lax.dot_general(a, b, (((1,), (1,)), ((), ())), preferred_element_type=F32)


def _mod_kernel(c_ref, w_ref, b_ref, o_ref):
    c = c_ref[...]
    s = (c * _sigmoid(c)).astype(BF16)
    o_ref[0] = _dot(s, w_ref[0].astype(BF16)) + b_ref[0]


def _modulation(c_all, ada_w, ada_b):
    nb = 1536
    n = N_MOD * D_MODEL
    return pl.pallas_call(
        _mod_kernel,
        out_shape=jax.ShapeDtypeStruct((DEPTH, SUBLANES, n), F32),
        grid=(DEPTH, n // nb),
        in_specs=[pl.BlockSpec((SUBLANES, D_MODEL), lambda l, j: (0, 0)),
                  pl.BlockSpec((1, D_MODEL, nb), lambda l, j: (l, 0, j)),
                  pl.BlockSpec((1, 1, nb), lambda l, j: (l, 0, j))],
        out_specs=pl.BlockSpec((1, SUBLANES, nb), lambda l, j: (l, 0, j)),
        compiler_params=_cparams("arbitrary", "arbitrary"),
        name="modulation",
    )(c_all, ada_w, ada_b.reshape(DEPTH, 1, n))


def _group_rms(z, g_ref, gain):
    ms = _dot((z * z).astype(BF16), g_ref[...])
    return z * lax.rsqrt(ms + EPS) * gain


def _rope(x, cos, sin):
    w = x.shape[-1]
    lane = lax.broadcasted_iota(jnp.int32, x.shape, 1)
    partner = jnp.where((lane % 16) < 8, pltpu.roll(x, w - 8, 1), pltpu.roll(x, 8, 1))
    return x * cos + partner * sin


def _prep_kernel(latent, x_ref, sh_ref, sc_ref, g_ref, w_ref, qn_ref, kn_ref, dqn_ref, dkn_ref,
                 g64_ref, g32_ref, *rest):
    if latent:
        cos_ref, sin_ref, lx_ref, lg_ref, nq_ref, nk_ref, nv_ref, dq_ref, dk_ref, dv_ref = rest
    else:
        lx_ref, lg_ref, nq_ref, nk_ref, nv_ref, dq_ref, dk_ref, dv_ref = rest
    x = x_ref[0]
    ms = jnp.mean(x * x, axis=-1, keepdims=True)
    y = x * lax.rsqrt(ms + EPS) * g_ref[...]
    h = (y * (1.0 + sc_ref[0]) + sh_ref[0]).astype(BF16)

    def seg(off, size):
        return _dot(h, w_ref[:, off:off + size])

    lx_ref[0] = seg(OFF_LX, LRU_W)
    lg_ref[0] = seg(OFF_LG, LRU_W)

    nq = _group_rms(seg(OFF_NQ, NA_W), g64_ref, qn_ref[...]) * (NA_HD ** -0.5)
    nk = _group_rms(seg(OFF_NK, NA_W), g64_ref, kn_ref[...])
    nv = seg(OFF_NV, NA_W)
    dq = _group_rms(seg(OFF_DQ, DF_QK_W), g32_ref, dqn_ref[...])
    dk = _group_rms(seg(OFF_DK, DF_QK_W), g32_ref, dkn_ref[...])
    dv = seg(OFF_DV, DF_V_W)
    if latent:
        cos = cos_ref[...]
        sin = sin_ref[...]
        dq = _rope(dq, cos, sin)
        dk = _rope(dk, cos, sin)
    dq = dq * (DF_HD ** -0.5)

    for hh in range(NA_HEADS):
        sl = slice(hh * NA_HD, (hh + 1) * NA_HD)
        nq_ref[0, hh] = nq[:, sl].astype(nq_ref.dtype)
        nk_ref[0, hh] = nk[:, sl].astype(nk_ref.dtype)
        nv_ref[0, hh] = nv[:, sl].astype(nv_ref.dtype)
        if latent:
            dq_ref[0, hh] = dq[:, sl].astype(dq_ref.dtype)
            lane = lax.broadcasted_iota(jnp.int32, (dv.shape[0], DF_VA - DF_VD), 1)
            dv_ref[0, hh, :, 0:DF_VD] = dv[:, sl].astype(dv_ref.dtype)
            dv_ref[0, hh, :, DF_VD:] = jnp.where(lane == 0, 1.0, 0.0).astype(dv_ref.dtype)
        else:
            dv_ref[0, hh] = dv[:, sl].astype(dv_ref.dtype)
    if latent:
        dk_ref[0, 0] = dk.T.astype(dk_ref.dtype)
    else:
        for hc in range(2 * DF_HEADS):
            sl = slice(hc * DF_HD, (hc + 1) * DF_HD)
            dq_ref[0, hc // 2, hc % 2] = dq[:, sl].astype(dq_ref.dtype)
            dk_ref[0, hc // 2, hc % 2] = dk[:, sl].astype(dk_ref.dtype)


def _seq_split(latent, b, t, tm, dtype):
    if latent:
        assert tm * SUBLANES == t
        return (jax.ShapeDtypeStruct((b, tm, SUBLANES * LRU_W), dtype),
                pl.BlockSpec((1, tm, LRU_W), lambda i, j: (i, 0, j)))
    assert tm == t and b % SUBLANES == 0
    return (jax.ShapeDtypeStruct((b // SUBLANES, t, SUBLANES * LRU_W), dtype),
            pl.BlockSpec((1, tm, LRU_W), lambda i, j: (i // SUBLANES, 0, i % SUBLANES)))


def _group_matrix(width, group):
    idx = np.arange(width) // group
    return jnp.asarray((idx[:, None] == idx[None, :]).astype(np.float32) / group, dtype=BF16)


def _prep(latent, x, shift, scale, gain, w_in, qn, kn, dqn, dkn, rope_tabs, tm):
    b, t, _ = x.shape
    bm = shift.shape[0]
    mod_map = (lambda i, j: (i, 0, 0)) if bm > 1 else (lambda i, j: (0, 0, 0))
    const2 = lambda i, j: (0, 0)
    tile3 = lambda i, j: (i, j, 0)
    head4 = lambda i, j: (i, 0, j, 0)
    in_specs = [pl.BlockSpec((1, tm, D_MODEL), tile3),
                pl.BlockSpec((1, 1, D_MODEL), mod_map),
                pl.BlockSpec((1, 1, D_MODEL), mod_map),
                pl.BlockSpec((1, D_MODEL), const2),
                pl.BlockSpec((D_MODEL, D_IN), const2),
                pl.BlockSpec((1, NA_W), const2),
                pl.BlockSpec((1, NA_W), const2),
                pl.BlockSpec((1, DF_QK_W), const2),
                pl.BlockSpec((1, DF_QK_W), const2),
                pl.BlockSpec((NA_W, NA_W), const2),
                pl.BlockSpec((DF_QK_W, DF_QK_W), const2)]
    args = [x, shift, scale, gain.reshape(1, D_MODEL), w_in,
            jnp.tile(qn, NA_HEADS).reshape(1, NA_W), jnp.tile(kn, NA_HEADS).reshape(1, NA_W),
            jnp.tile(dqn, 2 * DF_HEADS).reshape(1, DF_QK_W), jnp.tile(dkn, 2 * DF_HEADS).reshape(1, DF_QK_W),
            _group_matrix(NA_W, NA_HD), _group_matrix(DF_QK_W, DF_HD)]
    head_shape = lambda dt: jax.ShapeDtypeStruct((b, NA_HEADS, t, NA_HD), dt)
    head_spec = pl.BlockSpec((1, NA_HEADS, tm, NA_HD), head4)
    seq_shape, seq_spec = _seq_split(latent, b, t, tm, F32)
    out_shape = [seq_shape, seq_shape]
    out_specs = [seq_spec, seq_spec]
    if latent:
        in_specs += [pl.BlockSpec((tm, DF_QK_W), lambda i, j: (j, 0))] * 2
        args += list(rope_tabs)
        out_shape += [head_shape(BF16), head_shape(BF16), head_shape(BF16),
                      head_shape(BF16), jax.ShapeDtypeStruct((b, t // tm, DF_QK_W, tm), BF16),
                      jax.ShapeDtypeStruct((b, DF_HEADS, t, DF_VA), BF16)]
        out_specs += [head_spec, head_spec, head_spec,
                      head_spec, pl.BlockSpec((1, 1, DF_QK_W, tm), lambda i, j: (i, j, 0, 0)),
                      pl.BlockSpec((1, DF_HEADS, tm, DF_VA), head4)]
    else:
        comp_shape = lambda dt: jax.ShapeDtypeStruct((b, DF_HEADS, 2, t, DF_HD), dt)
        comp_spec = pl.BlockSpec((1, DF_HEADS, 2, tm, DF_HD), lambda i, j: (i, 0, 0, j, 0))
        out_shape += [head_shape(BF16), head_shape(F32), head_shape(F32),
                      comp_shape(BF16), comp_shape(F32), head_shape(F32)]
        out_specs += [head_spec, head_spec, head_spec, comp_spec, comp_spec, head_spec]
    return pl.pallas_call(
        functools.partial(_prep_kernel, latent),
        out_shape=out_shape,
        grid=(b, t // tm),
        in_specs=in_specs,
        out_specs=out_specs,
        compiler_params=_cparams("arbitrary", "arbitrary"),
        name="prep_latent" if latent else "prep_context",
    )(*args)


def _rope_tables(t):
    half = DF_HD // 4
    freqs = ROPE_BASE ** (-jnp.arange(half, dtype=F32) / half)
    tok = jnp.arange(t)
    lane = np.arange(DF_QK_W)
    d = lane % DF_HD
    pos = jnp.where((d < DF_HD // 2)[None, :], (tok // GRID_W)[:, None], (tok % GRID_W)[:, None])
    ang = pos.astype(F32) * freqs[d % half][None, :]
    sign = np.where((d % (2 * half)) < half, -1.0, 1.0).astype(np.float32)
    return jnp.cos(ang), jnp.sin(ang) * sign[None, :]


def _lru_kernel(chained, jn, rc, lx_ref, lg_ref, cw_ref, cb_ref, wg_ref, gb_ref, lam_ref, h0_ref,
                ya_ref, st_ref, xpad_ref, af_ref, bf_ref, ab_ref, bb_ref, hf_ref):
    ct = lx_ref.shape[-1]
    row = lax.broadcasted_iota(jnp.int32, (SUBLANES, ct), 0)
    zero = jnp.zeros((SUBLANES, ct), F32)

    def from_prev_seq(v):
        return jnp.where(row >= 1, pltpu.roll(v, 1, 0), 0.0) if chained else zero

    def from_next_seq(v):
        return jnp.where(row < SUBLANES - 1, pltpu.roll(v, SUBLANES - 1, 0), 0.0) if chained else zero

    xpad_ref[0] = from_prev_seq(lx_ref[0, jn - 1])
    xpad_ref[1:jn + 1] = lx_ref[0]
    xpad_ref[jn + 1] = from_next_seq(lx_ref[0, 0])
    xpad_ref[jn + 2] = from_next_seq(lx_ref[0, 1])

    lam = -lam_ref[...]
    sp = jnp.maximum(lam, 0.0) + jnp.log1p(jnp.exp(-jnp.abs(lam)))
    cw = cw_ref[...]
    cb = cb_ref[...]
    wg = wg_ref[0]
    gb = gb_ref[0]

    def gates(i, carry):
        r0 = pl.multiple_of(i * rc, rc)
        xw = xpad_ref[pl.ds(r0, rc + LRU_CONV - 1)]
        xc = cb
        for j in range(LRU_CONV):
            xc = xc + cw[j:j + 1, :] * xw[j:j + rc]
        xc = xc.reshape(rc * SUBLANES, ct)
        g = _dot(xc.astype(BF16), wg) + gb
        for d, (a_ref, b_ref) in enumerate(((af_ref, bf_ref), (ab_ref, bb_ref))):
            r = _sigmoid(g[:, (2 * d) * ct:(2 * d + 1) * ct])
            ig = _sigmoid(g[:, (2 * d + 1) * ct:(2 * d + 2) * ct])
            log_a = (-LRU_C) * r * sp[d:d + 1, :]
            a = jnp.exp(log_a)
            b = jnp.sqrt(_neg_expm1(2.0 * log_a, a * a)) * (ig * xc)
            a_ref[pl.ds(r0, rc)] = a.reshape(rc, SUBLANES, ct)
            b_ref[pl.ds(r0, rc)] = b.reshape(rc, SUBLANES, ct)
        return carry

    lax.fori_loop(0, jn // rc, gates, 0)

    hf0 = h0_ref[0, 0]
    hb0 = h0_ref[0, 1]
    if chained:
        def summarise(j, carry):
            hf, pf, hb, pb = carry
            a = af_ref[j]
            jb = jn - 1 - j
            a2 = ab_ref[jb]
            return a * hf + bf_ref[j], a * pf, a2 * hb + bb_ref[jb], a2 * pb

        one = jnp.ones((SUBLANES, ct), F32)
        hf_end, pf_end, hb_end, pb_end = lax.fori_loop(0, jn, summarise, (zero, one, zero, one), unroll=LRU_UNROLL)
        for s in range(SUBLANES - 1):
            hf0 = jnp.where(row == s + 1, pltpu.roll(hf_end + pf_end * hf0, 1, 0), hf0)
            sb = SUBLANES - 2 - s
            hb0 = jnp.where(row == sb, pltpu.roll(hb_end + pb_end * hb0, SUBLANES - 1, 0), hb0)

    def fwd(j, h):
        h = af_ref[j] * h + bf_ref[j]
        hf_ref[j] = h
        return h

    st_ref[0, 0] = lax.fori_loop(0, jn, fwd, hf0, unroll=LRU_UNROLL)

    def bwd(j, h):
        jb = jn - 1 - j
        h = ab_ref[jb] * h + bb_ref[jb]
        hf_ref[jb] = (hf_ref[jb] + h) * _gelu(lg_ref[0, jb])
        return h

    st_ref[0, 1] = lax.fori_loop(0, jn, bwd, hb0, unroll=LRU_UNROLL)

    def emit(i, carry):
        r0 = pl.multiple_of(i * rc, rc)
        y = hf_ref[pl.ds(r0, rc)].reshape(rc * SUBLANES, ct)
        ya_ref[0, pl.ds(pl.multiple_of(r0 * SUBLANES, rc * SUBLANES), rc * SUBLANES), :] = y.astype(ya_ref.dtype)
        return carry

    lax.fori_loop(0, jn // rc, emit, 0)


def _lru_gate_weights(wa, wx, ba, bx, ct):
    nct = LRU_W // ct
    per = ct // LRU_BW
    eye = jnp.eye(per, dtype=F32)

    def dense(w):
        wt = w.reshape(nct, per, LRU_BW, LRU_BW)
        return jnp.einsum('cpij,pq->cpiqj', wt, eye).reshape(nct, ct, ct)

    mats = [dense(wa[0]), dense(wx[0]), dense(wa[1]), dense(wx[1])]
    bias = [ba[0], bx[0], ba[1], bx[1]]
    wg = jnp.concatenate(mats, axis=-1).astype(BF16)
    gb = jnp.concatenate([v.reshape(nct, 1, ct) for v in bias], axis=-1)
    return wg, gb


def _lru(chained, lx, lg, conv_w, conv_b, wa, wx, ba, bx, lam, h0, ct, rc):
    g, jn, _, _ = lx.shape
    nct = LRU_W // ct
    wg, gb = _lru_gate_weights(wa, wx, ba, bx, ct)
    tile = lambda i, j: (i, 0, 0, j)
    seq = (jn, SUBLANES, ct)
    return pl.pallas_call(
        functools.partial(_lru_kernel, chained, jn, rc),
        out_shape=[jax.ShapeDtypeStruct((g, jn * SUBLANES, LRU_W), BF16),
                   jax.ShapeDtypeStruct((g, 2, SUBLANES, LRU_W), F32)],
        grid=(g, nct),
        in_specs=[pl.BlockSpec((1,) + seq, tile),
                  pl.BlockSpec((1,) + seq, tile),
                  pl.BlockSpec((LRU_CONV, ct), lambda i, j: (0, j)),
                  pl.BlockSpec((1, ct), lambda i, j: (0, j)),
                  pl.BlockSpec((1, ct, 4 * ct), lambda i, j: (j, 0, 0)),
                  pl.BlockSpec((1, 1, 4 * ct), lambda i, j: (j, 0, 0)),
                  pl.BlockSpec((2, ct), lambda i, j: (0, j)),
                  pl.BlockSpec((1, 2, SUBLANES, ct), tile)],
        out_specs=[pl.BlockSpec((1, jn * SUBLANES, ct), lambda i, j: (i, 0, j)),
                   pl.BlockSpec((1, 2, SUBLANES, ct), tile)],
        scratch_shapes=[pltpu.VMEM((jn + LRU_CONV - 1, SUBLANES, ct), F32)] + [pltpu.VMEM(seq, F32)] * 5,
        compiler_params=_cparams("arbitrary", "arbitrary"),
        name="lru_latent" if chained else "lru_context",
    )(lx, lg, conv_w, conv_b.reshape(1, LRU_W), wg, gb, lam, h0)


def _df_lambda(lv_ref, lam_init):
    lv = lv_ref[...]
    l1 = jnp.sum(lv[0:1, :] * lv[1:2, :], axis=-1, keepdims=True)
    l2 = jnp.sum(lv[2:3, :] * lv[3:4, :], axis=-1, keepdims=True)
    return jnp.exp(l1) - jnp.exp(l2) + lam_init


def _diff_combine(o1, o2, lam, subln, lam_init):
    y = o1 - lam * o2
    ms = jnp.mean(y * y, axis=-1, keepdims=True)
    return y * lax.rsqrt(ms + EPS) * subln * (1.0 - lam_init)


def _softmax_pv(s, v):
    m = jnp.max(s, axis=-1, keepdims=True)
    e = jnp.exp(s - m)
    l = jnp.sum(e, axis=-1, keepdims=True)
    return _dot(e.astype(BF16), v) / l


def _ctx_attn_kernel(lam_init, nq_ref, nk_ref, nv_ref, dq_ref, dk_ref, dv_ref, lv_ref, sub_ref, yb_ref, yc_ref):
    lam = _df_lambda(lv_ref, lam_init)
    sub = sub_ref[...]
    for h in range(NA_HEADS):
        s = _dot_nt(nq_ref[0, h], nk_ref[0, h].astype(BF16))
        o = _softmax_pv(s, nv_ref[0, h].astype(BF16))
        yb_ref[0, :, h * NA_HD:(h + 1) * NA_HD] = o.astype(yb_ref.dtype)
    for h in range(DF_HEADS):
        v = dv_ref[0, h].astype(BF16)
        o1 = _softmax_pv(_dot_nt(dq_ref[0, h, 0], dk_ref[0, h, 0].astype(BF16)), v)
        o2 = _softmax_pv(_dot_nt(dq_ref[0, h, 1], dk_ref[0, h, 1].astype(BF16)), v)
        y = _diff_combine(o1, o2, lam, sub, lam_init)
        yc_ref[0, :, h * DF_VD:(h + 1) * DF_VD] = y.astype(yc_ref.dtype)


def _ctx_attn(nq, nk, nv, dq, dk, dv, df_lambda, df_subln, lam_init):
    b, _, t, _ = nq.shape
    head = pl.BlockSpec((1, NA_HEADS, t, NA_HD), lambda i: (i, 0, 0, 0))
    comp = pl.BlockSpec((1, DF_HEADS, 2, t, DF_HD), lambda i: (i, 0, 0, 0, 0))
    out = pl.BlockSpec((1, t, NA_W), lambda i: (i, 0, 0))
    return pl.pallas_call(
        functools.partial(_ctx_attn_kernel, lam_init),
        out_shape=[jax.ShapeDtypeStruct((b, t, NA_W), BF16), jax.ShapeDtypeStruct((b, t, DF_V_W), BF16)],
        grid=(b,),
        in_specs=[head, head, head, comp, comp, head,
                  pl.BlockSpec((4, DF_HD), lambda i: (0, 0)),
                  pl.BlockSpec((1, DF_VD), lambda i: (0, 0))],
        out_specs=[out, out],
        compiler_params=_cparams("arbitrary"),
        name="ctx_attn",
    )(nq, nk, nv, dq, dk, dv, df_lambda, df_subln.reshape(1, DF_VD))


def _na_bias_tables(rpb, rows_n):
    n_dr, n_dc = 2 * NA_ROWS - 1, 2 * NA_COLS - 1
    qc = np.arange(GRID_W)[:, None]
    kc = np.arange(GRID_W)[None, :]
    cs = np.clip(qc - NA_COLS // 2, 0, GRID_W - NA_COLS)
    ok_col = (kc >= cs) & (kc < cs + NA_COLS)
    d_col = np.clip(kc - qc + NA_COLS - 1, 0, n_dc - 1)
    sel_col = (d_col[None] == np.arange(n_dc)[:, None, None]) & ok_col[None]
    sel_row = np.zeros((3, NA_QROWS, NA_KROWS, n_dr), np.float32)
    for kind, r0 in enumerate((0, NA_QROWS, rows_n - NA_QROWS)):
        kstart = int(np.clip(r0 - NA_ROWS // 2, 0, rows_n - NA_KROWS))
        r = r0 + np.arange(NA_QROWS)[:, None]
        kr = kstart + np.arange(NA_KROWS)[None, :]
        rs = np.clip(r - NA_ROWS // 2, 0, rows_n - NA_ROWS)
        ok_row = (kr >= rs) & (kr < rs + NA_ROWS)
        d_row = np.clip(kr - r + NA_ROWS - 1, 0, n_dr - 1)
        sel_row[kind] = (d_row[..., None] == np.arange(n_dr)) & ok_row[..., None]
    ok = (sel_row.sum(-1) > 0)[:, :, None, :, None] & ok_col[None, None, :, None, :]
    hi = lax.Precision.HIGHEST
    cols = jnp.einsum('had,dqk->haqk', rpb, jnp.asarray(sel_col, F32), precision=hi)
    tabs = jnp.einsum('tija,haqk->thiqjk', jnp.asarray(sel_row), cols, precision=hi)
    tabs = jnp.where(jnp.asarray(ok)[:, None], tabs, NEG_INF)
    return tabs.reshape(3, NA_HEADS, NA_QROWS * GRID_W, NA_KROWS * GRID_W)


def _lat_na_kernel(rows_n, q_ref, k_ref, v_ref, ck_ref, cv_ref, bias_ref, o_ref):
    i = pl.program_id(1)
    kstart = jnp.clip(i * NA_QROWS - NA_ROWS // 2, 0, rows_n - NA_KROWS)
    k0 = pl.multiple_of(kstart * GRID_W, NA_QROWS * GRID_W)
    nk = NA_KROWS * GRID_W
    for h in range(NA_HEADS):
        q = q_ref[0, h]
        s_loc = _dot_nt(q, k_ref[0, h, pl.ds(k0, nk), :]) + bias_ref[0, h]
        s_ctx = _dot_nt(q, ck_ref[0, h])
        m = jnp.maximum(jnp.max(s_loc, axis=-1, keepdims=True), jnp.max(s_ctx, axis=-1, keepdims=True))
        e_loc = jnp.exp(s_loc - m)
        e_ctx = jnp.exp(s_ctx - m)
        l = jnp.sum(e_loc, axis=-1, keepdims=True) + jnp.sum(e_ctx, axis=-1, keepdims=True)
        o = _dot(e_loc.astype(BF16), v_ref[0, h, pl.ds(k0, nk), :]) + _dot(e_ctx.astype(BF16), cv_ref[0, h])
        o_ref[0, :, h * NA_HD:(h + 1) * NA_HD] = (o / l).astype(o_ref.dtype)


def _lat_na(q, k, v, ck, cv, bias_tabs):
    b, _, t, _ = q.shape
    rows_n = t // GRID_W
    nt = rows_n // NA_QROWS
    tq = NA_QROWS * GRID_W
    past = ck.shape[2]
    full = lambda n: pl.BlockSpec((1, NA_HEADS, n, NA_HD), lambda i, j: (i, 0, 0, 0))

    def kind(i, j):
        return (jnp.where(j == 0, 0, jnp.where(j == nt - 1, 2, 1)), 0, 0, 0)

    return pl.pallas_call(
        functools.partial(_lat_na_kernel, rows_n),
        out_shape=jax.ShapeDtypeStruct((b, t, NA_W), BF16),
        grid=(b, nt),
        in_specs=[pl.BlockSpec((1, NA_HEADS, tq, NA_HD), lambda i, j: (i, 0, j, 0)),
                  full(t), full(t), full(past), full(past),
                  pl.BlockSpec((1, NA_HEADS, tq, NA_KROWS * GRID_W), kind)],
        out_specs=pl.BlockSpec((1, tq, NA_W), lambda i, j: (i, j, 0)),
        compiler_params=_cparams("arbitrary", "arbitrary"),
        name="lat_na",
    )(q, k, v, ck, cv, bias_tabs)


def _lat_df_kernel(lam_init, q_ref, qall_ref, ktl_ref, vl_ref, ktc_ref, vc_ref, lv_ref, sub_ref, o_ref, safe_ref):
    tq = q_ref.shape[2]
    n_lat = ktl_ref.shape[1]
    n_ctx = ktc_ref.shape[1]
    lam = _df_lambda(lv_ref, lam_init)
    q = q_ref[0, 0]
    qs = [q[:, c * DF_HD:(c + 1) * DF_HD] for c in range(2)]
    rows = [slice(c * DF_HD, (c + 1) * DF_HD) for c in range(2)]

    @pl.when(pl.program_id(2) == 0)
    def _():
        bound = jnp.zeros((1, 1), F32)
        for c in range(2):
            mx = jnp.zeros((1, DF_CK), F32)
            for kt_ref, n in ((ktl_ref, n_lat), (ktc_ref, n_ctx)):
                def longest(j, mx):
                    k = kt_ref[0, j, rows[c], :].astype(F32)
                    return jnp.maximum(mx, jnp.sum(k * k, axis=0, keepdims=True))

                mx = lax.fori_loop(0, n, longest, mx)
            kn = jnp.max(mx, axis=-1, keepdims=True)
            qa = qall_ref[0, 0, :, rows[c]].astype(F32)
            qn = jnp.max(jnp.sum(qa * qa, axis=-1, keepdims=True), axis=0, keepdims=True)
            bound = jnp.maximum(bound, qn * kn)
        safe_ref[0] = (jnp.max(bound) <= DF_SAFE_SCORE ** 2).astype(jnp.int32)

    safe = safe_ref[0] == 1

    def finish(a1, a2):
        o1 = a1[:, 0:DF_VD] / a1[:, DF_VD:DF_VD + 1]
        o2 = a2[:, 0:DF_VD] / a2[:, DF_VD:DF_VD + 1]
        o_ref[0, 0] = _diff_combine(o1, o2, lam, sub_ref[...], lam_init).astype(o_ref.dtype)

    def over_chunks(step, carry):
        def trip(t, carry):
            for u in range(DF_UNROLL):
                j = t * DF_UNROLL + u
                v = vl_ref[0, 0, pl.ds(pl.multiple_of(j * DF_CK, DF_CK), DF_CK), :]
                carry = step(carry, lambda c: ktl_ref[0, j, rows[c], :], v)
            return carry

        carry = lax.fori_loop(0, n_lat // DF_UNROLL, trip, carry)
        for j in range(n_ctx):
            carry = step(carry, lambda c: ktc_ref[0, j, rows[c], :], vc_ref[0, 0, j * DF_CK:(j + 1) * DF_CK, :])
        return carry

    @pl.when(safe)
    def _():
        def step(acc, kt, v):
            e = [jnp.exp(_dot(qs[c], kt(c))).astype(BF16) for c in range(2)]
            return acc + _dot(jnp.concatenate(e, axis=0), v)

        acc = over_chunks(step, jnp.zeros((2 * tq, DF_VA), F32))
        finish(acc[0:tq], acc[tq:])

    @pl.when(jnp.logical_not(safe))
    def _():
        def step(carry, kt, v):
            new = []
            for c in range(2):
                m, acc = carry[c]
                s = _dot(qs[c], kt(c))
                m_new = jnp.maximum(m, jnp.max(s, axis=-1, keepdims=True))
                new.append((m_new, jnp.exp(m - m_new) * acc + _dot(jnp.exp(s - m_new).astype(BF16), v)))
            return tuple(new)

        init = (jnp.full((tq, 1), NEG_INF, F32), jnp.zeros((tq, DF_VA), F32))
        (_, a1), (_, a2) = over_chunks(step, (init, init))
        finish(a1, a2)


def _lat_df(q, kt_lat, v_lat, kt_ctx, v_ctx, df_lambda, df_subln, lam_init, tq):
    b, _, t, _ = q.shape
    assert kt_lat.shape[1] % DF_UNROLL == 0
    kt_spec = lambda kt: pl.BlockSpec((1, kt.shape[1], 2 * DF_HD, DF_CK), lambda i, h, j: (i, 0, h, 0))
    v_spec = lambda v: pl.BlockSpec((1, 1, v.shape[2], DF_VA), lambda i, h, j: (i, h, 0, 0))
    return pl.pallas_call(
        functools.partial(_lat_df_kernel, lam_init),
        out_shape=jax.ShapeDtypeStruct((b, DF_HEADS, t, DF_VD), BF16),
        grid=(b, DF_HEADS, t // tq),
        in_specs=[pl.BlockSpec((1, 1, tq, 2 * DF_HD), lambda i, h, j: (i, h, j, 0)),
                  pl.BlockSpec((1, 1, t, 2 * DF_HD), lambda i, h, j: (i, h, 0, 0)),
                  kt_spec(kt_lat), v_spec(v_lat), kt_spec(kt_ctx), v_spec(v_ctx),
                  pl.BlockSpec((4, DF_HD), lambda i, h, j: (0, 0)),
                  pl.BlockSpec((1, DF_VD), lambda i, h, j: (0, 0))],
        out_specs=pl.BlockSpec((1, 1, tq, DF_VD), lambda i, h, j: (i, h, j, 0)),
        scratch_shapes=[pltpu.SMEM((1,), jnp.int32)],
        compiler_params=_cparams("arbitrary", "arbitrary", "arbitrary"),
        name="lat_df",
    )(q, q, kt_lat, v_lat, kt_ctx, v_ctx, df_lambda, df_subln.reshape(1, DF_VD))


def _with_ones_column(v, width):
    pad = width - v.shape[-1] - 1
    return jnp.concatenate([v, jnp.ones(v.shape[:-1] + (1,), v.dtype), jnp.zeros(v.shape[:-1] + (pad,), v.dtype)],
                           axis=-1)


def _out_kernel(ya_ref, yb_ref, yc_ref, x_ref, ga_ref, w_ref, o_ref):
    if len(yc_ref.shape) == 4:
        yc = jnp.concatenate([yc_ref[0, h] for h in range(DF_HEADS)], axis=-1)
    else:
        yc = yc_ref[0]
    y = (_dot(ya_ref[0], w_ref[0:LRU_W, :]) + _dot(yb_ref[0], w_ref[LRU_W:LRU_W + NA_W, :])
         + _dot(yc, w_ref[LRU_W + NA_W:, :]))
    o_ref[0] = x_ref[0] + ga_ref[0] * y


def _out_proj(latent, ya, yb, yc, x, g_a, w_out, tm):
    b, t, _ = x.shape
    _, ya_spec = _seq_split(latent, b, t, tm, BF16)
    ya = ya.reshape(ya.shape[0], ya.shape[1] // SUBLANES, SUBLANES * LRU_W)
    mod_map = (lambda i, j: (i, 0, 0)) if g_a.shape[0] > 1 else (lambda i, j: (0, 0, 0))
    tile = lambda i, j: (i, j, 0)
    if yc.ndim == 4:
        yc_spec = pl.BlockSpec((1, DF_HEADS, tm, DF_VD), lambda i, j: (i, 0, j, 0))
    else:
        yc_spec = pl.BlockSpec((1, tm, DF_V_W), tile)
    return pl.pallas_call(
        _out_kernel,
        out_shape=jax.ShapeDtypeStruct(x.shape, F32),
        grid=(b, t // tm),
        in_specs=[ya_spec, pl.BlockSpec((1, tm, NA_W), tile),
                  yc_spec, pl.BlockSpec((1, tm, D_MODEL), tile),
                  pl.BlockSpec((1, 1, D_MODEL), mod_map),
                  pl.BlockSpec((D_CAT, D_MODEL), lambda i, j: (0, 0))],
        out_specs=pl.BlockSpec((1, tm, D_MODEL), tile),
        compiler_params=_cparams("arbitrary", "arbitrary"),
        name="out_proj",
    )(ya, yb, yc, x, g_a, w_out)


def _ffn_kernel(nt, xp_ref, x_ref, xn_ref, sh_ref, sc_ref, gf_ref, g_ref, up_ref, cw_ref, cb_ref, dn_ref,
                o_ref, acc_ref):
    j = pl.program_id(1)
    tm = x_ref.shape[1]
    x = x_ref[0]
    xw = jnp.concatenate([xp_ref[0], x, xn_ref[0]], axis=0)
    ms = jnp.mean(xw * xw, axis=-1, keepdims=True)
    y = xw * lax.rsqrt(ms + EPS) * g_ref[...]
    h = y * (1.0 + sc_ref[0]) + sh_ref[0]
    row = lax.broadcasted_iota(jnp.int32, (tm + 2 * HALO, 1), 0)
    valid = ((row >= HALO) | (j > 0)) & ((row < HALO + tm) | (j < nt - 1))
    h = jnp.where(valid, h, 0.0).astype(BF16)

    n_chunks = D_FF // FFN_CF

    def up_proj(c):
        return [_dot(h, up_ref[:, off:off + FFN_CF]) for off in (c * FFN_CF, D_FF + c * FFN_CF)]

    def conv(u, off):
        rows = u.shape[0]
        taps = (pltpu.roll(u, 1, 0), u, pltpu.roll(u, rows - 1, 0))
        out = cb_ref[:, off:off + FFN_CF]
        for k in range(3):
            out = out + cw_ref[k:k + 1, off:off + FFN_CF] * taps[k][HALO:HALO + tm, :]
        return out

    u_cur = up_proj(0)
    for c in range(n_chunks):
        u_next = up_proj(c + 1) if c + 1 < n_chunks else None
        act = (_gelu(conv(u_cur[1], D_FF + c * FFN_CF)) * conv(u_cur[0], c * FFN_CF)).astype(BF16)
        down = _dot(act, dn_ref[c * FFN_CF:(c + 1) * FFN_CF, :])
        if c == 0:
            acc_ref[...] = down
        else:
            acc_ref[...] += down
        u_cur = u_next
    o_ref[0] = x + gf_ref[0] * acc_ref[...]


def _ffn(x, shift, scale, g_f, gain, up, conv_w, conv_b, down, tm):
    b, t, _ = x.shape
    nt = t // tm
    hb = tm // HALO
    mod_map = (lambda i, j: (i, 0, 0)) if shift.shape[0] > 1 else (lambda i, j: (0, 0, 0))
    const2 = lambda i, j: (0, 0)
    mod_spec = pl.BlockSpec((1, 1, D_MODEL), mod_map)
    return pl.pallas_call(
        functools.partial(_ffn_kernel, nt),
        out_shape=jax.ShapeDtypeStruct(x.shape, F32),
        grid=(b, nt),
        in_specs=[pl.BlockSpec((1, HALO, D_MODEL), lambda i, j: (i, jnp.maximum(j * hb - 1, 0), 0)),
                  pl.BlockSpec((1, tm, D_MODEL), lambda i, j: (i, j, 0)),
                  pl.BlockSpec((1, HALO, D_MODEL), lambda i, j: (i, jnp.minimum((j + 1) * hb, t // HALO - 1), 0)),
                  mod_spec, mod_spec, mod_spec,
                  pl.BlockSpec((1, D_MODEL), const2),
                  pl.BlockSpec((D_MODEL, 2 * D_FF), const2),
                  pl.BlockSpec((3, 2 * D_FF), const2),
                  pl.BlockSpec((1, 2 * D_FF), const2),
                  pl.BlockSpec((D_FF, D_MODEL), const2)],
        out_specs=pl.BlockSpec((1, tm, D_MODEL), lambda i, j: (i, j, 0)),
        scratch_shapes=[pltpu.VMEM((tm, D_MODEL), F32)],
        compiler_params=_cparams("arbitrary", "arbitrary"),
        name="ffn",
    )(x, x, x, shift, scale, g_f, gain.reshape(1, D_MODEL), up, conv_w, conv_b.reshape(1, 2 * D_FF), down)


def _split_mod(mod):
    return [mod[:, None, i * D_MODEL:(i + 1) * D_MODEL] for i in range(N_MOD)]


def kernel(x_prompt, x_sample, state_lru, cache_na_k, cache_na_v, cache_df_k, cache_df_v, c, c_ctx, ada_w, ada_b, norm_mix, norm_ffn, w_in, lru_conv_w, lru_conv_b, lru_wa, lru_ba, lru_wx, lru_bx, lru_lambda, na_q_norm, na_k_norm, na_rpb, df_q_norm, df_k_norm, df_lambda, df_subln, w_out, ffn_up, ffn_conv_w, ffn_conv_b, ffn_down):
    bp, tp, _ = x_prompt.shape
    bs, ts, _ = x_sample.shape
    past = cache_na_k.shape[3]
    rows_n = ts // GRID_W

    c_all = jnp.zeros((SUBLANES, D_MODEL), F32).at[0].set(c_ctx).at[1:1 + bs].set(c)
    mods = _modulation(c_all, ada_w, ada_b)
    rope_tabs = _rope_tables(ts)
    zeros_state = jnp.zeros((bp // SUBLANES, 2, SUBLANES, LRU_W), F32)

    y_p, y_s = x_prompt, x_sample
    s_lru, s_nk, s_nv, s_dk, s_dv = [], [], [], [], []
    for l in range(DEPTH):
        lam_init = 0.8 - 0.6 * math.exp(-0.3 * l)
        w_in_l = w_in[l].astype(BF16)
        w_out_l = w_out[l].astype(BF16)
        up_l = ffn_up[l].astype(BF16)
        down_l = ffn_down[l].astype(BF16)
        lru_args = (lru_conv_w[l], lru_conv_b[l], lru_wa[l], lru_wx[l], lru_ba[l], lru_bx[l], lru_lambda[l])
        norm_args = (na_q_norm[l], na_k_norm[l], df_q_norm[l], df_k_norm[l])

        sh_a, sc_a, g_a, sh_f, sc_f, g_f = _split_mod(mods[l, 0:1])
        lx, lg, nq, nk, nv, dq, dk, dv = _prep(False, y_p, sh_a, sc_a, norm_mix[l], w_in_l, *norm_args, None, tp)
        seq4 = lambda v: v.reshape(v.shape[0], v.shape[1], SUBLANES, LRU_W)
        ya, st = _lru(False, seq4(lx), seq4(lg), *lru_args, zeros_state, 256, LRU_RC)
        yb, yc = _ctx_attn(nq, nk, nv, dq, dk, dv, df_lambda[l], df_subln[l], lam_init)
        x1 = _out_proj(False, ya, yb, yc, y_p, g_a, w_out_l, tp)
        y_p = _ffn(x1, sh_f, sc_f, g_f, norm_ffn[l], up_l, ffn_conv_w[l], ffn_conv_b[l], down_l, tp)
        s_lru.append(st.transpose(0, 2, 1, 3).reshape(bp, 2, LRU_W))
        s_nk.append(nk)
        s_nv.append(nv)
        s_dk.append(dk)
        s_dv.append(dv)

        sh_a, sc_a, g_a, sh_f, sc_f, g_f = _split_mod(mods[l, 1:1 + bs])
        lx, lg, nq, nk, nv, dq, dk, dv = _prep(True, y_s, sh_a, sc_a, norm_mix[l], w_in_l, *norm_args, rope_tabs,
                                               ts // SUBLANES)
        h0 = jnp.broadcast_to(state_lru[:, l, :, None, :], (bs, 2, SUBLANES, LRU_W))
        ya, _ = _lru(True, seq4(lx), seq4(lg), *lru_args, h0, 128, LRU_RC)
        yb = _lat_na(nq, nk, nv, cache_na_k[:, l].astype(BF16), cache_na_v[:, l].astype(BF16),
                     _na_bias_tables(na_rpb[l], rows_n))
        kt_ctx = jnp.swapaxes(cache_df_k[:, l].astype(BF16), -1, -2).reshape(bs, DF_QK_W, past // DF_CK, DF_CK)
        kt_ctx = kt_ctx.transpose(0, 2, 1, 3)
        v_ctx = _with_ones_column(cache_df_v[:, l].astype(BF16), DF_VA)
        yc = _lat_df(dq, dk, dv, kt_ctx, v_ctx, df_lambda[l], df_subln[l], lam_init, DF_TQ)
        x1 = _out_proj(True, ya, yb, yc, y_s, g_a, w_out_l, ts // SUBLANES)
        y_s = _ffn(x1, sh_f, sc_f, g_f, norm_ffn[l], up_l, ffn_conv_w[l], ffn_conv_b[l], down_l, 512)

    return (y_p, y_s, jnp.stack(s_lru, axis=1), jnp.stack(s_nk, axis=1), jnp.stack(s_nv, axis=1),
            jnp.stack(s_dk, axis=1), jnp.stack(s_dv, axis=1))
```

```python
import functools
import math

import numpy as np
import jax
import jax.numpy as jnp
from jax import lax
from jax.experimental import pallas as pl
from jax.experimental.pallas import tpu as pltpu

F32 = jnp.float32
BF16 = jnp.bfloat16

D_MODEL = 1024
DEPTH = 2
GRID_W = 64
LRU_W = 512
LRU_BLOCKS = 8
LRU_BW = LRU_W // LRU_BLOCKS
LRU_CONV = 4
LRU_C = 8.0
NA_HEADS = 4
NA_HD = 64
NA_W = NA_HEADS * NA_HD
NA_ROWS = 8
NA_COLS = 16
DF_HEADS = 4
DF_HD = 32
DF_VD = 2 * DF_HD
DF_QK_W = DF_HEADS * 2 * DF_HD
DF_V_W = DF_HEADS * DF_VD
D_IN = 2 * LRU_W + 3 * NA_W + 2 * DF_QK_W + DF_V_W
D_CAT = LRU_W + NA_W + DF_V_W
D_FF = 2816
ROPE_BASE = 10000.0
EPS = 1e-6
NEG_INF = -1e30
N_MOD = 6

OFF_LX = 0
OFF_LG = OFF_LX + LRU_W
OFF_NQ = OFF_LG + LRU_W
OFF_NK = OFF_NQ + NA_W
OFF_NV = OFF_NK + NA_W
OFF_DQ = OFF_NV + NA_W
OFF_DK = OFF_DQ + DF_QK_W
OFF_DV = OFF_DK + DF_QK_W

SUBLANES = 8
VMEM_LIMIT = 56 * 1024 * 1024

NA_QROWS = 4
NA_KROWS = NA_QROWS + NA_ROWS
DF_CK = 512
DF_TQ = 512
DF_VA = 128
DF_SAFE_SCORE = 50.0
DF_UNROLL = 8
LRU_RC = 32
LRU_UNROLL = 8
FFN_CF = 256
HALO = SUBLANES


def _cparams(*sem):
    return pltpu.CompilerParams(dimension_semantics=sem, vmem_limit_bytes=VMEM_LIMIT)


def _gelu(x):
    return 0.5 * x * (1.0 + jnp.tanh(math.sqrt(2.0 / math.pi) * (x + 0.044715 * (x * x * x))))


def _sigmoid(x):
    return 1.0 / (1.0 + jnp.exp(-x))


def _neg_expm1(x, u):
    safe = u != 1.0
    return jnp.where(safe, (1.0 - u) * x / jnp.log(jnp.where(safe, u, 0.5)), -x)


def _dot(a, b):
    return jnp.dot(a, b, preferred_element_type=F32)


def _dot_nt(a, b):
    return lax.dot_general(a, b, (((1,), (1,)), ((), ())), preferred_element_type=F32)


def _mod_kernel(c_ref, w_ref, b_ref, o_ref):
    c = c_ref[...]
    s = (c * _sigmoid(c)).astype(BF16)
    o_ref[0] = _dot(s, w_ref[0].astype(BF16)) + b_ref[0]


def _modulation(c_all, ada_w, ada_b):
    nb = 1536
    n = N_MOD * D_MODEL
    return pl.pallas_call(
        _mod_kernel,
        out_shape=jax.ShapeDtypeStruct((DEPTH, SUBLANES, n), F32),
        grid=(DEPTH, n // nb),
        in_specs=[pl.BlockSpec((SUBLANES, D_MODEL), lambda l, j: (0, 0)),
                  pl.BlockSpec((1, D_MODEL, nb), lambda l, j: (l, 0, j)),
                  pl.BlockSpec((1, 1, nb), lambda l, j: (l, 0, j))],
        out_specs=pl.BlockSpec((1, SUBLANES, nb), lambda l, j: (l, 0, j)),
        compiler_params=_cparams("arbitrary", "arbitrary"),
        name="modulation",
    )(c_all, ada_w, ada_b.reshape(DEPTH, 1, n))


def _group_rms(z, g_ref, gain):
    ms = _dot((z * z).astype(BF16), g_ref[...])
    return z * lax.rsqrt(ms + EPS) * gain


def _rope(x, cos, sin):
    w = x.shape[-1]
    lane = lax.broadcasted_iota(jnp.int32, x.shape, 1)
    partner = jnp.where((lane % 16) < 8, pltpu.roll(x, w - 8, 1), pltpu.roll(x, 8, 1))
    return x * cos + partner * sin


def _prep_kernel(latent, x_ref, sh_ref, sc_ref, g_ref, w_ref, qn_ref, kn_ref, dqn_ref, dkn_ref,
                 g64_ref, g32_ref, *rest):
    if latent:
        cos_ref, sin_ref, lx_ref, lg_ref, nq_ref, nk_ref, nv_ref, dq_ref, dk_ref, dv_ref = rest
    else:
        lx_ref, lg_ref, nq_ref, nk_ref, nv_ref, dq_ref, dk_ref, dv_ref = rest[-8:]
        nk_ref, nv_ref, dk_ref, dv_ref = nk_ref.at[0], nv_ref.at[0], dk_ref.at[0], dv_ref.at[0]
    x = x_ref[0]
    ms = jnp.mean(x * x, axis=-1, keepdims=True)
    y = x * lax.rsqrt(ms + EPS) * g_ref[...]
    h = (y * (1.0 + sc_ref[0]) + sh_ref[0]).astype(BF16)

    def seg(off, size):
        return _dot(h, w_ref[:, off:off + size])

    lx_ref[0] = seg(OFF_LX, LRU_W)
    lg_ref[0] = seg(OFF_LG, LRU_W)

    nq = _group_rms(seg(OFF_NQ, NA_W), g64_ref, qn_ref[...]) * (NA_HD ** -0.5)
    nk = _group_rms(seg(OFF_NK, NA_W), g64_ref, kn_ref[...])
    nv = seg(OFF_NV, NA_W)
    dq = _group_rms(seg(OFF_DQ, DF_QK_W), g32_ref, dqn_ref[...])
    dk = _group_rms(seg(OFF_DK, DF_QK_W), g32_ref, dkn_ref[...])
    dv = seg(OFF_DV, DF_V_W)
    if latent:
        cos = cos_ref[...]
        sin = sin_ref[...]
        dq = _rope(dq, cos, sin)
        dk = _rope(dk, cos, sin)
    dq = dq * (DF_HD ** -0.5)

    for hh in range(NA_HEADS):
        sl = slice(hh * NA_HD, (hh + 1) * NA_HD)
        nq_ref[0, hh] = nq[:, sl].astype(nq_ref.dtype)
        nk_ref[0, hh] = nk[:, sl].astype(nk_ref.dtype)
        nv_ref[0, hh] = nv[:, sl].astype(nv_ref.dtype)
        if latent:
            dq_ref[0, hh] = dq[:, sl].astype(dq_ref.dtype)
            lane = lax.broadcasted_iota(jnp.int32, (dv.shape[0], DF_VA - DF_VD), 1)
            dv_ref[0, hh, :, 0:DF_VD] = dv[:, sl].astype(dv_ref.dtype)
            dv_ref[0, hh, :, DF_VD:] = jnp.where(lane == 0, 1.0, 0.0).astype(dv_ref.dtype)
        else:
            dv_ref[0, hh] = dv[:, sl].astype(dv_ref.dtype)
    if latent:
        dk_ref[0, 0] = dk.T.astype(dk_ref.dtype)
    else:
        for hc in range(2 * DF_HEADS):
            sl = slice(hc * DF_HD, (hc + 1) * DF_HD)
            dq_ref[0, hc // 2, hc % 2] = dq[:, sl].astype(dq_ref.dtype)
            dk_ref[0, hc // 2, hc % 2] = dk[:, sl].astype(dk_ref.dtype)


def _seq_split(latent, b, t, tm, dtype):
    if latent:
        assert tm * SUBLANES == t
        return (jax.ShapeDtypeStruct((b, tm, SUBLANES * LRU_W), dtype),
                pl.BlockSpec((1, tm, LRU_W), lambda i, j: (i, 0, j)))
    assert tm == t and b % SUBLANES == 0
    return (jax.ShapeDtypeStruct((b // SUBLANES, t, SUBLANES * LRU_W), dtype),
            pl.BlockSpec((1, tm, LRU_W), lambda i, j: (i // SUBLANES, 0, i % SUBLANES)))


def _group_matrix(width, group):
    idx = np.arange(width) // group
    return jnp.asarray((idx[:, None] == idx[None, :]).astype(np.float32) / group, dtype=BF16)


def _prep(latent, x, shift, scale, gain, w_in, qn, kn, dqn, dkn, rope_tabs, tm, layer=0, caches=None):
    b, t, _ = x.shape
    aliases = {}
    bm = shift.shape[0]
    mod_map = (lambda i, j: (i, 0, 0)) if bm > 1 else (lambda i, j: (0, 0, 0))
    const2 = lambda i, j: (0, 0)
    tile3 = lambda i, j: (i, j, 0)
    head4 = lambda i, j: (i, 0, j, 0)
    in_specs = [pl.BlockSpec((1, tm, D_MODEL), tile3),
                pl.BlockSpec((1, 1, D_MODEL), mod_map),
                pl.BlockSpec((1, 1, D_MODEL), mod_map),
                pl.BlockSpec((1, D_MODEL), const2),
                pl.BlockSpec((D_MODEL, D_IN), const2),
                pl.BlockSpec((1, NA_W), const2),
                pl.BlockSpec((1, NA_W), const2),
                pl.BlockSpec((1, DF_QK_W), const2),
                pl.BlockSpec((1, DF_QK_W), const2),
                pl.BlockSpec((NA_W, NA_W), const2),
                pl.BlockSpec((DF_QK_W, DF_QK_W), const2)]
    args = [x, shift, scale, gain.reshape(1, D_MODEL), w_in,
            jnp.tile(qn, NA_HEADS).reshape(1, NA_W), jnp.tile(kn, NA_HEADS).reshape(1, NA_W),
            jnp.tile(dqn, 2 * DF_HEADS).reshape(1, DF_QK_W), jnp.tile(dkn, 2 * DF_HEADS).reshape(1, DF_QK_W),
            _group_matrix(NA_W, NA_HD), _group_matrix(DF_QK_W, DF_HD)]
    head_shape = lambda dt: jax.ShapeDtypeStruct((b, NA_HEADS, t, NA_HD), dt)
    head_spec = pl.BlockSpec((1, NA_HEADS, tm, NA_HD), head4)
    seq_shape, seq_spec = _seq_split(latent, b, t, tm, F32)
    out_shape = [seq_shape, seq_shape]
    out_specs = [seq_spec, seq_spec]
    if latent:
        in_specs += [pl.BlockSpec((tm, DF_QK_W), lambda i, j: (j, 0))] * 2
        args += list(rope_tabs)
        out_shape += [head_shape(BF16), head_shape(BF16), head_shape(BF16),
                      head_shape(BF16), jax.ShapeDtypeStruct((b, t // tm, DF_QK_W, tm), BF16),
                      jax.ShapeDtypeStruct((b, DF_HEADS, t, DF_VA), BF16)]
        out_specs += [head_spec, head_spec, head_spec,
                      head_spec, pl.BlockSpec((1, 1, DF_QK_W, tm), lambda i, j: (i, j, 0, 0)),
                      pl.BlockSpec((1, DF_HEADS, tm, DF_VA), head4)]
    else:
        comp_shape = jax.ShapeDtypeStruct((b, DF_HEADS, 2, t, DF_HD), BF16)
        comp_spec = pl.BlockSpec((1, DF_HEADS, 2, tm, DF_HD), lambda i, j: (i, 0, 0, j, 0))
        hcache_shape = jax.ShapeDtypeStruct((b, DEPTH, NA_HEADS, t, NA_HD), F32)
        hcache_spec = pl.BlockSpec((1, 1, NA_HEADS, tm, NA_HD), lambda i, j: (i, layer, 0, j, 0))
        ccache_shape = jax.ShapeDtypeStruct((b, DEPTH, DF_HEADS, 2, t, DF_HD), F32)
        ccache_spec = pl.BlockSpec((1, 1, DF_HEADS, 2, tm, DF_HD), lambda i, j: (i, layer, 0, 0, j, 0))
        out_shape += [head_shape(BF16), hcache_shape, hcache_shape, comp_shape, ccache_shape, hcache_shape]
        out_specs += [head_spec, hcache_spec, hcache_spec, comp_spec, ccache_spec, hcache_spec]
        aliases = {len(args) + k: out_idx for k, out_idx in enumerate((3, 4, 6, 7))}
        in_specs += [pl.BlockSpec(memory_space=pl.ANY)] * 4
        args += list(caches)
    return pl.pallas_call(
        functools.partial(_prep_kernel, latent),
        out_shape=out_shape,
        grid=(b, t // tm),
        in_specs=in_specs,
        out_specs=out_specs,
        input_output_aliases=aliases,
        compiler_params=_cparams("arbitrary", "arbitrary"),
        name="prep_latent" if latent else "prep_context",
    )(*args)


def _rope_tables(t):
    half = DF_HD // 4
    freqs = ROPE_BASE ** (-jnp.arange(half, dtype=F32) / half)
    tok = jnp.arange(t)
    lane = np.arange(DF_QK_W)
    d = lane % DF_HD
    pos = jnp.where((d < DF_HD // 2)[None, :], (tok // GRID_W)[:, None], (tok % GRID_W)[:, None])
    ang = pos.astype(F32) * freqs[d % half][None, :]
    sign = np.where((d % (2 * half)) < half, -1.0, 1.0).astype(np.float32)
    return jnp.cos(ang), jnp.sin(ang) * sign[None, :]


def _lru_kernel(chained, jn, rc, lx_ref, lg_ref, cw_ref, cb_ref, wg_ref, gb_ref, lam_ref, h0_ref,
                ya_ref, st_ref, xpad_ref, af_ref, bf_ref, ab_ref, bb_ref, hf_ref):
    ct = lx_ref.shape[-1]
    row = lax.broadcasted_iota(jnp.int32, (SUBLANES, ct), 0)
    zero = jnp.zeros((SUBLANES, ct), F32)

    def from_prev_seq(v):
        return jnp.where(row >= 1, pltpu.roll(v, 1, 0), 0.0) if chained else zero

    def from_next_seq(v):
        return jnp.where(row < SUBLANES - 1, pltpu.roll(v, SUBLANES - 1, 0), 0.0) if chained else zero

    xpad_ref[0] = from_prev_seq(lx_ref[0, jn - 1])
    xpad_ref[1:jn + 1] = lx_ref[0]
    xpad_ref[jn + 1] = from_next_seq(lx_ref[0, 0])
    xpad_ref[jn + 2] = from_next_seq(lx_ref[0, 1])

    lam = -lam_ref[...]
    sp = jnp.maximum(lam, 0.0) + jnp.log1p(jnp.exp(-jnp.abs(lam)))
    cw = cw_ref[...]
    cb = cb_ref[...]
    wg = wg_ref[0]
    gb = gb_ref[0]

    def gates(i, carry):
        r0 = pl.multiple_of(i * rc, rc)
        xw = xpad_ref[pl.ds(r0, rc + LRU_CONV - 1)]
        xc = cb
        for j in range(LRU_CONV):
            xc = xc + cw[j:j + 1, :] * xw[j:j + rc]
        xc = xc.reshape(rc * SUBLANES, ct)
        g = _dot(xc.astype(BF16), wg) + gb
        for d, (a_ref, b_ref) in enumerate(((af_ref, bf_ref), (ab_ref, bb_ref))):
            r = _sigmoid(g[:, (2 * d) * ct:(2 * d + 1) * ct])
            ig = _sigmoid(g[:, (2 * d + 1) * ct:(2 * d + 2) * ct])
            log_a = (-LRU_C) * r * sp[d:d + 1, :]
            a = jnp.exp(log_a)
            b = jnp.sqrt(_neg_expm1(2.0 * log_a, a * a)) * (ig * xc)
            a_ref[pl.ds(r0, rc)] = a.reshape(rc, SUBLANES, ct)
            b_ref[pl.ds(r0, rc)] = b.reshape(rc, SUBLANES, ct)
        return carry

    lax.fori_loop(0, jn // rc, gates, 0)

    hf0 = h0_ref[0, 0]
    hb0 = h0_ref[0, 1]
    if chained:
        def summarise(j, carry):
            hf, pf, hb, pb = carry
            a = af_ref[j]
            jb = jn - 1 - j
            a2 = ab_ref[jb]
            return a * hf + bf_ref[j], a * pf, a2 * hb + bb_ref[jb], a2 * pb

        one = jnp.ones((SUBLANES, ct), F32)
        hf_end, pf_end, hb_end, pb_end = lax.fori_loop(0, jn, summarise, (zero, one, zero, one), unroll=LRU_UNROLL)
        for s in range(SUBLANES - 1):
            hf0 = jnp.where(row == s + 1, pltpu.roll(hf_end + pf_end * hf0, 1, 0), hf0)
            sb = SUBLANES - 2 - s
            hb0 = jnp.where(row == sb, pltpu.roll(hb_end + pb_end * hb0, SUBLANES - 1, 0), hb0)

    def fwd(j, h):
        h = af_ref[j] * h + bf_ref[j]
        hf_ref[j] = h
        return h

    st_ref[0, 0] = lax.fori_loop(0, jn, fwd, hf0, unroll=LRU_UNROLL)

    def bwd(j, h):
        jb = jn - 1 - j
        h = ab_ref[jb] * h + bb_ref[jb]
        hf_ref[jb] = hf_ref[jb] + h
        return h

    st_ref[0, 1] = lax.fori_loop(0, jn, bwd, hb0, unroll=LRU_UNROLL)

    def emit(i, carry):
        r0 = pl.multiple_of(i * rc, rc)
        y = (hf_ref[pl.ds(r0, rc)] * _gelu(lg_ref[0, pl.ds(r0, rc)])).reshape(rc * SUBLANES, ct)
        ya_ref[0, pl.ds(pl.multiple_of(r0 * SUBLANES, rc * SUBLANES), rc * SUBLANES), :] = y.astype(ya_ref.dtype)
        return carry

    lax.fori_loop(0, jn // rc, emit, 0)


def _lru_gate_weights(wa, wx, ba, bx, ct):
    nct = LRU_W // ct
    per = ct // LRU_BW
    eye = jnp.eye(per, dtype=F32)

    def dense(w):
        wt = w.reshape(nct, per, LRU_BW, LRU_BW)
        return jnp.einsum('cpij,pq->cpiqj', wt, eye).reshape(nct, ct, ct)

    mats = [dense(wa[0]), dense(wx[0]), dense(wa[1]), dense(wx[1])]
    bias = [ba[0], bx[0], ba[1], bx[1]]
    wg = jnp.concatenate(mats, axis=-1).astype(BF16)
    gb = jnp.concatenate([v.reshape(nct, 1, ct) for v in bias], axis=-1)
    return wg, gb


def _lru(chained, lx, lg, conv_w, conv_b, wa, wx, ba, bx, lam, h0, ct, rc):
    g, jn, _, _ = lx.shape
    nct = LRU_W // ct
    wg, gb = _lru_gate_weights(wa, wx, ba, bx, ct)
    tile = lambda i, j: (i, 0, 0, j)
    seq = (jn, SUBLANES, ct)
    return pl.pallas_call(
        functools.partial(_lru_kernel, chained, jn, rc),
        out_shape=[jax.ShapeDtypeStruct((g, jn * SUBLANES, LRU_W), BF16),
                   jax.ShapeDtypeStruct((g, 2, SUBLANES, LRU_W), F32)],
        grid=(g, nct),
        in_specs=[pl.BlockSpec((1,) + seq, tile),
                  pl.BlockSpec((1,) + seq, tile),
                  pl.BlockSpec((LRU_CONV, ct), lambda i, j: (0, j)),
                  pl.BlockSpec((1, ct), lambda i, j: (0, j)),
                  pl.BlockSpec((1, ct, 4 * ct), lambda i, j: (j, 0, 0)),
                  pl.BlockSpec((1, 1, 4 * ct), lambda i, j: (j, 0, 0)),
                  pl.BlockSpec((2, ct), lambda i, j: (0, j)),
                  pl.BlockSpec((1, 2, SUBLANES, ct), tile)],
        out_specs=[pl.BlockSpec((1, jn * SUBLANES, ct), lambda i, j: (i, 0, j)),
                   pl.BlockSpec((1, 2, SUBLANES, ct), tile)],
        scratch_shapes=[pltpu.VMEM((jn + LRU_CONV - 1, SUBLANES, ct), F32)] + [pltpu.VMEM(seq, F32)] * 5,
        compiler_params=_cparams("arbitrary", "arbitrary"),
        name="lru_latent" if chained else "lru_context",
    )(lx, lg, conv_w, conv_b.reshape(1, LRU_W), wg, gb, lam, h0)


def _df_lambda(lv_ref, lam_init):
    lv = lv_ref[...]
    l1 = jnp.sum(lv[0:1, :] * lv[1:2, :], axis=-1, keepdims=True)
    l2 = jnp.sum(lv[2:3, :] * lv[3:4, :], axis=-1, keepdims=True)
    return jnp.exp(l1) - jnp.exp(l2) + lam_init


def _diff_combine(o1, o2, lam, subln, lam_init):
    y = o1 - lam * o2
    ms = jnp.mean(y * y, axis=-1, keepdims=True)
    return y * lax.rsqrt(ms + EPS) * subln * (1.0 - lam_init)


def _softmax_pv(s, v):
    m = jnp.max(s, axis=-1, keepdims=True)
    e = jnp.exp(s - m)
    l = jnp.sum(e, axis=-1, keepdims=True)
    return _dot(e.astype(BF16), v) / l


def _ctx_attn_kernel(lam_init, nq_ref, nk_ref, nv_ref, dq_ref, dk_ref, dv_ref, lv_ref, sub_ref, yb_ref, yc_ref):
    nk_ref, nv_ref, dk_ref, dv_ref = nk_ref.at[0], nv_ref.at[0], dk_ref.at[0], dv_ref.at[0]
    lam = _df_lambda(lv_ref, lam_init)
    sub = sub_ref[...]
    for h in range(NA_HEADS):
        s = _dot_nt(nq_ref[0, h], nk_ref[0, h].astype(BF16))
        o = _softmax_pv(s, nv_ref[0, h].astype(BF16))
        yb_ref[0, :, h * NA_HD:(h + 1) * NA_HD] = o.astype(yb_ref.dtype)
    for h in range(DF_HEADS):
        v = dv_ref[0, h].astype(BF16)
        o1 = _softmax_pv(_dot_nt(dq_ref[0, h, 0], dk_ref[0, h, 0].astype(BF16)), v)
        o2 = _softmax_pv(_dot_nt(dq_ref[0, h, 1], dk_ref[0, h, 1].astype(BF16)), v)
        y = _diff_combine(o1, o2, lam, sub, lam_init)
        yc_ref[0, :, h * DF_VD:(h + 1) * DF_VD] = y.astype(yc_ref.dtype)


def _ctx_attn(nq, nk, nv, dq, dk, dv, df_lambda, df_subln, lam_init, layer):
    b, _, t, _ = nq.shape
    head = pl.BlockSpec((1, NA_HEADS, t, NA_HD), lambda i: (i, 0, 0, 0))
    comp = pl.BlockSpec((1, DF_HEADS, 2, t, DF_HD), lambda i: (i, 0, 0, 0, 0))
    hcache = pl.BlockSpec((1, 1, NA_HEADS, t, NA_HD), lambda i: (i, layer, 0, 0, 0))
    ccache = pl.BlockSpec((1, 1, DF_HEADS, 2, t, DF_HD), lambda i: (i, layer, 0, 0, 0, 0))
    out = pl.BlockSpec((1, t, NA_W), lambda i: (i, 0, 0))
    return pl.pallas_call(
        functools.partial(_ctx_attn_kernel, lam_init),
        out_shape=[jax.ShapeDtypeStruct((b, t, NA_W), BF16), jax.ShapeDtypeStruct((b, t, DF_V_W), BF16)],
        grid=(b,),
        in_specs=[head, hcache, hcache, comp, ccache, hcache,
                  pl.BlockSpec((4, DF_HD), lambda i: (0, 0)),
                  pl.BlockSpec((1, DF_VD), lambda i: (0, 0))],
        out_specs=[out, out],
        compiler_params=_cparams("arbitrary"),
        name="ctx_attn",
    )(nq, nk, nv, dq, dk, dv, df_lambda, df_subln.reshape(1, DF_VD))


def _na_bias_tables(rpb, rows_n):
    n_dr, n_dc = 2 * NA_ROWS - 1, 2 * NA_COLS - 1
    qc = np.arange(GRID_W)[:, None]
    kc = np.arange(GRID_W)[None, :]
    cs = np.clip(qc - NA_COLS // 2, 0, GRID_W - NA_COLS)
    ok_col = (kc >= cs) & (kc < cs + NA_COLS)
    d_col = np.clip(kc - qc + NA_COLS - 1, 0, n_dc - 1)
    sel_col = (d_col[None] == np.arange(n_dc)[:, None, None]) & ok_col[None]
    sel_row = np.zeros((3, NA_QROWS, NA_KROWS, n_dr), np.float32)
    for kind, r0 in enumerate((0, NA_QROWS, rows_n - NA_QROWS)):
        kstart = int(np.clip(r0 - NA_ROWS // 2, 0, rows_n - NA_KROWS))
        r = r0 + np.arange(NA_QROWS)[:, None]
        kr = kstart + np.arange(NA_KROWS)[None, :]
        rs = np.clip(r - NA_ROWS // 2, 0, rows_n - NA_ROWS)
        ok_row = (kr >= rs) & (kr < rs + NA_ROWS)
        d_row = np.clip(kr - r + NA_ROWS - 1, 0, n_dr - 1)
        sel_row[kind] = (d_row[..., None] == np.arange(n_dr)) & ok_row[..., None]
    ok = (sel_row.sum(-1) > 0)[:, :, None, :, None] & ok_col[None, None, :, None, :]
    hi = lax.Precision.HIGHEST
    cols = jnp.einsum('had,dqk->haqk', rpb, jnp.asarray(sel_col, F32), precision=hi)
    tabs = jnp.einsum('tija,haqk->thiqjk', jnp.asarray(sel_row), cols, precision=hi)
    tabs = jnp.where(jnp.asarray(ok)[:, None], tabs, NEG_INF)
    return tabs.reshape(3, NA_HEADS, NA_QROWS * GRID_W, NA_KROWS * GRID_W)


def _lat_na_kernel(rows_n, q_ref, k_ref, v_ref, ck_ref, cv_ref, bias_ref, o_ref):
    i = pl.program_id(1)
    kstart = jnp.clip(i * NA_QROWS - NA_ROWS // 2, 0, rows_n - NA_KROWS)
    k0 = pl.multiple_of(kstart * GRID_W, NA_QROWS * GRID_W)
    nk = NA_KROWS * GRID_W
    for h in range(NA_HEADS):
        q = q_ref[0, h]
        s_loc = _dot_nt(q, k_ref[0, h, pl.ds(k0, nk), :]) + bias_ref[0, h]
        s_ctx = _dot_nt(q, ck_ref[0, h])
        m = jnp.maximum(jnp.max(s_loc, axis=-1, keepdims=True), jnp.max(s_ctx, axis=-1, keepdims=True))
        e_loc = jnp.exp(s_loc - m)
        e_ctx = jnp.exp(s_ctx - m)
        l = jnp.sum(e_loc, axis=-1, keepdims=True) + jnp.sum(e_ctx, axis=-1, keepdims=True)
        o = _dot(e_loc.astype(BF16), v_ref[0, h, pl.ds(k0, nk), :]) + _dot(e_ctx.astype(BF16), cv_ref[0, h])
        o_ref[0, :, h * NA_HD:(h + 1) * NA_HD] = (o / l).astype(o_ref.dtype)


def _lat_na(q, k, v, ck, cv, bias_tabs):
    b, _, t, _ = q.shape
    rows_n = t // GRID_W
    nt = rows_n // NA_QROWS
    tq = NA_QROWS * GRID_W
    past = ck.shape[2]
    full = lambda n: pl.BlockSpec((1, NA_HEADS, n, NA_HD), lambda i, j: (i, 0, 0, 0))

    def kind(i, j):
        return (jnp.where(j == 0, 0, jnp.where(j == nt - 1, 2, 1)), 0, 0, 0)

    return pl.pallas_call(
        functools.partial(_lat_na_kernel, rows_n),
        out_shape=jax.ShapeDtypeStruct((b, t, NA_W), BF16),
        grid=(b, nt),
        in_specs=[pl.BlockSpec((1, NA_HEADS, tq, NA_HD), lambda i, j: (i, 0, j, 0)),
                  full(t), full(t), full(past), full(past),
                  pl.BlockSpec((1, NA_HEADS, tq, NA_KROWS * GRID_W), kind)],
        out_specs=pl.BlockSpec((1, tq, NA_W), lambda i, j: (i, j, 0)),
        compiler_params=_cparams("arbitrary", "arbitrary"),
        name="lat_na",
    )(q, k, v, ck, cv, bias_tabs)


def _lat_df_kernel(lam_init, q_ref, qall_ref, ktl_ref, vl_ref, ktc_ref, vc_ref, lv_ref, sub_ref, o_ref, safe_ref):
    tq = q_ref.shape[2]
    n_lat = ktl_ref.shape[1]
    n_ctx = ktc_ref.shape[1]
    lam = _df_lambda(lv_ref, lam_init)
    q = q_ref[0, 0]
    qs = [q[:, c * DF_HD:(c + 1) * DF_HD] for c in range(2)]
    rows = [slice(c * DF_HD, (c + 1) * DF_HD) for c in range(2)]

    @pl.when(pl.program_id(2) == 0)
    def _():
        bound = jnp.zeros((1, 1), F32)
        for c in range(2):
            mx = jnp.zeros((1, DF_CK), F32)
            for kt_ref, n in ((ktl_ref, n_lat), (ktc_ref, n_ctx)):
                def longest(j, mx):
                    k = kt_ref[0, j, rows[c], :].astype(F32)
                    return jnp.maximum(mx, jnp.sum(k * k, axis=0, keepdims=True))

                mx = lax.fori_loop(0, n, longest, mx)
            kn = jnp.max(mx, axis=-1, keepdims=True)
            qa = qall_ref[0, 0, :, rows[c]].astype(F32)
            qn = jnp.max(jnp.sum(qa * qa, axis=-1, keepdims=True), axis=0, keepdims=True)
            bound = jnp.maximum(bound, qn * kn)
        safe_ref[0] = (jnp.max(bound) <= DF_SAFE_SCORE ** 2).astype(jnp.int32)

    safe = safe_ref[0] == 1

    def finish(a1, a2):
        o1 = a1[:, 0:DF_VD] / a1[:, DF_VD:DF_VD + 1]
        o2 = a2[:, 0:DF_VD] / a2[:, DF_VD:DF_VD + 1]
        o_ref[0, 0] = _diff_combine(o1, o2, lam, sub_ref[...], lam_init).astype(o_ref.dtype)

    def over_chunks(step, carry):
        def trip(t, carry):
            for u in range(DF_UNROLL):
                j = t * DF_UNROLL + u
                v = vl_ref[0, 0, pl.ds(pl.multiple_of(j * DF_CK, DF_CK), DF_CK), :]
                carry = step(carry, lambda c: ktl_ref[0, j, rows[c], :], v)
            return carry

        carry = lax.fori_loop(0, n_lat // DF_UNROLL, trip, carry)
        for j in range(n_ctx):
            carry = step(carry, lambda c: ktc_ref[0, j, rows[c], :], vc_ref[0, 0, j * DF_CK:(j + 1) * DF_CK, :])
        return carry

    @pl.when(safe)
    def _():
        def step(acc, kt, v):
            e = [jnp.exp(_dot(qs[c], kt(c))).astype(BF16) for c in range(2)]
            return acc + _dot(jnp.concatenate(e, axis=0), v)

        acc = over_chunks(step, jnp.zeros((2 * tq, DF_VA), F32))
        finish(acc[0:tq], acc[tq:])

    @pl.when(jnp.logical_not(safe))
    def _():
        def step(carry, kt, v):
            new = []
            for c in range(2):
                m, acc = carry[c]
                s = _dot(qs[c], kt(c))
                m_new = jnp.maximum(m, jnp.max(s, axis=-1, keepdims=True))
                new.append((m_new, jnp.exp(m - m_new) * acc + _dot(jnp.exp(s - m_new).astype(BF16), v)))
            return tuple(new)

        init = (jnp.full((tq, 1), NEG_INF, F32), jnp.zeros((tq, DF_VA), F32))
        (_, a1), (_, a2) = over_chunks(step, (init, init))
        finish(a1, a2)


def _lat_df(q, kt_lat, v_lat, kt_ctx, v_ctx, df_lambda, df_subln, lam_init, tq):
    b, _, t, _ = q.shape
    assert kt_lat.shape[1] % DF_UNROLL == 0
    kt_spec = lambda kt: pl.BlockSpec((1, kt.shape[1], 2 * DF_HD, DF_CK), lambda i, h, j: (i, 0, h, 0))
    v_spec = lambda v: pl.BlockSpec((1, 1, v.shape[2], DF_VA), lambda i, h, j: (i, h, 0, 0))
    return pl.pallas_call(
        functools.partial(_lat_df_kernel, lam_init),
        out_shape=jax.ShapeDtypeStruct((b, DF_HEADS, t, DF_VD), BF16),
        grid=(b, DF_HEADS, t // tq),
        in_specs=[pl.BlockSpec((1, 1, tq, 2 * DF_HD), lambda i, h, j: (i, h, j, 0)),
                  pl.BlockSpec((1, 1, t, 2 * DF_HD), lambda i, h, j: (i, h, 0, 0)),
                  kt_spec(kt_lat), v_spec(v_lat), kt_spec(kt_ctx), v_spec(v_ctx),
                  pl.BlockSpec((4, DF_HD), lambda i, h, j: (0, 0)),
                  pl.BlockSpec((1, DF_VD), lambda i, h, j: (0, 0))],
        out_specs=pl.BlockSpec((1, 1, tq, DF_VD), lambda i, h, j: (i, h, j, 0)),
        scratch_shapes=[pltpu.SMEM((1,), jnp.int32)],
        compiler_params=_cparams("arbitrary", "arbitrary", "arbitrary"),
        name="lat_df",
    )(q, q, kt_lat, v_lat, kt_ctx, v_ctx, df_lambda, df_subln.reshape(1, DF_VD))


def _with_ones_column(v, width):
    pad = width - v.shape[-1] - 1
    return jnp.concatenate([v, jnp.ones(v.shape[:-1] + (1,), v.dtype), jnp.zeros(v.shape[:-1] + (pad,), v.dtype)],
                           axis=-1)


def _out_kernel(ya_ref, yb_ref, yc_ref, x_ref, ga_ref, w_ref, o_ref):
    if len(yc_ref.shape) == 4:
        yc = jnp.concatenate([yc_ref[0, h] for h in range(DF_HEADS)], axis=-1)
    else:
        yc = yc_ref[0]
    y = (_dot(ya_ref[0], w_ref[0:LRU_W, :]) + _dot(yb_ref[0], w_ref[LRU_W:LRU_W + NA_W, :])
         + _dot(yc, w_ref[LRU_W + NA_W:, :]))
    o_ref[0] = x_ref[0] + ga_ref[0] * y


def _out_proj(latent, ya, yb, yc, x, g_a, w_out, tm):
    b, t, _ = x.shape
    _, ya_spec = _seq_split(latent, b, t, tm, BF16)
    ya = ya.reshape(ya.shape[0], ya.shape[1] // SUBLANES, SUBLANES * LRU_W)
    mod_map = (lambda i, j: (i, 0, 0)) if g_a.shape[0] > 1 else (lambda i, j: (0, 0, 0))
    tile = lambda i, j: (i, j, 0)
    if yc.ndim == 4:
        yc_spec = pl.BlockSpec((1, DF_HEADS, tm, DF_VD), lambda i, j: (i, 0, j, 0))
    else:
        yc_spec = pl.BlockSpec((1, tm, DF_V_W), tile)
    return pl.pallas_call(
        _out_kernel,
        out_shape=jax.ShapeDtypeStruct(x.shape, F32),
        grid=(b, t // tm),
        in_specs=[ya_spec, pl.BlockSpec((1, tm, NA_W), tile),
                  yc_spec, pl.BlockSpec((1, tm, D_MODEL), tile),
                  pl.BlockSpec((1, 1, D_MODEL), mod_map),
                  pl.BlockSpec((D_CAT, D_MODEL), lambda i, j: (0, 0))],
        out_specs=pl.BlockSpec((1, tm, D_MODEL), tile),
        compiler_params=_cparams("arbitrary", "arbitrary"),
        name="out_proj",
    )(ya, yb, yc, x, g_a, w_out)


def _ffn_kernel(nt, xp_ref, x_ref, xn_ref, sh_ref, sc_ref, gf_ref, g_ref, up_ref, cw_ref, cb_ref, dn_ref,
                o_ref, acc_ref):
    j = pl.program_id(1)
    tm = x_ref.shape[1]
    x = x_ref[0]
    xw = jnp.concatenate([xp_ref[0], x, xn_ref[0]], axis=0)
    ms = jnp.mean(xw * xw, axis=-1, keepdims=True)
    y = xw * lax.rsqrt(ms + EPS) * g_ref[...]
    h = y * (1.0 + sc_ref[0]) + sh_ref[0]
    row = lax.broadcasted_iota(jnp.int32, (tm + 2 * HALO, 1), 0)
    valid = ((row >= HALO) | (j > 0)) & ((row < HALO + tm) | (j < nt - 1))
    h = jnp.where(valid, h, 0.0).astype(BF16)

    n_chunks = D_FF // FFN_CF

    def up_proj(c):
        return [_dot(h, up_ref[:, off:off + FFN_CF]) for off in (c * FFN_CF, D_FF + c * FFN_CF)]

    def conv(u, off):
        rows = u.shape[0]
        taps = (pltpu.roll(u, 1, 0), u, pltpu.roll(u, rows - 1, 0))
        out = cb_ref[:, off:off + FFN_CF]
        for k in range(3):
            out = out + cw_ref[k:k + 1, off:off + FFN_CF] * taps[k][HALO:HALO + tm, :]
        return out

    u_cur = up_proj(0)
    for c in range(n_chunks):
        u_next = up_proj(c + 1) if c + 1 < n_chunks else None
        act = (_gelu(conv(u_cur[1], D_FF + c * FFN_CF)) * conv(u_cur[0], c * FFN_CF)).astype(BF16)
        down = _dot(act, dn_ref[c * FFN_CF:(c + 1) * FFN_CF, :])
        if c == 0:
            acc_ref[...] = down
        else:
            acc_ref[...] += down
        u_cur = u_next
    o_ref[0] = x + gf_ref[0] * acc_ref[...]


def _ffn(x, shift, scale, g_f, gain, up, conv_w, conv_b, down, tm):
    b, t, _ = x.shape
    nt = t // tm
    hb = tm // HALO
    mod_map = (lambda i, j: (i, 0, 0)) if shift.shape[0] > 1 else (lambda i, j: (0, 0, 0))
    const2 = lambda i, j: (0, 0)
    mod_spec = pl.BlockSpec((1, 1, D_MODEL), mod_map)
    return pl.pallas_call(
        functools.partial(_ffn_kernel, nt),
        out_shape=jax.ShapeDtypeStruct(x.shape, F32),
        grid=(b, nt),
        in_specs=[pl.BlockSpec((1, HALO, D_MODEL), lambda i, j: (i, jnp.maximum(j * hb - 1, 0), 0)),
                  pl.BlockSpec((1, tm, D_MODEL), lambda i, j: (i, j, 0)),
                  pl.BlockSpec((1, HALO, D_MODEL), lambda i, j: (i, jnp.minimum((j + 1) * hb, t // HALO - 1), 0)),
                  mod_spec, mod_spec, mod_spec,
                  pl.BlockSpec((1, D_MODEL), const2),
                  pl.BlockSpec((D_MODEL, 2 * D_FF), const2),
                  pl.BlockSpec((3, 2 * D_FF), const2),
                  pl.BlockSpec((1, 2 * D_FF), const2),
                  pl.BlockSpec((D_FF, D_MODEL), const2)],
        out_specs=pl.BlockSpec((1, tm, D_MODEL), lambda i, j: (i, j, 0)),
        scratch_shapes=[pltpu.VMEM((tm, D_MODEL), F32)],
        compiler_params=_cparams("arbitrary", "arbitrary"),
        name="ffn",
    )(x, x, x, shift, scale, g_f, gain.reshape(1, D_MODEL), up, conv_w, conv_b.reshape(1, 2 * D_FF), down)


def _split_mod(mod):
    return [mod[:, None, i * D_MODEL:(i + 1) * D_MODEL] for i in range(N_MOD)]


def kernel(x_prompt, x_sample, state_lru, cache_na_k, cache_na_v, cache_df_k, cache_df_v, c, c_ctx, ada_w, ada_b, norm_mix, norm_ffn, w_in, lru_conv_w, lru_conv_b, lru_wa, lru_ba, lru_wx, lru_bx, lru_lambda, na_q_norm, na_k_norm, na_rpb, df_q_norm, df_k_norm, df_lambda, df_subln, w_out, ffn_up, ffn_conv_w, ffn_conv_b, ffn_down):
    bp, tp, _ = x_prompt.shape
    bs, ts, _ = x_sample.shape
    past = cache_na_k.shape[3]
    rows_n = ts // GRID_W

    c_all = jnp.zeros((SUBLANES, D_MODEL), F32).at[0].set(c_ctx).at[1:1 + bs].set(c)
    mods = _modulation(c_all, ada_w, ada_b)
    rope_tabs = _rope_tables(ts)
    zeros_state = jnp.zeros((bp // SUBLANES, 2, SUBLANES, LRU_W), F32)

    y_p, y_s = x_prompt, x_sample
    s_lru = []
    hcache = (bp, DEPTH, NA_HEADS, tp, NA_HD)
    caches = [jnp.zeros(hcache, F32), jnp.zeros(hcache, F32),
              jnp.zeros((bp, DEPTH, DF_HEADS, 2, tp, DF_HD), F32), jnp.zeros(hcache, F32)]
    for l in range(DEPTH):
        lam_init = 0.8 - 0.6 * math.exp(-0.3 * l)
        w_in_l = w_in[l].astype(BF16)
        w_out_l = w_out[l].astype(BF16)
        up_l = ffn_up[l].astype(BF16)
        down_l = ffn_down[l].astype(BF16)
        lru_args = (lru_conv_w[l], lru_conv_b[l], lru_wa[l], lru_wx[l], lru_ba[l], lru_bx[l], lru_lambda[l])
        norm_args = (na_q_norm[l], na_k_norm[l], df_q_norm[l], df_k_norm[l])

        sh_a, sc_a, g_a, sh_f, sc_f, g_f = _split_mod(mods[l, 0:1])
        lx, lg, nq, *caches = _prep(False, y_p, sh_a, sc_a, norm_mix[l], w_in_l, *norm_args, None, tp, l, caches)
        dq = caches.pop(2)
        seq4 = lambda v: v.reshape(v.shape[0], v.shape[1], SUBLANES, LRU_W)
        ya, st = _lru(False, seq4(lx), seq4(lg), *lru_args, zeros_state, 256, LRU_RC)
        yb, yc = _ctx_attn(nq, caches[0], caches[1], dq, caches[2], caches[3], df_lambda[l], df_subln[l], lam_init, l)
        x1 = _out_proj(False, ya, yb, yc, y_p, g_a, w_out_l, tp)
        y_p = _ffn(x1, sh_f, sc_f, g_f, norm_ffn[l], up_l, ffn_conv_w[l], ffn_conv_b[l], down_l, tp)
        s_lru.append(st.transpose(0, 2, 1, 3).reshape(bp, 2, LRU_W))

        sh_a, sc_a, g_a, sh_f, sc_f, g_f = _split_mod(mods[l, 1:1 + bs])
        lx, lg, nq, nk, nv, dq, dk, dv = _prep(True, y_s, sh_a, sc_a, norm_mix[l], w_in_l, *norm_args, rope_tabs,
                                               ts // SUBLANES)
        h0 = jnp.broadcast_to(state_lru[:, l, :, None, :], (bs, 2, SUBLANES, LRU_W))
        ya, _ = _lru(True, seq4(lx), seq4(lg), *lru_args, h0, 128, LRU_RC)
        yb = _lat_na(nq, nk, nv, cache_na_k[:, l].astype(BF16), cache_na_v[:, l].astype(BF16),
                     _na_bias_tables(na_rpb[l], rows_n))
        kt_ctx = jnp.swapaxes(cache_df_k[:, l].astype(BF16), -1, -2).reshape(bs, DF_QK_W, past // DF_CK, DF_CK)
        kt_ctx = kt_ctx.transpose(0, 2, 1, 3)
        v_ctx = _with_ones_column(cache_df_v[:, l].astype(BF16), DF_VA)
        yc = _lat_df(dq, dk, dv, kt_ctx, v_ctx, df_lambda[l], df_subln[l], lam_init, DF_TQ)
        x1 = _out_proj(True, ya, yb, yc, y_s, g_a, w_out_l, ts // SUBLANES)
        y_s = _ffn(x1, sh_f, sc_f, g_f, norm_ffn[l], up_l, ffn_conv_w[l], ffn_conv_b[l], down_l, 512)

    return (y_p, y_s, jnp.stack(s_lru, axis=1), *caches)
```

```python
import functools
import math

import numpy as np
import jax
import jax.numpy as jnp
from jax import lax
from jax.experimental import pallas as pl
from jax.experimental.pallas import tpu as pltpu

F32 = jnp.float32
BF16 = jnp.bfloat16

D_MODEL = 1024
DEPTH = 2
GRID_W = 64
LRU_W = 512
LRU_BLOCKS = 8
LRU_BW = LRU_W // LRU_BLOCKS
LRU_CONV = 4
LRU_C = 8.0
NA_HEADS = 4
NA_HD = 64
NA_W = NA_HEADS * NA_HD
NA_ROWS = 8
NA_COLS = 16
DF_HEADS = 4
DF_HD = 32
DF_VD = 2 * DF_HD
DF_QK_W = DF_HEADS * 2 * DF_HD
DF_V_W = DF_HEADS * DF_VD
D_IN = 2 * LRU_W + 3 * NA_W + 2 * DF_QK_W + DF_V_W
D_CAT = LRU_W + NA_W + DF_V_W
D_FF = 2816
ROPE_BASE = 10000.0
EPS = 1e-6
NEG_INF = -1e30
N_MOD = 6

OFF_LX = 0
OFF_LG = OFF_LX + LRU_W
OFF_NQ = OFF_LG + LRU_W
OFF_NK = OFF_NQ + NA_W
OFF_NV = OFF_NK + NA_W
OFF_DQ = OFF_NV + NA_W
OFF_DK = OFF_DQ + DF_QK_W
OFF_DV = OFF_DK + DF_QK_W

SUBLANES = 8
VMEM_LIMIT = 56 * 1024 * 1024

NA_QROWS = 4
NA_KROWS = NA_QROWS + NA_ROWS
DF_CK = 512
DF_TQ = 512
DF_VA = 128
NA_VA = DF_VA
DF_SAFE_SCORE = 50.0
DF_UNROLL = 8
LRU_RC = 32
LRU_UNROLL = 8
FFN_CF = 256
FFN_TM = 512
HALO = SUBLANES


def _cparams(*sem):
    return pltpu.CompilerParams(dimension_semantics=sem, vmem_limit_bytes=VMEM_LIMIT)


def _gelu(x):
    return 0.5 * x * (1.0 + jnp.tanh(math.sqrt(2.0 / math.pi) * (x + 0.044715 * (x * x * x))))


def _sigmoid(x):
    return 1.0 / (1.0 + jnp.exp(-x))


def _neg_expm1(x, u):
    safe = u != 1.0
    return jnp.where(safe, (1.0 - u) * x / jnp.log(jnp.where(safe, u, 0.5)), -x)


def _dot(a, b):
    return jnp.dot(a, b, preferred_element_type=F32)


def _dot_nt(a, b):
    return lax.dot_general(a, b, (((1,), (1,)), ((), ())), preferred_element_type=F32)


def _mod_kernel(c_ref, w_ref, b_ref, o_ref):
    c = c_ref[...]
    s = (c * _sigmoid(c)).astype(BF16)
    o_ref[0] = _dot(s, w_ref[0].astype(BF16)) + b_ref[0]


def _modulation(c_all, ada_w, ada_b):
    nb = 1536
    n = N_MOD * D_MODEL
    return pl.pallas_call(
        _mod_kernel,
        out_shape=jax.ShapeDtypeStruct((DEPTH, SUBLANES, n), F32),
        grid=(DEPTH, n // nb),
        in_specs=[pl.BlockSpec((SUBLANES, D_MODEL), lambda l, j: (0, 0)),
                  pl.BlockSpec((1, D_MODEL, nb), lambda l, j: (l, 0, j)),
                  pl.BlockSpec((1, 1, nb), lambda l, j: (l, 0, j))],
        out_specs=pl.BlockSpec((1, SUBLANES, nb), lambda l, j: (l, 0, j)),
        compiler_params=_cparams("arbitrary", "arbitrary"),
        name="modulation",
    )(c_all, ada_w, ada_b.reshape(DEPTH, 1, n))


def _group_rms(z, g_ref, gain):
    ms = _dot((z * z).astype(BF16), g_ref[...])
    return z * lax.rsqrt(ms + EPS) * gain


def _rope(x, cos, sin):
    w = x.shape[-1]
    lane = lax.broadcasted_iota(jnp.int32, x.shape, 1)
    partner = jnp.where((lane % 16) < 8, pltpu.roll(x, w - 8, 1), pltpu.roll(x, 8, 1))
    return x * cos + partner * sin


def _prep_kernel(latent, x_ref, sh_ref, sc_ref, g_ref, w_ref, qn_ref, kn_ref, dqn_ref, dkn_ref,
                 g64_ref, g32_ref, *rest):
    if latent:
        cos_ref, sin_ref, lx_ref, lg_ref, nq_ref, nk_ref, nv_ref, dq_ref, dk_ref, dv_ref = rest
    else:
        lx_ref, lg_ref, nq_ref, nk_ref, nv_ref, dq_ref, dk_ref, dv_ref = rest[-8:]
        nk_ref, nv_ref, dk_ref, dv_ref = nk_ref.at[0], nv_ref.at[0], dk_ref.at[0], dv_ref.at[0]
    x = x_ref[0]
    ms = jnp.mean(x * x, axis=-1, keepdims=True)
    y = x * lax.rsqrt(ms + EPS) * g_ref[...]
    h = (y * (1.0 + sc_ref[0]) + sh_ref[0]).astype(BF16)

    def seg(off, size):
        return _dot(h, w_ref[:, off:off + size])

    lx_ref[0] = seg(OFF_LX, LRU_W)
    lg_ref[0] = seg(OFF_LG, LRU_W)

    nq = _group_rms(seg(OFF_NQ, NA_W), g64_ref, qn_ref[...]) * (NA_HD ** -0.5)
    nk = _group_rms(seg(OFF_NK, NA_W), g64_ref, kn_ref[...])
    nv = seg(OFF_NV, NA_W)
    dq = _group_rms(seg(OFF_DQ, DF_QK_W), g32_ref, dqn_ref[...])
    dk = _group_rms(seg(OFF_DK, DF_QK_W), g32_ref, dkn_ref[...])
    dv = seg(OFF_DV, DF_V_W)
    if latent:
        cos = cos_ref[...]
        sin = sin_ref[...]
        dq = _rope(dq, cos, sin)
        dk = _rope(dk, cos, sin)
    dq = dq * (DF_HD ** -0.5)

    for hh in range(NA_HEADS):
        sl = slice(hh * NA_HD, (hh + 1) * NA_HD)
        nq_ref[0, hh] = nq[:, sl].astype(nq_ref.dtype)
        nk_ref[0, hh] = nk[:, sl].astype(nk_ref.dtype)
        if latent:
            dq_ref[0, hh] = dq[:, sl].astype(dq_ref.dtype)
            lane = lax.broadcasted_iota(jnp.int32, (dv.shape[0], DF_VA - DF_VD), 1)
            ones_col = jnp.where(lane == 0, 1.0, 0.0).astype(dv_ref.dtype)
            for val, ref in ((nv, nv_ref), (dv, dv_ref)):
                ref[0, hh, :, 0:DF_VD] = val[:, sl].astype(ref.dtype)
                ref[0, hh, :, DF_VD:] = ones_col
        else:
            nv_ref[0, hh] = nv[:, sl].astype(nv_ref.dtype)
            dv_ref[0, hh] = dv[:, sl].astype(dv_ref.dtype)
    if latent:
        dk_ref[0, 0] = dk.T.astype(dk_ref.dtype)
    else:
        for hc in range(2 * DF_HEADS):
            sl = slice(hc * DF_HD, (hc + 1) * DF_HD)
            dq_ref[0, hc // 2, hc % 2] = dq[:, sl].astype(dq_ref.dtype)
            dk_ref[0, hc // 2, hc % 2] = dk[:, sl].astype(dk_ref.dtype)


def _seq_split(latent, b, t, tm, dtype):
    if latent:
        assert tm * SUBLANES == t
        return (jax.ShapeDtypeStruct((b, tm, SUBLANES * LRU_W), dtype),
                pl.BlockSpec((1, tm, LRU_W), lambda i, j: (i, 0, j)))
    assert tm == t and b % SUBLANES == 0
    return (jax.ShapeDtypeStruct((b // SUBLANES, t, SUBLANES * LRU_W), dtype),
            pl.BlockSpec((1, tm, LRU_W), lambda i, j: (i // SUBLANES, 0, i % SUBLANES)))


def _group_matrix(width, group):
    idx = np.arange(width) // group
    return jnp.asarray((idx[:, None] == idx[None, :]).astype(np.float32) / group, dtype=BF16)


def _prep(latent, x, shift, scale, gain, w_in, qn, kn, dqn, dkn, rope_tabs, tm, layer=0, caches=None):
    b, t, _ = x.shape
    aliases = {}
    bm = shift.shape[0]
    mod_map = (lambda i, j: (i, 0, 0)) if bm > 1 else (lambda i, j: (0, 0, 0))
    const2 = lambda i, j: (0, 0)
    tile3 = lambda i, j: (i, j, 0)
    head4 = lambda i, j: (i, 0, j, 0)
    in_specs = [pl.BlockSpec((1, tm, D_MODEL), tile3),
                pl.BlockSpec((1, 1, D_MODEL), mod_map),
                pl.BlockSpec((1, 1, D_MODEL), mod_map),
                pl.BlockSpec((1, D_MODEL), const2),
                pl.BlockSpec((D_MODEL, D_IN), const2),
                pl.BlockSpec((1, NA_W), const2),
                pl.BlockSpec((1, NA_W), const2),
                pl.BlockSpec((1, DF_QK_W), const2),
                pl.BlockSpec((1, DF_QK_W), const2),
                pl.BlockSpec((NA_W, NA_W), const2),
                pl.BlockSpec((DF_QK_W, DF_QK_W), const2)]
    args = [x, shift, scale, gain.reshape(1, D_MODEL), w_in,
            jnp.tile(qn, NA_HEADS).reshape(1, NA_W), jnp.tile(kn, NA_HEADS).reshape(1, NA_W),
            jnp.tile(dqn, 2 * DF_HEADS).reshape(1, DF_QK_W), jnp.tile(dkn, 2 * DF_HEADS).reshape(1, DF_QK_W),
            _group_matrix(NA_W, NA_HD), _group_matrix(DF_QK_W, DF_HD)]
    head_shape = lambda dt: jax.ShapeDtypeStruct((b, NA_HEADS, t, NA_HD), dt)
    head_spec = pl.BlockSpec((1, NA_HEADS, tm, NA_HD), head4)
    seq_shape, seq_spec = _seq_split(latent, b, t, tm, F32)
    out_shape = [seq_shape, seq_shape]
    out_specs = [seq_spec, seq_spec]
    if latent:
        in_specs += [pl.BlockSpec((tm, DF_QK_W), lambda i, j: (j, 0))] * 2
        args += list(rope_tabs)
        aug_shape = jax.ShapeDtypeStruct((b, DF_HEADS, t, DF_VA), BF16)
        aug_spec = pl.BlockSpec((1, DF_HEADS, tm, DF_VA), head4)
        out_shape += [head_shape(BF16), head_shape(BF16), aug_shape,
                      head_shape(BF16), jax.ShapeDtypeStruct((b, t // tm, DF_QK_W, tm), BF16), aug_shape]
        out_specs += [head_spec, head_spec, aug_spec,
                      head_spec, pl.BlockSpec((1, 1, DF_QK_W, tm), lambda i, j: (i, j, 0, 0)), aug_spec]
    else:
        comp_shape = jax.ShapeDtypeStruct((b, DF_HEADS, 2, t, DF_HD), BF16)
        comp_spec = pl.BlockSpec((1, DF_HEADS, 2, tm, DF_HD), lambda i, j: (i, 0, 0, j, 0))
        hcache_shape = jax.ShapeDtypeStruct((b, DEPTH, NA_HEADS, t, NA_HD), F32)
        hcache_spec = pl.BlockSpec((1, 1, NA_HEADS, tm, NA_HD), lambda i, j: (i, layer, 0, j, 0))
        ccache_shape = jax.ShapeDtypeStruct((b, DEPTH, DF_HEADS, 2, t, DF_HD), F32)
        ccache_spec = pl.BlockSpec((1, 1, DF_HEADS, 2, tm, DF_HD), lambda i, j: (i, layer, 0, 0, j, 0))
        out_shape += [head_shape(BF16), hcache_shape, hcache_shape, comp_shape, ccache_shape, hcache_shape]
        out_specs += [head_spec, hcache_spec, hcache_spec, comp_spec, ccache_spec, hcache_spec]
        aliases = {len(args) + k: out_idx for k, out_idx in enumerate((3, 4, 6, 7))}
        in_specs += [pl.BlockSpec(memory_space=pl.ANY)] * 4
        args += list(caches)
    return pl.pallas_call(
        functools.partial(_prep_kernel, latent),
        out_shape=out_shape,
        grid=(b, t // tm),
        in_specs=in_specs,
        out_specs=out_specs,
        input_output_aliases=aliases,
        compiler_params=_cparams("arbitrary", "arbitrary"),
        name="prep_latent" if latent else "prep_context",
    )(*args)


def _rope_tables(t):
    half = DF_HD // 4
    freqs = ROPE_BASE ** (-jnp.arange(half, dtype=F32) / half)
    tok = jnp.arange(t)
    lane = np.arange(DF_QK_W)
    d = lane % DF_HD
    pos = jnp.where((d < DF_HD // 2)[None, :], (tok // GRID_W)[:, None], (tok % GRID_W)[:, None])
    ang = pos.astype(F32) * freqs[d % half][None, :]
    sign = np.where((d % (2 * half)) < half, -1.0, 1.0).astype(np.float32)
    return jnp.cos(ang), jnp.sin(ang) * sign[None, :]


def _lru_kernel(chained, jn, rc, lx_ref, lg_ref, cw_ref, cb_ref, wg_ref, gb_ref, lam_ref, h0_ref,
                ya_ref, st_ref, xpad_ref, af_ref, bf_ref, ab_ref, bb_ref, hf_ref):
    ct = lx_ref.shape[-1]
    row = lax.broadcasted_iota(jnp.int32, (SUBLANES, ct), 0)
    zero = jnp.zeros((SUBLANES, ct), F32)

    def from_prev_seq(v):
        return jnp.where(row >= 1, pltpu.roll(v, 1, 0), 0.0) if chained else zero

    def from_next_seq(v):
        return jnp.where(row < SUBLANES - 1, pltpu.roll(v, SUBLANES - 1, 0), 0.0) if chained else zero

    xpad_ref[0] = from_prev_seq(lx_ref[0, jn - 1])
    xpad_ref[1:jn + 1] = lx_ref[0]
    xpad_ref[jn + 1] = from_next_seq(lx_ref[0, 0])
    xpad_ref[jn + 2] = from_next_seq(lx_ref[0, 1])

    lam = -lam_ref[...]
    sp = jnp.maximum(lam, 0.0) + jnp.log1p(jnp.exp(-jnp.abs(lam)))
    cw = cw_ref[...]
    cb = cb_ref[...]
    wg = wg_ref[0]
    gb = gb_ref[0]

    def gates(i, carry):
        r0 = pl.multiple_of(i * rc, rc)
        xw = xpad_ref[pl.ds(r0, rc + LRU_CONV - 1)]
        xc = cb
        for j in range(LRU_CONV):
            xc = xc + cw[j:j + 1, :] * xw[j:j + rc]
        xc = xc.reshape(rc * SUBLANES, ct)
        g = _dot(xc.astype(BF16), wg) + gb
        for d, (a_ref, b_ref) in enumerate(((af_ref, bf_ref), (ab_ref, bb_ref))):
            r = _sigmoid(g[:, (2 * d) * ct:(2 * d + 1) * ct])
            ig = _sigmoid(g[:, (2 * d + 1) * ct:(2 * d + 2) * ct])
            log_a = (-LRU_C) * r * sp[d:d + 1, :]
            a = jnp.exp(log_a)
            b = jnp.sqrt(_neg_expm1(2.0 * log_a, a * a)) * (ig * xc)
            a_ref[pl.ds(r0, rc)] = a.reshape(rc, SUBLANES, ct)
            b_ref[pl.ds(r0, rc)] = b.reshape(rc, SUBLANES, ct)
        return carry

    lax.fori_loop(0, jn // rc, gates, 0)

    hf0 = h0_ref[0, 0]
    hb0 = h0_ref[0, 1]
    if chained:
        def summarise(j, carry):
            hf, pf, hb, pb = carry
            a = af_ref[j]
            jb = jn - 1 - j
            a2 = ab_ref[jb]
            return a * hf + bf_ref[j], a * pf, a2 * hb + bb_ref[jb], a2 * pb

        one = jnp.ones((SUBLANES, ct), F32)
        hf_end, pf_end, hb_end, pb_end = lax.fori_loop(0, jn, summarise, (zero, one, zero, one), unroll=LRU_UNROLL)
        for s in range(SUBLANES - 1):
            hf0 = jnp.where(row == s + 1, pltpu.roll(hf_end + pf_end * hf0, 1, 0), hf0)
            sb = SUBLANES - 2 - s
            hb0 = jnp.where(row == sb, pltpu.roll(hb_end + pb_end * hb0, SUBLANES - 1, 0), hb0)

    def fwd(j, h):
        h = af_ref[j] * h + bf_ref[j]
        hf_ref[j] = h
        return h

    st_ref[0, 0] = lax.fori_loop(0, jn, fwd, hf0, unroll=LRU_UNROLL)

    def bwd(j, h):
        jb = jn - 1 - j
        h = ab_ref[jb] * h + bb_ref[jb]
        hf_ref[jb] = hf_ref[jb] + h
        return h

    st_ref[0, 1] = lax.fori_loop(0, jn, bwd, hb0, unroll=LRU_UNROLL)

    def emit(i, carry):
        r0 = pl.multiple_of(i * rc, rc)
        y = (hf_ref[pl.ds(r0, rc)] * _gelu(lg_ref[0, pl.ds(r0, rc)])).reshape(rc * SUBLANES, ct)
        ya_ref[0, pl.ds(pl.multiple_of(r0 * SUBLANES, rc * SUBLANES), rc * SUBLANES), :] = y.astype(ya_ref.dtype)
        return carry

    lax.fori_loop(0, jn // rc, emit, 0)


def _lru_gate_weights(wa, wx, ba, bx, ct):
    nct = LRU_W // ct
    per = ct // LRU_BW
    eye = jnp.eye(per, dtype=F32)

    def dense(w):
        wt = w.reshape(nct, per, LRU_BW, LRU_BW)
        return jnp.einsum('cpij,pq->cpiqj', wt, eye).reshape(nct, ct, ct)

    mats = [dense(wa[0]), dense(wx[0]), dense(wa[1]), dense(wx[1])]
    bias = [ba[0], bx[0], ba[1], bx[1]]
    wg = jnp.concatenate(mats, axis=-1).astype(BF16)
    gb = jnp.concatenate([v.reshape(nct, 1, ct) for v in bias], axis=-1)
    return wg, gb


def _lru(chained, lx, lg, conv_w, conv_b, wa, wx, ba, bx, lam, h0, ct, rc):
    g, jn, _, _ = lx.shape
    nct = LRU_W // ct
    wg, gb = _lru_gate_weights(wa, wx, ba, bx, ct)
    tile = lambda i, j: (i, 0, 0, j)
    seq = (jn, SUBLANES, ct)
    return pl.pallas_call(
        functools.partial(_lru_kernel, chained, jn, rc),
        out_shape=[jax.ShapeDtypeStruct((g, jn * SUBLANES, LRU_W), BF16),
                   jax.ShapeDtypeStruct((g, 2, SUBLANES, LRU_W), F32)],
        grid=(g, nct),
        in_specs=[pl.BlockSpec((1,) + seq, tile),
                  pl.BlockSpec((1,) + seq, tile),
                  pl.BlockSpec((LRU_CONV, ct), lambda i, j: (0, j)),
                  pl.BlockSpec((1, ct), lambda i, j: (0, j)),
                  pl.BlockSpec((1, ct, 4 * ct), lambda i, j: (j, 0, 0)),
                  pl.BlockSpec((1, 1, 4 * ct), lambda i, j: (j, 0, 0)),
                  pl.BlockSpec((2, ct), lambda i, j: (0, j)),
                  pl.BlockSpec((1, 2, SUBLANES, ct), tile)],
        out_specs=[pl.BlockSpec((1, jn * SUBLANES, ct), lambda i, j: (i, 0, j)),
                   pl.BlockSpec((1, 2, SUBLANES, ct), tile)],
        scratch_shapes=[pltpu.VMEM((jn + LRU_CONV - 1, SUBLANES, ct), F32)] + [pltpu.VMEM(seq, F32)] * 5,
        compiler_params=_cparams("arbitrary", "arbitrary"),
        name="lru_latent" if chained else "lru_context",
    )(lx, lg, conv_w, conv_b.reshape(1, LRU_W), wg, gb, lam, h0)


def _df_lambda(lv_ref, lam_init):
    lv = lv_ref[...]
    l1 = jnp.sum(lv[0:1, :] * lv[1:2, :], axis=-1, keepdims=True)
    l2 = jnp.sum(lv[2:3, :] * lv[3:4, :], axis=-1, keepdims=True)
    return jnp.exp(l1) - jnp.exp(l2) + lam_init


def _diff_combine(o1, o2, lam, subln, lam_init):
    y = o1 - lam * o2
    ms = jnp.mean(y * y, axis=-1, keepdims=True)
    return y * lax.rsqrt(ms + EPS) * subln * (1.0 - lam_init)


def _softmax_pv(s, v):
    m = jnp.max(s, axis=-1, keepdims=True)
    e = jnp.exp(s - m)
    l = jnp.sum(e, axis=-1, keepdims=True)
    return _dot(e.astype(BF16), v) / l


def _ctx_attn_kernel(lam_init, nq_ref, nk_ref, nv_ref, dq_ref, dk_ref, dv_ref, lv_ref, sub_ref, yb_ref, yc_ref):
    nk_ref, nv_ref, dk_ref, dv_ref = nk_ref.at[0], nv_ref.at[0], dk_ref.at[0], dv_ref.at[0]
    lam = _df_lambda(lv_ref, lam_init)
    sub = sub_ref[...]
    for h in range(NA_HEADS):
        s = _dot_nt(nq_ref[0, h], nk_ref[0, h].astype(BF16))
        o = _softmax_pv(s, nv_ref[0, h].astype(BF16))
        yb_ref[0, :, h * NA_HD:(h + 1) * NA_HD] = o.astype(yb_ref.dtype)
    for h in range(DF_HEADS):
        v = dv_ref[0, h].astype(BF16)
        o1 = _softmax_pv(_dot_nt(dq_ref[0, h, 0], dk_ref[0, h, 0].astype(BF16)), v)
        o2 = _softmax_pv(_dot_nt(dq_ref[0, h, 1], dk_ref[0, h, 1].astype(BF16)), v)
        y = _diff_combine(o1, o2, lam, sub, lam_init)
        yc_ref[0, :, h * DF_VD:(h + 1) * DF_VD] = y.astype(yc_ref.dtype)


def _ctx_attn(nq, nk, nv, dq, dk, dv, df_lambda, df_subln, lam_init, layer):
    b, _, t, _ = nq.shape
    head = pl.BlockSpec((1, NA_HEADS, t, NA_HD), lambda i: (i, 0, 0, 0))
    comp = pl.BlockSpec((1, DF_HEADS, 2, t, DF_HD), lambda i: (i, 0, 0, 0, 0))
    hcache = pl.BlockSpec((1, 1, NA_HEADS, t, NA_HD), lambda i: (i, layer, 0, 0, 0))
    ccache = pl.BlockSpec((1, 1, DF_HEADS, 2, t, DF_HD), lambda i: (i, layer, 0, 0, 0, 0))
    out = pl.BlockSpec((1, t, NA_W), lambda i: (i, 0, 0))
    return pl.pallas_call(
        functools.partial(_ctx_attn_kernel, lam_init),
        out_shape=[jax.ShapeDtypeStruct((b, t, NA_W), BF16), jax.ShapeDtypeStruct((b, t, DF_V_W), BF16)],
        grid=(b,),
        in_specs=[head, hcache, hcache, comp, ccache, hcache,
                  pl.BlockSpec((4, DF_HD), lambda i: (0, 0)),
                  pl.BlockSpec((1, DF_VD), lambda i: (0, 0))],
        out_specs=[out, out],
        compiler_params=_cparams("arbitrary"),
        name="ctx_attn",
    )(nq, nk, nv, dq, dk, dv, df_lambda, df_subln.reshape(1, DF_VD))


def _na_bias_tables(rpb, rows_n):
    n_dr, n_dc = 2 * NA_ROWS - 1, 2 * NA_COLS - 1
    qc = np.arange(GRID_W)[:, None]
    kc = np.arange(GRID_W)[None, :]
    cs = np.clip(qc - NA_COLS // 2, 0, GRID_W - NA_COLS)
    ok_col = (kc >= cs) & (kc < cs + NA_COLS)
    d_col = np.clip(kc - qc + NA_COLS - 1, 0, n_dc - 1)
    sel_col = (d_col[None] == np.arange(n_dc)[:, None, None]) & ok_col[None]
    sel_row = np.zeros((3, NA_QROWS, NA_KROWS, n_dr), np.float32)
    for kind, r0 in enumerate((0, NA_QROWS, rows_n - NA_QROWS)):
        kstart = int(np.clip(r0 - NA_ROWS // 2, 0, rows_n - NA_KROWS))
        r = r0 + np.arange(NA_QROWS)[:, None]
        kr = kstart + np.arange(NA_KROWS)[None, :]
        rs = np.clip(r - NA_ROWS // 2, 0, rows_n - NA_ROWS)
        ok_row = (kr >= rs) & (kr < rs + NA_ROWS)
        d_row = np.clip(kr - r + NA_ROWS - 1, 0, n_dr - 1)
        sel_row[kind] = (d_row[..., None] == np.arange(n_dr)) & ok_row[..., None]
    ok = (sel_row.sum(-1) > 0)[:, :, None, :, None] & ok_col[None, None, :, None, :]
    hi = lax.Precision.HIGHEST
    cols = jnp.einsum('had,dqk->haqk', rpb, jnp.asarray(sel_col, F32), precision=hi)
    tabs = jnp.einsum('tija,haqk->thiqjk', jnp.asarray(sel_row), cols, precision=hi)
    tabs = jnp.where(jnp.asarray(ok)[:, None], tabs, NEG_INF)
    return tabs.reshape(3, NA_HEADS, NA_QROWS * GRID_W, NA_KROWS * GRID_W)


def _lat_na_kernel(rows_n, q_ref, k_ref, v_ref, ck_ref, cv_ref, bias_ref, qg_ref, kg_ref, rpb_ref, o_ref, safe_ref):
    i = pl.program_id(1)
    kstart = jnp.clip(i * NA_QROWS - NA_ROWS // 2, 0, rows_n - NA_KROWS)
    k0 = pl.multiple_of(kstart * GRID_W, NA_QROWS * GRID_W)
    nk = NA_KROWS * GRID_W

    @pl.when(i == 0)
    def _():
        ck2 = jnp.zeros((1, 1), F32)
        for h in range(NA_HEADS):
            c = ck_ref[0, h].astype(F32)
            ck2 = jnp.maximum(ck2, jnp.max(jnp.sum(c * c, axis=-1, keepdims=True), axis=0, keepdims=True))
        k_max = jnp.maximum(jnp.max(jnp.abs(kg_ref[...])) * NA_HD ** 0.5, jnp.sqrt(jnp.max(ck2)))
        bound = jnp.max(jnp.abs(qg_ref[...])) * k_max + jnp.max(jnp.abs(rpb_ref[...]))
        safe_ref[0] = (bound <= DF_SAFE_SCORE).astype(jnp.int32)

    safe = safe_ref[0] == 1

    def heads(fast):
        for h in range(NA_HEADS):
            q = q_ref[0, h]
            s_loc = _dot_nt(q, k_ref[0, h, pl.ds(k0, nk), :]) + bias_ref[0, h]
            s_ctx = _dot_nt(q, ck_ref[0, h])
            if not fast:
                m = jnp.maximum(jnp.max(s_loc, axis=-1, keepdims=True), jnp.max(s_ctx, axis=-1, keepdims=True))
                s_loc = s_loc - m
                s_ctx = s_ctx - m
            o = (_dot(jnp.exp(s_loc).astype(BF16), v_ref[0, h, pl.ds(k0, nk), :])
                 + _dot(jnp.exp(s_ctx).astype(BF16), cv_ref[0, h]))
            o = o[:, 0:NA_HD] / o[:, NA_HD:NA_HD + 1]
            o_ref[0, :, h * NA_HD:(h + 1) * NA_HD] = o.astype(o_ref.dtype)

    @pl.when(safe)
    def _():
        heads(True)

    @pl.when(jnp.logical_not(safe))
    def _():
        heads(False)


def _lat_na(q, k, v, ck, cv, bias_tabs, q_gain, k_gain, rpb):
    b, _, t, _ = q.shape
    rows_n = t // GRID_W
    nt = rows_n // NA_QROWS
    tq = NA_QROWS * GRID_W
    past = ck.shape[2]
    full = lambda n, w: pl.BlockSpec((1, NA_HEADS, n, w), lambda i, j: (i, 0, 0, 0))
    const2 = lambda i, j: (0, 0)
    rpb2 = rpb.reshape(-1, rpb.shape[-1])

    def kind(i, j):
        return (jnp.where(j == 0, 0, jnp.where(j == nt - 1, 2, 1)), 0, 0, 0)

    return pl.pallas_call(
        functools.partial(_lat_na_kernel, rows_n),
        out_shape=jax.ShapeDtypeStruct((b, t, NA_W), BF16),
        grid=(b, nt),
        in_specs=[pl.BlockSpec((1, NA_HEADS, tq, NA_HD), lambda i, j: (i, 0, j, 0)),
                  full(t, NA_HD), full(t, NA_VA), full(past, NA_HD), full(past, NA_VA),
                  pl.BlockSpec((1, NA_HEADS, tq, NA_KROWS * GRID_W), kind),
                  pl.BlockSpec((1, NA_HD), const2), pl.BlockSpec((1, NA_HD), const2),
                  pl.BlockSpec(rpb2.shape, const2)],
        out_specs=pl.BlockSpec((1, tq, NA_W), lambda i, j: (i, j, 0)),
        scratch_shapes=[pltpu.SMEM((1,), jnp.int32)],
        compiler_params=_cparams("arbitrary", "arbitrary"),
        name="lat_na",
    )(q, k, v, ck, cv, bias_tabs, q_gain.reshape(1, NA_HD), k_gain.reshape(1, NA_HD), rpb2)


def _lat_df_kernel(lam_init, q_ref, qall_ref, ktl_ref, vl_ref, ktc_ref, vc_ref, lv_ref, sub_ref, o_ref, safe_ref):
    tq = q_ref.shape[2]
    n_lat = ktl_ref.shape[1]
    n_ctx = ktc_ref.shape[1]
    lam = _df_lambda(lv_ref, lam_init)
    q = q_ref[0, 0]
    qs = [q[:, c * DF_HD:(c + 1) * DF_HD] for c in range(2)]
    rows = [slice(c * DF_HD, (c + 1) * DF_HD) for c in range(2)]

    @pl.when(pl.program_id(2) == 0)
    def _():
        bound = jnp.zeros((1, 1), F32)
        for c in range(2):
            mx = jnp.zeros((1, DF_CK), F32)
            for kt_ref, n in ((ktl_ref, n_lat), (ktc_ref, n_ctx)):
                def longest(j, mx):
                    k = kt_ref[0, j, rows[c], :].astype(F32)
                    return jnp.maximum(mx, jnp.sum(k * k, axis=0, keepdims=True))

                mx = lax.fori_loop(0, n, longest, mx)
            kn = jnp.max(mx, axis=-1, keepdims=True)
            qa = qall_ref[0, 0, :, rows[c]].astype(F32)
            qn = jnp.max(jnp.sum(qa * qa, axis=-1, keepdims=True), axis=0, keepdims=True)
            bound = jnp.maximum(bound, qn * kn)
        safe_ref[0] = (jnp.max(bound) <= DF_SAFE_SCORE ** 2).astype(jnp.int32)

    safe = safe_ref[0] == 1

    def finish(a1, a2):
        o1 = a1[:, 0:DF_VD] / a1[:, DF_VD:DF_VD + 1]
        o2 = a2[:, 0:DF_VD] / a2[:, DF_VD:DF_VD + 1]
        o_ref[0, 0] = _diff_combine(o1, o2, lam, sub_ref[...], lam_init).astype(o_ref.dtype)

    def over_chunks(step, carry):
        def trip(t, carry):
            for u in range(DF_UNROLL):
                j = t * DF_UNROLL + u
                v = vl_ref[0, 0, pl.ds(pl.multiple_of(j * DF_CK, DF_CK), DF_CK), :]
                carry = step(carry, lambda c: ktl_ref[0, j, rows[c], :], v)
            return carry

        carry = lax.fori_loop(0, n_lat // DF_UNROLL, trip, carry)
        for j in range(n_ctx):
            carry = step(carry, lambda c: ktc_ref[0, j, rows[c], :], vc_ref[0, 0, j * DF_CK:(j + 1) * DF_CK, :])
        return carry

    @pl.when(safe)
    def _():
        def step(acc, kt, v):
            e = [jnp.exp(_dot(qs[c], kt(c))).astype(BF16) for c in range(2)]
            return acc + _dot(jnp.concatenate(e, axis=0), v)

        acc = over_chunks(step, jnp.zeros((2 * tq, DF_VA), F32))
        finish(acc[0:tq], acc[tq:])

    @pl.when(jnp.logical_not(safe))
    def _():
        def step(carry, kt, v):
            new = []
            for c in range(2):
                m, acc = carry[c]
                s = _dot(qs[c], kt(c))
                m_new = jnp.maximum(m, jnp.max(s, axis=-1, keepdims=True))
                new.append((m_new, jnp.exp(m - m_new) * acc + _dot(jnp.exp(s - m_new).astype(BF16), v)))
            return tuple(new)

        init = (jnp.full((tq, 1), NEG_INF, F32), jnp.zeros((tq, DF_VA), F32))
        (_, a1), (_, a2) = over_chunks(step, (init, init))
        finish(a1, a2)


def _lat_df(q, kt_lat, v_lat, kt_ctx, v_ctx, df_lambda, df_subln, lam_init, tq):
    b, _, t, _ = q.shape
    assert kt_lat.shape[1] % DF_UNROLL == 0
    kt_spec = lambda kt: pl.BlockSpec((1, kt.shape[1], 2 * DF_HD, DF_CK), lambda i, h, j: (i, 0, h, 0))
    v_spec = lambda v: pl.BlockSpec((1, 1, v.shape[2], DF_VA), lambda i, h, j: (i, h, 0, 0))
    return pl.pallas_call(
        functools.partial(_lat_df_kernel, lam_init),
        out_shape=jax.ShapeDtypeStruct((b, DF_HEADS, t, DF_VD), BF16),
        grid=(b, DF_HEADS, t // tq),
        in_specs=[pl.BlockSpec((1, 1, tq, 2 * DF_HD), lambda i, h, j: (i, h, j, 0)),
                  pl.BlockSpec((1, 1, t, 2 * DF_HD), lambda i, h, j: (i, h, 0, 0)),
                  kt_spec(kt_lat), v_spec(v_lat), kt_spec(kt_ctx), v_spec(v_ctx),
                  pl.BlockSpec((4, DF_HD), lambda i, h, j: (0, 0)),
                  pl.BlockSpec((1, DF_VD), lambda i, h, j: (0, 0))],
        out_specs=pl.BlockSpec((1, 1, tq, DF_VD), lambda i, h, j: (i, h, j, 0)),
        scratch_shapes=[pltpu.SMEM((1,), jnp.int32)],
        compiler_params=_cparams("arbitrary", "arbitrary", "arbitrary"),
        name="lat_df",
    )(q, q, kt_lat, v_lat, kt_ctx, v_ctx, df_lambda, df_subln.reshape(1, DF_VD))


def _with_ones_column(v, width):
    pad = width - v.shape[-1] - 1
    return jnp.concatenate([v, jnp.ones(v.shape[:-1] + (1,), v.dtype), jnp.zeros(v.shape[:-1] + (pad,), v.dtype)],
                           axis=-1)


def _out_kernel(ya_ref, yb_ref, yc_ref, x_ref, ga_ref, w_ref, o_ref):
    if len(yc_ref.shape) == 4:
        yc = jnp.concatenate([yc_ref[0, h] for h in range(DF_HEADS)], axis=-1)
    else:
        yc = yc_ref[0]
    y = (_dot(ya_ref[0], w_ref[0:LRU_W, :]) + _dot(yb_ref[0], w_ref[LRU_W:LRU_W + NA_W, :])
         + _dot(yc, w_ref[LRU_W + NA_W:, :]))
    o_ref[0] = x_ref[0] + ga_ref[0] * y


def _out_proj(latent, ya, yb, yc, x, g_a, w_out, tm):
    b, t, _ = x.shape
    _, ya_spec = _seq_split(latent, b, t, tm, BF16)
    ya = ya.reshape(ya.shape[0], ya.shape[1] // SUBLANES, SUBLANES * LRU_W)
    mod_map = (lambda i, j: (i, 0, 0)) if g_a.shape[0] > 1 else (lambda i, j: (0, 0, 0))
    tile = lambda i, j: (i, j, 0)
    if yc.ndim == 4:
        yc_spec = pl.BlockSpec((1, DF_HEADS, tm, DF_VD), lambda i, j: (i, 0, j, 0))
    else:
        yc_spec = pl.BlockSpec((1, tm, DF_V_W), tile)
    return pl.pallas_call(
        _out_kernel,
        out_shape=jax.ShapeDtypeStruct(x.shape, F32),
        grid=(b, t // tm),
        in_specs=[ya_spec, pl.BlockSpec((1, tm, NA_W), tile),
                  yc_spec, pl.BlockSpec((1, tm, D_MODEL), tile),
                  pl.BlockSpec((1, 1, D_MODEL), mod_map),
                  pl.BlockSpec((D_CAT, D_MODEL), lambda i, j: (0, 0))],
        out_specs=pl.BlockSpec((1, tm, D_MODEL), tile),
        compiler_params=_cparams("arbitrary", "arbitrary"),
        name="out_proj",
    )(ya, yb, yc, x, g_a, w_out)


def _ffn_kernel(nt, seq, xp_ref, x_ref, xn_ref, sh_ref, sc_ref, gf_ref, g_ref, up_ref, cw_ref, cb_ref, dn_ref,
                o_ref, acc_ref):
    j = pl.program_id(1)
    tm = x_ref.shape[1]
    x = x_ref[0]
    n_seq = max(tm // seq, 1)
    span = tm // n_seq
    pieces = [xp_ref[0]]
    for s in range(n_seq):
        pieces += [x[s * span:(s + 1) * span], xn_ref[0]]
    xw = jnp.concatenate(pieces, axis=0)
    rows = xw.shape[0]
    ms = jnp.mean(xw * xw, axis=-1, keepdims=True)
    y = xw * lax.rsqrt(ms + EPS) * g_ref[...]
    h = y * (1.0 + sc_ref[0]) + sh_ref[0]
    row = lax.broadcasted_iota(jnp.int32, (rows, 1), 0)
    if n_seq == 1:
        valid = ((row >= HALO) | (j > 0)) & ((row < HALO + tm) | (j < nt - 1))
    else:
        valid = ((row - HALO) % (span + HALO)) < span
    h = jnp.where(valid, h, 0.0).astype(BF16)

    n_chunks = D_FF // FFN_CF

    def up_proj(c):
        return [_dot(h, up_ref[:, off:off + FFN_CF]) for off in (c * FFN_CF, D_FF + c * FFN_CF)]

    def conv(u, off):
        taps = (pltpu.roll(u, 1, 0), u, pltpu.roll(u, rows - 1, 0))
        out = cb_ref[:, off:off + FFN_CF]
        for k in range(3):
            out = out + cw_ref[k:k + 1, off:off + FFN_CF] * taps[k][HALO:rows - HALO, :]
        if n_seq == 1:
            return out
        return jnp.concatenate([out[s * (span + HALO):s * (span + HALO) + span] for s in range(n_seq)], axis=0)

    u_cur = up_proj(0)
    for c in range(n_chunks):
        u_next = up_proj(c + 1) if c + 1 < n_chunks else None
        act = (_gelu(conv(u_cur[1], D_FF + c * FFN_CF)) * conv(u_cur[0], c * FFN_CF)).astype(BF16)
        down = _dot(act, dn_ref[c * FFN_CF:(c + 1) * FFN_CF, :])
        if c == 0:
            acc_ref[...] = down
        else:
            acc_ref[...] += down
        u_cur = u_next
    o_ref[0] = x + gf_ref[0] * acc_ref[...]


def _ffn(x, shift, scale, g_f, gain, up, conv_w, conv_b, down, tm):
    shape = x.shape
    seq = shape[1]
    if tm > seq:
        assert shift.shape[0] == 1 and tm % seq == 0 and shape[0] % (tm // seq) == 0
        x = x.reshape(shape[0] * seq // tm, tm, D_MODEL)
    b, t, _ = x.shape
    nt = t // tm
    hb = tm // HALO
    mod_map = (lambda i, j: (i, 0, 0)) if shift.shape[0] > 1 else (lambda i, j: (0, 0, 0))
    const2 = lambda i, j: (0, 0)
    mod_spec = pl.BlockSpec((1, 1, D_MODEL), mod_map)
    return pl.pallas_call(
        functools.partial(_ffn_kernel, nt, seq),
        out_shape=jax.ShapeDtypeStruct(x.shape, F32),
        grid=(b, nt),
        in_specs=[pl.BlockSpec((1, HALO, D_MODEL), lambda i, j: (i, jnp.maximum(j * hb - 1, 0), 0)),
                  pl.BlockSpec((1, tm, D_MODEL), lambda i, j: (i, j, 0)),
                  pl.BlockSpec((1, HALO, D_MODEL), lambda i, j: (i, jnp.minimum((j + 1) * hb, t // HALO - 1), 0)),
                  mod_spec, mod_spec, mod_spec,
                  pl.BlockSpec((1, D_MODEL), const2),
                  pl.BlockSpec((D_MODEL, 2 * D_FF), const2),
                  pl.BlockSpec((3, 2 * D_FF), const2),
                  pl.BlockSpec((1, 2 * D_FF), const2),
                  pl.BlockSpec((D_FF, D_MODEL), const2)],
        out_specs=pl.BlockSpec((1, tm, D_MODEL), lambda i, j: (i, j, 0)),
        scratch_shapes=[pltpu.VMEM((tm, D_MODEL), F32)],
        compiler_params=_cparams("arbitrary", "arbitrary"),
        name="ffn",
    )(x, x, x, shift, scale, g_f, gain.reshape(1, D_MODEL), up, conv_w, conv_b.reshape(1, 2 * D_FF), down
      ).reshape(shape)


def _split_mod(mod):
    return [mod[:, None, i * D_MODEL:(i + 1) * D_MODEL] for i in range(N_MOD)]


def kernel(x_prompt, x_sample, state_lru, cache_na_k, cache_na_v, cache_df_k, cache_df_v, c, c_ctx, ada_w, ada_b, norm_mix, norm_ffn, w_in, lru_conv_w, lru_conv_b, lru_wa, lru_ba, lru_wx, lru_bx, lru_lambda, na_q_norm, na_k_norm, na_rpb, df_q_norm, df_k_norm, df_lambda, df_subln, w_out, ffn_up, ffn_conv_w, ffn_conv_b, ffn_down):
    bp, tp, _ = x_prompt.shape
    bs, ts, _ = x_sample.shape
    past = cache_na_k.shape[3]
    rows_n = ts // GRID_W

    c_all = jnp.zeros((SUBLANES, D_MODEL), F32).at[0].set(c_ctx).at[1:1 + bs].set(c)
    mods = _modulation(c_all, ada_w, ada_b)
    rope_tabs = _rope_tables(ts)
    zeros_state = jnp.zeros((bp // SUBLANES, 2, SUBLANES, LRU_W), F32)

    y_p, y_s = x_prompt, x_sample
    s_lru = []
    hcache = (bp, DEPTH, NA_HEADS, tp, NA_HD)
    caches = [jnp.zeros(hcache, F32), jnp.zeros(hcache, F32),
              jnp.zeros((bp, DEPTH, DF_HEADS, 2, tp, DF_HD), F32), jnp.zeros(hcache, F32)]
    for l in range(DEPTH):
        lam_init = 0.8 - 0.6 * math.exp(-0.3 * l)
        w_in_l = w_in[l].astype(BF16)
        w_out_l = w_out[l].astype(BF16)
        up_l = ffn_up[l].astype(BF16)
        down_l = ffn_down[l].astype(BF16)
        lru_args = (lru_conv_w[l], lru_conv_b[l], lru_wa[l], lru_wx[l], lru_ba[l], lru_bx[l], lru_lambda[l])
        norm_args = (na_q_norm[l], na_k_norm[l], df_q_norm[l], df_k_norm[l])

        sh_a, sc_a, g_a, sh_f, sc_f, g_f = _split_mod(mods[l, 0:1])
        lx, lg, nq, *caches = _prep(False, y_p, sh_a, sc_a, norm_mix[l], w_in_l, *norm_args, None, tp, l, caches)
        dq = caches.pop(2)
        seq4 = lambda v: v.reshape(v.shape[0], v.shape[1], SUBLANES, LRU_W)
        ya, st = _lru(False, seq4(lx), seq4(lg), *lru_args, zeros_state, 256, LRU_RC)
        yb, yc = _ctx_attn(nq, caches[0], caches[1], dq, caches[2], caches[3], df_lambda[l], df_subln[l], lam_init, l)
        x1 = _out_proj(False, ya, yb, yc, y_p, g_a, w_out_l, tp)
        y_p = _ffn(x1, sh_f, sc_f, g_f, norm_ffn[l], up_l, ffn_conv_w[l], ffn_conv_b[l], down_l, FFN_TM)
        s_lru.append(st.transpose(0, 2, 1, 3).reshape(bp, 2, LRU_W))

        sh_a, sc_a, g_a, sh_f, sc_f, g_f = _split_mod(mods[l, 1:1 + bs])
        lx, lg, nq, nk, nv, dq, dk, dv = _prep(True, y_s, sh_a, sc_a, norm_mix[l], w_in_l, *norm_args, rope_tabs,
                                               ts // SUBLANES)
        h0 = jnp.broadcast_to(state_lru[:, l, :, None, :], (bs, 2, SUBLANES, LRU_W))
        ya, _ = _lru(True, seq4(lx), seq4(lg), *lru_args, h0, 128, LRU_RC)
        yb = _lat_na(nq, nk, nv, cache_na_k[:, l].astype(BF16), _with_ones_column(cache_na_v[:, l].astype(BF16), NA_VA),
                     _na_bias_tables(na_rpb[l], rows_n), na_q_norm[l], na_k_norm[l], na_rpb[l])
        kt_ctx = jnp.swapaxes(cache_df_k[:, l].astype(BF16), -1, -2).reshape(bs, DF_QK_W, past // DF_CK, DF_CK)
        kt_ctx = kt_ctx.transpose(0, 2, 1, 3)
        v_ctx = _with_ones_column(cache_df_v[:, l].astype(BF16), DF_VA)
        yc = _lat_df(dq, dk, dv, kt_ctx, v_ctx, df_lambda[l], df_subln[l], lam_init, DF_TQ)
        x1 = _out_proj(True, ya, yb, yc, y_s, g_a, w_out_l, ts // SUBLANES)
        y_s = _ffn(x1, sh_f, sc_f, g_f, norm_ffn[l], up_l, ffn_conv_w[l], ffn_conv_b[l], down_l, FFN_TM)

    return (y_p, y_s, jnp.stack(s_lru, axis=1), *caches)
```

```python
import functools
import math

import numpy as np
import jax
import jax.numpy as jnp
from jax import lax
from jax.experimental import pallas as pl
from jax.experimental.pallas import tpu as pltpu

F32 = jnp.float32
BF16 = jnp.bfloat16

D_MODEL = 1024
DEPTH = 2
GRID_W = 64
LRU_W = 512
LRU_BLOCKS = 8
LRU_BW = LRU_W // LRU_BLOCKS
LRU_CONV = 4
LRU_C = 8.0
NA_HEADS = 4
NA_HD = 64
NA_W = NA_HEADS * NA_HD
NA_ROWS = 8
NA_COLS = 16
DF_HEADS = 4
DF_HD = 32
DF_VD = 2 * DF_HD
DF_QK_W = DF_HEADS * 2 * DF_HD
DF_V_W = DF_HEADS * DF_VD
D_IN = 2 * LRU_W + 3 * NA_W + 2 * DF_QK_W + DF_V_W
D_CAT = LRU_W + NA_W + DF_V_W
D_FF = 2816
ROPE_BASE = 10000.0
EPS = 1e-6
NEG_INF = -1e30
N_MOD = 6

OFF_LX = 0
OFF_LG = OFF_LX + LRU_W
OFF_NQ = OFF_LG + LRU_W
OFF_NK = OFF_NQ + NA_W
OFF_NV = OFF_NK + NA_W
OFF_DQ = OFF_NV + NA_W
OFF_DK = OFF_DQ + DF_QK_W
OFF_DV = OFF_DK + DF_QK_W

SUBLANES = 8
VMEM_LIMIT = 56 * 1024 * 1024

NA_QROWS = 4
NA_KROWS = NA_QROWS + NA_ROWS
DF_CK = 512
DF_TQ = 512
DF_VA = 128
NA_VA = DF_VA
DF_SAFE_SCORE = 50.0
DF_UNROLL = 8
LRU_RC = 32
LRU_UNROLL = 8
FFN_CF = 256
FFN_TM = 256
HALO = SUBLANES


def _cparams(*sem):
    return pltpu.CompilerParams(dimension_semantics=sem, vmem_limit_bytes=VMEM_LIMIT)


def _gelu(x):
    return 0.5 * x * (1.0 + jnp.tanh(math.sqrt(2.0 / math.pi) * (x + 0.044715 * (x * x * x))))


def _sigmoid(x):
    return 1.0 / (1.0 + jnp.exp(-x))


def _neg_expm1(x, u):
    safe = u != 1.0
    return jnp.where(safe, (1.0 - u) * x / jnp.log(jnp.where(safe, u, 0.5)), -x)


def _dot(a, b):
    return jnp.dot(a, b, preferred_element_type=F32)


def _dot_nt(a, b):
    return lax.dot_general(a, b, (((1,), (1,)), ((), ())), preferred_element_type=F32)


def _mod_kernel(c_ref, w_ref, b_ref, o_ref):
    c = c_ref[...]
    s = (c * _sigmoid(c)).astype(BF16)
    o_ref[0] = _dot(s, w_ref[0].astype(BF16)) + b_ref[0]


def _modulation(c_all, ada_w, ada_b):
    nb = 1536
    n = N_MOD * D_MODEL
    return pl.pallas_call(
        _mod_kernel,
        out_shape=jax.ShapeDtypeStruct((DEPTH, SUBLANES, n), F32),
        grid=(DEPTH, n // nb),
        in_specs=[pl.BlockSpec((SUBLANES, D_MODEL), lambda l, j: (0, 0)),
                  pl.BlockSpec((1, D_MODEL, nb), lambda l, j: (l, 0, j)),
                  pl.BlockSpec((1, 1, nb), lambda l, j: (l, 0, j))],
        out_specs=pl.BlockSpec((1, SUBLANES, nb), lambda l, j: (l, 0, j)),
        compiler_params=_cparams("arbitrary", "arbitrary"),
        name="modulation",
    )(c_all, ada_w, ada_b.reshape(DEPTH, 1, n))


def _group_rms(z, g_ref, gain):
    ms = _dot((z * z).astype(BF16), g_ref[...])
    return z * lax.rsqrt(ms + EPS) * gain


def _rope(x, cos, sin):
    w = x.shape[-1]
    lane = lax.broadcasted_iota(jnp.int32, x.shape, 1)
    partner = jnp.where((lane % 16) < 8, pltpu.roll(x, w - 8, 1), pltpu.roll(x, 8, 1))
    return x * cos + partner * sin


def _prep_kernel(latent, x_ref, sh_ref, sc_ref, g_ref, w_ref, qn_ref, kn_ref, dqn_ref, dkn_ref,
                 g64_ref, g32_ref, *rest):
    if latent:
        cos_ref, sin_ref, lx_ref, lg_ref, nq_ref, nk_ref, nv_ref, dq_ref, dk_ref, dv_ref = rest
    else:
        lx_ref, lg_ref, nq_ref, nk_ref, nv_ref, dq_ref, dk_ref, dv_ref = rest[-8:]
        nk_ref, nv_ref, dk_ref, dv_ref = nk_ref.at[0], nv_ref.at[0], dk_ref.at[0], dv_ref.at[0]
    x = x_ref[0]
    ms = jnp.mean(x * x, axis=-1, keepdims=True)
    y = x * lax.rsqrt(ms + EPS) * g_ref[...]
    h = (y * (1.0 + sc_ref[0]) + sh_ref[0]).astype(BF16)

    def seg(off, size):
        return _dot(h, w_ref[:, off:off + size])

    lx_ref[0] = seg(OFF_LX, LRU_W)
    lg_ref[0] = seg(OFF_LG, LRU_W)

    nq = _group_rms(seg(OFF_NQ, NA_W), g64_ref, qn_ref[...]) * (NA_HD ** -0.5)
    nk = _group_rms(seg(OFF_NK, NA_W), g64_ref, kn_ref[...])
    nv = seg(OFF_NV, NA_W)
    dq = _group_rms(seg(OFF_DQ, DF_QK_W), g32_ref, dqn_ref[...])
    dk = _group_rms(seg(OFF_DK, DF_QK_W), g32_ref, dkn_ref[...])
    dv = seg(OFF_DV, DF_V_W)
    if latent:
        cos = cos_ref[...]
        sin = sin_ref[...]
        dq = _rope(dq, cos, sin)
        dk = _rope(dk, cos, sin)
    dq = dq * (DF_HD ** -0.5)

    for hh in range(NA_HEADS):
        sl = slice(hh * NA_HD, (hh + 1) * NA_HD)
        nq_ref[0, hh] = nq[:, sl].astype(nq_ref.dtype)
        nk_ref[0, hh] = nk[:, sl].astype(nk_ref.dtype)
        if latent:
            dq_ref[0, hh] = dq[:, sl].astype(dq_ref.dtype)
            lane = lax.broadcasted_iota(jnp.int32, (dv.shape[0], DF_VA - DF_VD), 1)
            ones_col = jnp.where(lane == 0, 1.0, 0.0).astype(dv_ref.dtype)
            for val, ref in ((nv, nv_ref), (dv, dv_ref)):
                ref[0, hh, :, 0:DF_VD] = val[:, sl].astype(ref.dtype)
                ref[0, hh, :, DF_VD:] = ones_col
        else:
            nv_ref[0, hh] = nv[:, sl].astype(nv_ref.dtype)
            dv_ref[0, hh] = dv[:, sl].astype(dv_ref.dtype)
    if latent:
        dk_ref[0, 0] = dk.T.astype(dk_ref.dtype)
    else:
        for hc in range(2 * DF_HEADS):
            sl = slice(hc * DF_HD, (hc + 1) * DF_HD)
            dq_ref[0, hc // 2, hc % 2] = dq[:, sl].astype(dq_ref.dtype)
            dk_ref[0, hc // 2, hc % 2] = dk[:, sl].astype(dk_ref.dtype)


def _seq_split(latent, b, t, tm, dtype):
    if latent:
        assert tm * SUBLANES == t
        return (jax.ShapeDtypeStruct((b, tm, SUBLANES * LRU_W), dtype),
                pl.BlockSpec((1, tm, LRU_W), lambda i, j: (i, 0, j)))
    assert tm == t and b % SUBLANES == 0
    return (jax.ShapeDtypeStruct((b // SUBLANES, t, SUBLANES * LRU_W), dtype),
            pl.BlockSpec((1, tm, LRU_W), lambda i, j: (i // SUBLANES, 0, i % SUBLANES)))


def _group_matrix(width, group):
    idx = np.arange(width) // group
    return jnp.asarray((idx[:, None] == idx[None, :]).astype(np.float32) / group, dtype=BF16)


def _prep(latent, x, shift, scale, gain, w_in, qn, kn, dqn, dkn, rope_tabs, tm, layer=0, caches=None):
    b, t, _ = x.shape
    aliases = {}
    bm = shift.shape[0]
    mod_map = (lambda i, j: (i, 0, 0)) if bm > 1 else (lambda i, j: (0, 0, 0))
    const2 = lambda i, j: (0, 0)
    tile3 = lambda i, j: (i, j, 0)
    head4 = lambda i, j: (i, 0, j, 0)
    in_specs = [pl.BlockSpec((1, tm, D_MODEL), tile3),
                pl.BlockSpec((1, 1, D_MODEL), mod_map),
                pl.BlockSpec((1, 1, D_MODEL), mod_map),
                pl.BlockSpec((1, D_MODEL), const2),
                pl.BlockSpec((D_MODEL, D_IN), const2),
                pl.BlockSpec((1, NA_W), const2),
                pl.BlockSpec((1, NA_W), const2),
                pl.BlockSpec((1, DF_QK_W), const2),
                pl.BlockSpec((1, DF_QK_W), const2),
                pl.BlockSpec((NA_W, NA_W), const2),
                pl.BlockSpec((DF_QK_W, DF_QK_W), const2)]
    args = [x, shift, scale, gain.reshape(1, D_MODEL), w_in,
            jnp.tile(qn, NA_HEADS).reshape(1, NA_W), jnp.tile(kn, NA_HEADS).reshape(1, NA_W),
            jnp.tile(dqn, 2 * DF_HEADS).reshape(1, DF_QK_W), jnp.tile(dkn, 2 * DF_HEADS).reshape(1, DF_QK_W),
            _group_matrix(NA_W, NA_HD), _group_matrix(DF_QK_W, DF_HD)]
    head_shape = lambda dt: jax.ShapeDtypeStruct((b, NA_HEADS, t, NA_HD), dt)
    head_spec = pl.BlockSpec((1, NA_HEADS, tm, NA_HD), head4)
    seq_shape, seq_spec = _seq_split(latent, b, t, tm, F32)
    out_shape = [seq_shape, seq_shape]
    out_specs = [seq_spec, seq_spec]
    if latent:
        in_specs += [pl.BlockSpec((tm, DF_QK_W), lambda i, j: (j, 0))] * 2
        args += list(rope_tabs)
        aug_shape = jax.ShapeDtypeStruct((b, DF_HEADS, t, DF_VA), BF16)
        aug_spec = pl.BlockSpec((1, DF_HEADS, tm, DF_VA), head4)
        out_shape += [head_shape(BF16), head_shape(BF16), aug_shape,
                      head_shape(BF16), jax.ShapeDtypeStruct((b, t // tm, DF_QK_W, tm), BF16), aug_shape]
        out_specs += [head_spec, head_spec, aug_spec,
                      head_spec, pl.BlockSpec((1, 1, DF_QK_W, tm), lambda i, j: (i, j, 0, 0)), aug_spec]
    else:
        comp_shape = jax.ShapeDtypeStruct((b, DF_HEADS, 2, t, DF_HD), BF16)
        comp_spec = pl.BlockSpec((1, DF_HEADS, 2, tm, DF_HD), lambda i, j: (i, 0, 0, j, 0))
        hcache_shape = jax.ShapeDtypeStruct((b, DEPTH, NA_HEADS, t, NA_HD), F32)
        hcache_spec = pl.BlockSpec((1, 1, NA_HEADS, tm, NA_HD), lambda i, j: (i, layer, 0, j, 0))
        ccache_shape = jax.ShapeDtypeStruct((b, DEPTH, DF_HEADS, 2, t, DF_HD), F32)
        ccache_spec = pl.BlockSpec((1, 1, DF_HEADS, 2, tm, DF_HD), lambda i, j: (i, layer, 0, 0, j, 0))
        out_shape += [head_shape(BF16), hcache_shape, hcache_shape, comp_shape, ccache_shape, hcache_shape]
        out_specs += [head_spec, hcache_spec, hcache_spec, comp_spec, ccache_spec, hcache_spec]
        aliases = {len(args) + k: out_idx for k, out_idx in enumerate((3, 4, 6, 7))}
        in_specs += [pl.BlockSpec(memory_space=pl.ANY)] * 4
        args += list(caches)
    return pl.pallas_call(
        functools.partial(_prep_kernel, latent),
        out_shape=out_shape,
        grid=(b, t // tm),
        in_specs=in_specs,
        out_specs=out_specs,
        input_output_aliases=aliases,
        compiler_params=_cparams("arbitrary", "arbitrary"),
        name="prep_latent" if latent else "prep_context",
    )(*args)


def _rope_tables(t):
    half = DF_HD // 4
    freqs = ROPE_BASE ** (-jnp.arange(half, dtype=F32) / half)
    tok = jnp.arange(t)
    lane = np.arange(DF_QK_W)
    d = lane % DF_HD
    pos = jnp.where((d < DF_HD // 2)[None, :], (tok // GRID_W)[:, None], (tok % GRID_W)[:, None])
    ang = pos.astype(F32) * freqs[d % half][None, :]
    sign = np.where((d % (2 * half)) < half, -1.0, 1.0).astype(np.float32)
    return jnp.cos(ang), jnp.sin(ang) * sign[None, :]


def _lru_kernel(chained, jn, rc, lx_ref, lg_ref, cw_ref, cb_ref, wg_ref, gb_ref, lam_ref, h0_ref,
                ya_ref, st_ref, xpad_ref, af_ref, bf_ref, ab_ref, bb_ref, hf_ref):
    ct = lx_ref.shape[-1]
    row = lax.broadcasted_iota(jnp.int32, (SUBLANES, ct), 0)
    zero = jnp.zeros((SUBLANES, ct), F32)

    def from_prev_seq(v):
        return jnp.where(row >= 1, pltpu.roll(v, 1, 0), 0.0) if chained else zero

    def from_next_seq(v):
        return jnp.where(row < SUBLANES - 1, pltpu.roll(v, SUBLANES - 1, 0), 0.0) if chained else zero

    xpad_ref[0] = from_prev_seq(lx_ref[0, jn - 1])
    xpad_ref[1:jn + 1] = lx_ref[0]
    xpad_ref[jn + 1] = from_next_seq(lx_ref[0, 0])
    xpad_ref[jn + 2] = from_next_seq(lx_ref[0, 1])

    lam = -lam_ref[...]
    sp = jnp.maximum(lam, 0.0) + jnp.log1p(jnp.exp(-jnp.abs(lam)))
    cw = cw_ref[...]
    cb = cb_ref[...]
    wg = wg_ref[0]
    gb = gb_ref[0]

    def gates(i, carry):
        r0 = pl.multiple_of(i * rc, rc)
        xw = xpad_ref[pl.ds(r0, rc + LRU_CONV - 1)]
        xc = cb
        for j in range(LRU_CONV):
            xc = xc + cw[j:j + 1, :] * xw[j:j + rc]
        xc = xc.reshape(rc * SUBLANES, ct)
        g = _dot(xc.astype(BF16), wg) + gb
        for d, (a_ref, b_ref) in enumerate(((af_ref, bf_ref), (ab_ref, bb_ref))):
            r = _sigmoid(g[:, (2 * d) * ct:(2 * d + 1) * ct])
            ig = _sigmoid(g[:, (2 * d + 1) * ct:(2 * d + 2) * ct])
            log_a = (-LRU_C) * r * sp[d:d + 1, :]
            a = jnp.exp(log_a)
            b = jnp.sqrt(_neg_expm1(2.0 * log_a, a * a)) * (ig * xc)
            a_ref[pl.ds(r0, rc)] = a.reshape(rc, SUBLANES, ct)
            b_ref[pl.ds(r0, rc)] = b.reshape(rc, SUBLANES, ct)
        return carry

    lax.fori_loop(0, jn // rc, gates, 0)

    hf0 = h0_ref[0, 0]
    hb0 = h0_ref[0, 1]
    if chained:
        def summarise(j, carry):
            hf, pf, hb, pb = carry
            a = af_ref[j]
            jb = jn - 1 - j
            a2 = ab_ref[jb]
            return a * hf + bf_ref[j], a * pf, a2 * hb + bb_ref[jb], a2 * pb

        one = jnp.ones((SUBLANES, ct), F32)
        hf_end, pf_end, hb_end, pb_end = lax.fori_loop(0, jn, summarise, (zero, one, zero, one), unroll=LRU_UNROLL)
        for s in range(SUBLANES - 1):
            hf0 = jnp.where(row == s + 1, pltpu.roll(hf_end + pf_end * hf0, 1, 0), hf0)
            sb = SUBLANES - 2 - s
            hb0 = jnp.where(row == sb, pltpu.roll(hb_end + pb_end * hb0, SUBLANES - 1, 0), hb0)

    def fwd(j, h):
        h = af_ref[j] * h + bf_ref[j]
        hf_ref[j] = h
        return h

    st_ref[0, 0] = lax.fori_loop(0, jn, fwd, hf0, unroll=LRU_UNROLL)

    def bwd(j, h):
        jb = jn - 1 - j
        h = ab_ref[jb] * h + bb_ref[jb]
        hf_ref[jb] = hf_ref[jb] + h
        return h

    st_ref[0, 1] = lax.fori_loop(0, jn, bwd, hb0, unroll=LRU_UNROLL)

    def emit(i, carry):
        r0 = pl.multiple_of(i * rc, rc)
        y = (hf_ref[pl.ds(r0, rc)] * _gelu(lg_ref[0, pl.ds(r0, rc)])).reshape(rc * SUBLANES, ct)
        ya_ref[0, pl.ds(pl.multiple_of(r0 * SUBLANES, rc * SUBLANES), rc * SUBLANES), :] = y.astype(ya_ref.dtype)
        return carry

    lax.fori_loop(0, jn // rc, emit, 0)


def _lru_gate_weights(wa, wx, ba, bx, ct):
    nct = LRU_W // ct
    per = ct // LRU_BW
    eye = jnp.eye(per, dtype=F32)

    def dense(w):
        wt = w.reshape(nct, per, LRU_BW, LRU_BW)
        return jnp.einsum('cpij,pq->cpiqj', wt, eye).reshape(nct, ct, ct)

    mats = [dense(wa[0]), dense(wx[0]), dense(wa[1]), dense(wx[1])]
    bias = [ba[0], bx[0], ba[1], bx[1]]
    wg = jnp.concatenate(mats, axis=-1).astype(BF16)
    gb = jnp.concatenate([v.reshape(nct, 1, ct) for v in bias], axis=-1)
    return wg, gb


def _lru(chained, lx, lg, conv_w, conv_b, wa, wx, ba, bx, lam, h0, ct, rc):
    g, jn, _, _ = lx.shape
    nct = LRU_W // ct
    wg, gb = _lru_gate_weights(wa, wx, ba, bx, ct)
    tile = lambda i, j: (i, 0, 0, j)
    seq = (jn, SUBLANES, ct)
    return pl.pallas_call(
        functools.partial(_lru_kernel, chained, jn, rc),
        out_shape=[jax.ShapeDtypeStruct((g, jn * SUBLANES, LRU_W), BF16),
                   jax.ShapeDtypeStruct((g, 2, SUBLANES, LRU_W), F32)],
        grid=(g, nct),
        in_specs=[pl.BlockSpec((1,) + seq, tile),
                  pl.BlockSpec((1,) + seq, tile),
                  pl.BlockSpec((LRU_CONV, ct), lambda i, j: (0, j)),
                  pl.BlockSpec((1, ct), lambda i, j: (0, j)),
                  pl.BlockSpec((1, ct, 4 * ct), lambda i, j: (j, 0, 0)),
                  pl.BlockSpec((1, 1, 4 * ct), lambda i, j: (j, 0, 0)),
                  pl.BlockSpec((2, ct), lambda i, j: (0, j)),
                  pl.BlockSpec((1, 2, SUBLANES, ct), tile)],
        out_specs=[pl.BlockSpec((1, jn * SUBLANES, ct), lambda i, j: (i, 0, j)),
                   pl.BlockSpec((1, 2, SUBLANES, ct), tile)],
        scratch_shapes=[pltpu.VMEM((jn + LRU_CONV - 1, SUBLANES, ct), F32)] + [pltpu.VMEM(seq, F32)] * 5,
        compiler_params=_cparams("arbitrary", "arbitrary"),
        name="lru_latent" if chained else "lru_context",
    )(lx, lg, conv_w, conv_b.reshape(1, LRU_W), wg, gb, lam, h0)


def _df_lambda(lv_ref, lam_init):
    lv = lv_ref[...]
    l1 = jnp.sum(lv[0:1, :] * lv[1:2, :], axis=-1, keepdims=True)
    l2 = jnp.sum(lv[2:3, :] * lv[3:4, :], axis=-1, keepdims=True)
    return jnp.exp(l1) - jnp.exp(l2) + lam_init


def _diff_combine(o1, o2, lam, subln, lam_init):
    y = o1 - lam * o2
    ms = jnp.mean(y * y, axis=-1, keepdims=True)
    return y * lax.rsqrt(ms + EPS) * subln * (1.0 - lam_init)


def _ctx_attn_kernel(lam_init, nq_ref, nk_ref, nv_ref, dq_ref, dk_ref, dv_ref, lv_ref, sub_ref,
                     nqg_ref, nkg_ref, dqg_ref, dkg_ref, yb_ref, yc_ref, va_ref, safe_ref):
    nk_ref, nv_ref, dk_ref, dv_ref = nk_ref.at[0], nv_ref.at[0], dk_ref.at[0], dv_ref.at[0]
    t = nq_ref.shape[2]
    lam = _df_lambda(lv_ref, lam_init)
    sub = sub_ref[...]

    @pl.when(pl.program_id(0) == 0)
    def _():
        gmax = lambda ref: jnp.max(jnp.abs(ref[...]))
        bound = jnp.maximum(gmax(nqg_ref) * gmax(nkg_ref) * NA_HD ** 0.5, gmax(dqg_ref) * gmax(dkg_ref) * DF_HD ** 0.5)
        safe_ref[0] = (bound <= DF_SAFE_SCORE).astype(jnp.int32)
        lane = lax.broadcasted_iota(jnp.int32, (t, NA_VA - NA_HD), 1)
        va_ref[:, NA_HD:] = jnp.where(lane == 0, 1.0, 0.0).astype(va_ref.dtype)

    safe = safe_ref[0] == 1

    def heads(fast):
        def attend(q, k):
            s = _dot_nt(q, k.astype(BF16))
            if not fast:
                s = s - jnp.max(s, axis=-1, keepdims=True)
            o = _dot(jnp.exp(s).astype(BF16), va_ref[...])
            return o[:, 0:NA_HD] / o[:, NA_HD:NA_HD + 1]

        for h in range(NA_HEADS):
            va_ref[:, 0:NA_HD] = nv_ref[0, h].astype(va_ref.dtype)
            yb_ref[0, :, h * NA_HD:(h + 1) * NA_HD] = attend(nq_ref[0, h], nk_ref[0, h]).astype(yb_ref.dtype)
        for h in range(DF_HEADS):
            va_ref[:, 0:DF_VD] = dv_ref[0, h].astype(va_ref.dtype)
            o1 = attend(dq_ref[0, h, 0], dk_ref[0, h, 0])
            o2 = attend(dq_ref[0, h, 1], dk_ref[0, h, 1])
            yc_ref[0, :, h * DF_VD:(h + 1) * DF_VD] = _diff_combine(o1, o2, lam, sub, lam_init).astype(yc_ref.dtype)

    @pl.when(safe)
    def _():
        heads(True)

    @pl.when(jnp.logical_not(safe))
    def _():
        heads(False)


def _ctx_attn(nq, nk, nv, dq, dk, dv, df_lambda, df_subln, gains, lam_init, layer):
    b, _, t, _ = nq.shape
    head = pl.BlockSpec((1, NA_HEADS, t, NA_HD), lambda i: (i, 0, 0, 0))
    comp = pl.BlockSpec((1, DF_HEADS, 2, t, DF_HD), lambda i: (i, 0, 0, 0, 0))
    hcache = pl.BlockSpec((1, 1, NA_HEADS, t, NA_HD), lambda i: (i, layer, 0, 0, 0))
    ccache = pl.BlockSpec((1, 1, DF_HEADS, 2, t, DF_HD), lambda i: (i, layer, 0, 0, 0, 0))
    out = pl.BlockSpec((1, t, NA_W), lambda i: (i, 0, 0))
    return pl.pallas_call(
        functools.partial(_ctx_attn_kernel, lam_init),
        out_shape=[jax.ShapeDtypeStruct((b, t, NA_W), BF16), jax.ShapeDtypeStruct((b, t, DF_V_W), BF16)],
        grid=(b,),
        in_specs=[head, hcache, hcache, comp, ccache, hcache,
                  pl.BlockSpec((4, DF_HD), lambda i: (0, 0)),
                  pl.BlockSpec((1, DF_VD), lambda i: (0, 0))]
                 + [pl.BlockSpec((1, g.shape[0]), lambda i: (0, 0)) for g in gains],
        out_specs=[out, out],
        scratch_shapes=[pltpu.VMEM((t, NA_VA), BF16), pltpu.SMEM((1,), jnp.int32)],
        compiler_params=_cparams("arbitrary"),
        name="ctx_attn",
    )(nq, nk, nv, dq, dk, dv, df_lambda, df_subln.reshape(1, DF_VD), *[g.reshape(1, -1) for g in gains])


def _na_bias_tables(rpb, rows_n):
    n_dr, n_dc = 2 * NA_ROWS - 1, 2 * NA_COLS - 1
    qc = np.arange(GRID_W)[:, None]
    kc = np.arange(GRID_W)[None, :]
    cs = np.clip(qc - NA_COLS // 2, 0, GRID_W - NA_COLS)
    ok_col = (kc >= cs) & (kc < cs + NA_COLS)
    d_col = np.clip(kc - qc + NA_COLS - 1, 0, n_dc - 1)
    sel_col = (d_col[None] == np.arange(n_dc)[:, None, None]) & ok_col[None]
    sel_row = np.zeros((3, NA_QROWS, NA_KROWS, n_dr), np.float32)
    for kind, r0 in enumerate((0, NA_QROWS, rows_n - NA_QROWS)):
        kstart = int(np.clip(r0 - NA_ROWS // 2, 0, rows_n - NA_KROWS))
        r = r0 + np.arange(NA_QROWS)[:, None]
        kr = kstart + np.arange(NA_KROWS)[None, :]
        rs = np.clip(r - NA_ROWS // 2, 0, rows_n - NA_ROWS)
        ok_row = (kr >= rs) & (kr < rs + NA_ROWS)
        d_row = np.clip(kr - r + NA_ROWS - 1, 0, n_dr - 1)
        sel_row[kind] = (d_row[..., None] == np.arange(n_dr)) & ok_row[..., None]
    ok = (sel_row.sum(-1) > 0)[:, :, None, :, None] & ok_col[None, None, :, None, :]
    hi = lax.Precision.HIGHEST
    cols = jnp.einsum('had,dqk->haqk', rpb, jnp.asarray(sel_col, F32), precision=hi)
    tabs = jnp.einsum('tija,haqk->thiqjk', jnp.asarray(sel_row), cols, precision=hi)
    tabs = jnp.where(jnp.asarray(ok)[:, None], tabs, NEG_INF)
    return tabs.reshape(3, NA_HEADS, NA_QROWS * GRID_W, NA_KROWS * GRID_W)


def _lat_na_kernel(rows_n, q_ref, k_ref, v_ref, ck_ref, cv_ref, bias_ref, qg_ref, kg_ref, rpb_ref, o_ref, safe_ref):
    i = pl.program_id(1)
    kstart = jnp.clip(i * NA_QROWS - NA_ROWS // 2, 0, rows_n - NA_KROWS)
    k0 = pl.multiple_of(kstart * GRID_W, NA_QROWS * GRID_W)
    nk = NA_KROWS * GRID_W

    @pl.when(i == 0)
    def _():
        ck2 = jnp.zeros((1, 1), F32)
        for h in range(NA_HEADS):
            c = ck_ref[0, h].astype(F32)
            ck2 = jnp.maximum(ck2, jnp.max(jnp.sum(c * c, axis=-1, keepdims=True), axis=0, keepdims=True))
        k_max = jnp.maximum(jnp.max(jnp.abs(kg_ref[...])) * NA_HD ** 0.5, jnp.sqrt(jnp.max(ck2)))
        bound = jnp.max(jnp.abs(qg_ref[...])) * k_max + jnp.max(jnp.abs(rpb_ref[...]))
        safe_ref[0] = (bound <= DF_SAFE_SCORE).astype(jnp.int32)

    safe = safe_ref[0] == 1

    def heads(fast):
        for h in range(NA_HEADS):
            q = q_ref[0, h]
            s_loc = _dot_nt(q, k_ref[0, h, pl.ds(k0, nk), :]) + bias_ref[0, h]
            s_ctx = _dot_nt(q, ck_ref[0, h])
            if not fast:
                m = jnp.maximum(jnp.max(s_loc, axis=-1, keepdims=True), jnp.max(s_ctx, axis=-1, keepdims=True))
                s_loc = s_loc - m
                s_ctx = s_ctx - m
            o = (_dot(jnp.exp(s_loc).astype(BF16), v_ref[0, h, pl.ds(k0, nk), :])
                 + _dot(jnp.exp(s_ctx).astype(BF16), cv_ref[0, h]))
            o = o[:, 0:NA_HD] / o[:, NA_HD:NA_HD + 1]
            o_ref[0, :, h * NA_HD:(h + 1) * NA_HD] = o.astype(o_ref.dtype)

    @pl.when(safe)
    def _():
        heads(True)

    @pl.when(jnp.logical_not(safe))
    def _():
        heads(False)


def _lat_na(q, k, v, ck, cv, bias_tabs, q_gain, k_gain, rpb):
    b, _, t, _ = q.shape
    rows_n = t // GRID_W
    nt = rows_n // NA_QROWS
    tq = NA_QROWS * GRID_W
    past = ck.shape[2]
    full = lambda n, w: pl.BlockSpec((1, NA_HEADS, n, w), lambda i, j: (i, 0, 0, 0))
    const2 = lambda i, j: (0, 0)
    rpb2 = rpb.reshape(-1, rpb.shape[-1])

    def kind(i, j):
        return (jnp.where(j == 0, 0, jnp.where(j == nt - 1, 2, 1)), 0, 0, 0)

    return pl.pallas_call(
        functools.partial(_lat_na_kernel, rows_n),
        out_shape=jax.ShapeDtypeStruct((b, t, NA_W), BF16),
        grid=(b, nt),
        in_specs=[pl.BlockSpec((1, NA_HEADS, tq, NA_HD), lambda i, j: (i, 0, j, 0)),
                  full(t, NA_HD), full(t, NA_VA), full(past, NA_HD), full(past, NA_VA),
                  pl.BlockSpec((1, NA_HEADS, tq, NA_KROWS * GRID_W), kind),
                  pl.BlockSpec((1, NA_HD), const2), pl.BlockSpec((1, NA_HD), const2),
                  pl.BlockSpec(rpb2.shape, const2)],
        out_specs=pl.BlockSpec((1, tq, NA_W), lambda i, j: (i, j, 0)),
        scratch_shapes=[pltpu.SMEM((1,), jnp.int32)],
        compiler_params=_cparams("arbitrary", "arbitrary"),
        name="lat_na",
    )(q, k, v, ck, cv, bias_tabs, q_gain.reshape(1, NA_HD), k_gain.reshape(1, NA_HD), rpb2)


def _lat_df_kernel(lam_init, q_ref, ktl_ref, vl_ref, ktc_ref, vc_ref, lv_ref, sub_ref, qg_ref, kg_ref,
                   o_ref, safe_ref):
    tq = q_ref.shape[2]
    n_lat = ktl_ref.shape[1]
    n_ctx = ktc_ref.shape[1]
    lam = _df_lambda(lv_ref, lam_init)
    q = q_ref[0, 0]
    qs = [q[:, c * DF_HD:(c + 1) * DF_HD] for c in range(2)]
    rows = [slice(c * DF_HD, (c + 1) * DF_HD) for c in range(2)]

    @pl.when(pl.program_id(2) == 0)
    def _():
        ck2 = jnp.zeros((1, DF_CK), F32)
        for c in range(2):
            for j in range(n_ctx):
                k = ktc_ref[0, j, rows[c], :].astype(F32)
                ck2 = jnp.maximum(ck2, jnp.sum(k * k, axis=0, keepdims=True))
        k_max = jnp.maximum(jnp.max(jnp.abs(kg_ref[...])) * DF_HD ** 0.5, jnp.sqrt(jnp.max(ck2)))
        safe_ref[0] = (jnp.max(jnp.abs(qg_ref[...])) * k_max <= DF_SAFE_SCORE).astype(jnp.int32)

    safe = safe_ref[0] == 1

    def finish(a1, a2):
        o1 = a1[:, 0:DF_VD] / a1[:, DF_VD:DF_VD + 1]
        o2 = a2[:, 0:DF_VD] / a2[:, DF_VD:DF_VD + 1]
        o_ref[0, 0] = _diff_combine(o1, o2, lam, sub_ref[...], lam_init).astype(o_ref.dtype)

    def over_chunks(step, carry):
        def trip(t, carry):
            for u in range(DF_UNROLL):
                j = t * DF_UNROLL + u
                v = vl_ref[0, 0, pl.ds(pl.multiple_of(j * DF_CK, DF_CK), DF_CK), :]
                carry = step(carry, lambda c: ktl_ref[0, j, rows[c], :], v)
            return carry

        carry = lax.fori_loop(0, n_lat // DF_UNROLL, trip, carry)
        for j in range(n_ctx):
            carry = step(carry, lambda c: ktc_ref[0, j, rows[c], :], vc_ref[0, 0, j * DF_CK:(j + 1) * DF_CK, :])
        return carry

    @pl.when(safe)
    def _():
        def step(acc, kt, v):
            e = [jnp.exp(_dot(qs[c], kt(c))).astype(BF16) for c in range(2)]
            return acc + _dot(jnp.concatenate(e, axis=0), v)

        acc = over_chunks(step, jnp.zeros((2 * tq, DF_VA), F32))
        finish(acc[0:tq], acc[tq:])

    @pl.when(jnp.logical_not(safe))
    def _():
        def step(carry, kt, v):
            new = []
            for c in range(2):
                m, acc = carry[c]
                s = _dot(qs[c], kt(c))
                m_new = jnp.maximum(m, jnp.max(s, axis=-1, keepdims=True))
                new.append((m_new, jnp.exp(m - m_new) * acc + _dot(jnp.exp(s - m_new).astype(BF16), v)))
            return tuple(new)

        init = (jnp.full((tq, 1), NEG_INF, F32), jnp.zeros((tq, DF_VA), F32))
        (_, a1), (_, a2) = over_chunks(step, (init, init))
        finish(a1, a2)


def _lat_df(q, kt_lat, v_lat, kt_ctx, v_ctx, df_lambda, df_subln, q_gain, k_gain, lam_init, tq):
    b, _, t, _ = q.shape
    const2 = lambda i, h, j: (0, 0)
    assert kt_lat.shape[1] % DF_UNROLL == 0
    kt_spec = lambda kt: pl.BlockSpec((1, kt.shape[1], 2 * DF_HD, DF_CK), lambda i, h, j: (i, 0, h, 0))
    v_spec = lambda v: pl.BlockSpec((1, 1, v.shape[2], DF_VA), lambda i, h, j: (i, h, 0, 0))
    return pl.pallas_call(
        functools.partial(_lat_df_kernel, lam_init),
        out_shape=jax.ShapeDtypeStruct((b, DF_HEADS, t, DF_VD), BF16),
        grid=(b, DF_HEADS, t // tq),
        in_specs=[pl.BlockSpec((1, 1, tq, 2 * DF_HD), lambda i, h, j: (i, h, j, 0)),
                  kt_spec(kt_lat), v_spec(v_lat), kt_spec(kt_ctx), v_spec(v_ctx),
                  pl.BlockSpec((4, DF_HD), const2), pl.BlockSpec((1, DF_VD), const2),
                  pl.BlockSpec((1, DF_HD), const2), pl.BlockSpec((1, DF_HD), const2)],
        out_specs=pl.BlockSpec((1, 1, tq, DF_VD), lambda i, h, j: (i, h, j, 0)),
        scratch_shapes=[pltpu.SMEM((1,), jnp.int32)],
        compiler_params=_cparams("arbitrary", "arbitrary", "arbitrary"),
        name="lat_df",
    )(q, kt_lat, v_lat, kt_ctx, v_ctx, df_lambda, df_subln.reshape(1, DF_VD),
      q_gain.reshape(1, DF_HD), k_gain.reshape(1, DF_HD))


def _with_ones_column(v, width):
    pad = width - v.shape[-1] - 1
    return jnp.concatenate([v, jnp.ones(v.shape[:-1] + (1,), v.dtype), jnp.zeros(v.shape[:-1] + (pad,), v.dtype)],
                           axis=-1)


def _out_kernel(ya_ref, yb_ref, yc_ref, x_ref, ga_ref, w_ref, o_ref):
    if len(yc_ref.shape) == 4:
        yc = jnp.concatenate([yc_ref[0, h] for h in range(DF_HEADS)], axis=-1)
    else:
        yc = yc_ref[0]
    y = (_dot(ya_ref[0], w_ref[0:LRU_W, :]) + _dot(yb_ref[0], w_ref[LRU_W:LRU_W + NA_W, :])
         + _dot(yc, w_ref[LRU_W + NA_W:, :]))
    o_ref[0] = x_ref[0] + ga_ref[0] * y


def _out_proj(latent, ya, yb, yc, x, g_a, w_out, tm):
    b, t, _ = x.shape
    _, ya_spec = _seq_split(latent, b, t, tm, BF16)
    ya = ya.reshape(ya.shape[0], ya.shape[1] // SUBLANES, SUBLANES * LRU_W)
    mod_map = (lambda i, j: (i, 0, 0)) if g_a.shape[0] > 1 else (lambda i, j: (0, 0, 0))
    tile = lambda i, j: (i, j, 0)
    if yc.ndim == 4:
        yc_spec = pl.BlockSpec((1, DF_HEADS, tm, DF_VD), lambda i, j: (i, 0, j, 0))
    else:
        yc_spec = pl.BlockSpec((1, tm, DF_V_W), tile)
    return pl.pallas_call(
        _out_kernel,
        out_shape=jax.ShapeDtypeStruct(x.shape, F32),
        grid=(b, t // tm),
        in_specs=[ya_spec, pl.BlockSpec((1, tm, NA_W), tile),
                  yc_spec, pl.BlockSpec((1, tm, D_MODEL), tile),
                  pl.BlockSpec((1, 1, D_MODEL), mod_map),
                  pl.BlockSpec((D_CAT, D_MODEL), lambda i, j: (0, 0))],
        out_specs=pl.BlockSpec((1, tm, D_MODEL), tile),
        compiler_params=_cparams("arbitrary", "arbitrary"),
        name="out_proj",
    )(ya, yb, yc, x, g_a, w_out)


def _ffn_kernel(nt, xp_ref, x_ref, xn_ref, sh_ref, sc_ref, gf_ref, g_ref, up_ref, cw_ref, cb_ref, dn_ref,
                o_ref, acc_ref):
    j = pl.program_id(1)
    tm = x_ref.shape[1]
    x = x_ref[0]
    xw = jnp.concatenate([xp_ref[0], x, xn_ref[0]], axis=0)
    rows = xw.shape[0]
    ms = jnp.mean(xw * xw, axis=-1, keepdims=True)
    y = xw * lax.rsqrt(ms + EPS) * g_ref[...]
    h = y * (1.0 + sc_ref[0]) + sh_ref[0]
    row = lax.broadcasted_iota(jnp.int32, (rows, 1), 0)
    valid = ((row >= HALO) | (j > 0)) & ((row < HALO + tm) | (j < nt - 1))
    h = jnp.where(valid, h, 0.0).astype(BF16)

    n_chunks = D_FF // FFN_CF

    def up_proj(c):
        return [_dot(h, up_ref[:, off:off + FFN_CF]) for off in (c * FFN_CF, D_FF + c * FFN_CF)]

    def conv(u, off):
        taps = (pltpu.roll(u, 1, 0), u, pltpu.roll(u, rows - 1, 0))
        out = cb_ref[:, off:off + FFN_CF]
        for k in range(3):
            out = out + cw_ref[k:k + 1, off:off + FFN_CF] * taps[k][HALO:HALO + tm, :]
        return out

    u_cur = up_proj(0)
    for c in range(n_chunks):
        u_next = up_proj(c + 1) if c + 1 < n_chunks else None
        act = (_gelu(conv(u_cur[1], D_FF + c * FFN_CF)) * conv(u_cur[0], c * FFN_CF)).astype(BF16)
        down = _dot(act, dn_ref[c * FFN_CF:(c + 1) * FFN_CF, :])
        if c == 0:
            acc_ref[...] = down
        else:
            acc_ref[...] += down
        u_cur = u_next
    o_ref[0] = x + gf_ref[0] * acc_ref[...]


def _ffn(x, shift, scale, g_f, gain, up, conv_w, conv_b, down, tm):
    b, t, _ = x.shape
    nt = t // tm
    hb = tm // HALO
    mod_map = (lambda i, j: (i, 0, 0)) if shift.shape[0] > 1 else (lambda i, j: (0, 0, 0))
    const2 = lambda i, j: (0, 0)
    mod_spec = pl.BlockSpec((1, 1, D_MODEL), mod_map)
    return pl.pallas_call(
        functools.partial(_ffn_kernel, nt),
        out_shape=jax.ShapeDtypeStruct(x.shape, F32),
        grid=(b, nt),
        in_specs=[pl.BlockSpec((1, HALO, D_MODEL), lambda i, j: (i, jnp.maximum(j * hb - 1, 0), 0)),
                  pl.BlockSpec((1, tm, D_MODEL), lambda i, j: (i, j, 0)),
                  pl.BlockSpec((1, HALO, D_MODEL), lambda i, j: (i, jnp.minimum((j + 1) * hb, t // HALO - 1), 0)),
                  mod_spec, mod_spec, mod_spec,
                  pl.BlockSpec((1, D_MODEL), const2),
                  pl.BlockSpec((D_MODEL, 2 * D_FF), const2),
                  pl.BlockSpec((3, 2 * D_FF), const2),
                  pl.BlockSpec((1, 2 * D_FF), const2),
                  pl.BlockSpec((D_FF, D_MODEL), const2)],
        out_specs=pl.BlockSpec((1, tm, D_MODEL), lambda i, j: (i, j, 0)),
        scratch_shapes=[pltpu.VMEM((tm, D_MODEL), F32)],
        compiler_params=_cparams("arbitrary", "arbitrary"),
        name="ffn",
    )(x, x, x, shift, scale, g_f, gain.reshape(1, D_MODEL), up, conv_w, conv_b.reshape(1, 2 * D_FF), down)


def _split_mod(mod):
    return [mod[:, None, i * D_MODEL:(i + 1) * D_MODEL] for i in range(N_MOD)]


def kernel(x_prompt, x_sample, state_lru, cache_na_k, cache_na_v, cache_df_k, cache_df_v, c, c_ctx, ada_w, ada_b, norm_mix, norm_ffn, w_in, lru_conv_w, lru_conv_b, lru_wa, lru_ba, lru_wx, lru_bx, lru_lambda, na_q_norm, na_k_norm, na_rpb, df_q_norm, df_k_norm, df_lambda, df_subln, w_out, ffn_up, ffn_conv_w, ffn_conv_b, ffn_down):
    bp, tp, _ = x_prompt.shape
    bs, ts, _ = x_sample.shape
    past = cache_na_k.shape[3]
    rows_n = ts // GRID_W

    c_all = jnp.zeros((SUBLANES, D_MODEL), F32).at[0].set(c_ctx).at[1:1 + bs].set(c)
    mods = _modulation(c_all, ada_w, ada_b)
    rope_tabs = _rope_tables(ts)
    zeros_state = jnp.zeros((bp // SUBLANES, 2, SUBLANES, LRU_W), F32)

    y_p, y_s = x_prompt, x_sample
    s_lru = []
    hcache = (bp, DEPTH, NA_HEADS, tp, NA_HD)
    caches = [jnp.zeros(hcache, F32), jnp.zeros(hcache, F32),
              jnp.zeros((bp, DEPTH, DF_HEADS, 2, tp, DF_HD), F32), jnp.zeros(hcache, F32)]
    for l in range(DEPTH):
        lam_init = 0.8 - 0.6 * math.exp(-0.3 * l)
        w_in_l = w_in[l].astype(BF16)
        w_out_l = w_out[l].astype(BF16)
        up_l = ffn_up[l].astype(BF16)
        down_l = ffn_down[l].astype(BF16)
        lru_args = (lru_conv_w[l], lru_conv_b[l], lru_wa[l], lru_wx[l], lru_ba[l], lru_bx[l], lru_lambda[l])
        norm_args = (na_q_norm[l], na_k_norm[l], df_q_norm[l], df_k_norm[l])

        sh_a, sc_a, g_a, sh_f, sc_f, g_f = _split_mod(mods[l, 0:1])
        lx, lg, nq, *caches = _prep(False, y_p, sh_a, sc_a, norm_mix[l], w_in_l, *norm_args, None, tp, l, caches)
        dq = caches.pop(2)
        seq4 = lambda v: v.reshape(v.shape[0], v.shape[1], SUBLANES, LRU_W)
        ya, st = _lru(False, seq4(lx), seq4(lg), *lru_args, zeros_state, 256, LRU_RC)
        yb, yc = _ctx_attn(nq, caches[0], caches[1], dq, caches[2], caches[3], df_lambda[l], df_subln[l], norm_args,
                           lam_init, l)
        x1 = _out_proj(False, ya, yb, yc, y_p, g_a, w_out_l, tp)
        y_p = _ffn(x1, sh_f, sc_f, g_f, norm_ffn[l], up_l, ffn_conv_w[l], ffn_conv_b[l], down_l, FFN_TM)
        s_lru.append(st.transpose(0, 2, 1, 3).reshape(bp, 2, LRU_W))

        sh_a, sc_a, g_a, sh_f, sc_f, g_f = _split_mod(mods[l, 1:1 + bs])
        lx, lg, nq, nk, nv, dq, dk, dv = _prep(True, y_s, sh_a, sc_a, norm_mix[l], w_in_l, *norm_args, rope_tabs,
                                               ts // SUBLANES)
        h0 = jnp.broadcast_to(state_lru[:, l, :, None, :], (bs, 2, SUBLANES, LRU_W))
        ya, _ = _lru(True, seq4(lx), seq4(lg), *lru_args, h0, 128, LRU_RC)
        yb = _lat_na(nq, nk, nv, cache_na_k[:, l].astype(BF16), _with_ones_column(cache_na_v[:, l].astype(BF16), NA_VA),
                     _na_bias_tables(na_rpb[l], rows_n), na_q_norm[l], na_k_norm[l], na_rpb[l])
        kt_ctx = jnp.swapaxes(cache_df_k[:, l].astype(BF16), -1, -2).reshape(bs, DF_QK_W, past // DF_CK, DF_CK)
        kt_ctx = kt_ctx.transpose(0, 2, 1, 3)
        v_ctx = _with_ones_column(cache_df_v[:, l].astype(BF16), DF_VA)
        yc = _lat_df(dq, dk, dv, kt_ctx, v_ctx, df_lambda[l], df_subln[l], df_q_norm[l], df_k_norm[l], lam_init, DF_TQ)
        x1 = _out_proj(True, ya, yb, yc, y_s, g_a, w_out_l, ts // SUBLANES)
        y_s = _ffn(x1, sh_f, sc_f, g_f, norm_ffn[l], up_l, ffn_conv_w[l], ffn_conv_b[l], down_l, FFN_TM)

    return (y_p, y_s, jnp.stack(s_lru, axis=1), *caches)
```

```python
import functools
import math

import numpy as np
import jax
import jax.numpy as jnp
from jax import lax
from jax.experimental import pallas as pl
from jax.experimental.pallas import tpu as pltpu

F32 = jnp.float32
BF16 = jnp.bfloat16

D_MODEL = 1024
DEPTH = 2
GRID_W = 64
LRU_W = 512
LRU_BLOCKS = 8
LRU_BW = LRU_W // LRU_BLOCKS
LRU_CONV = 4
LRU_C = 8.0
NA_HEADS = 4
NA_HD = 64
NA_W = NA_HEADS * NA_HD
NA_ROWS = 8
NA_COLS = 16
DF_HEADS = 4
DF_HD = 32
DF_VD = 2 * DF_HD
DF_QK_W = DF_HEADS * 2 * DF_HD
DF_V_W = DF_HEADS * DF_VD
D_IN = 2 * LRU_W + 3 * NA_W + 2 * DF_QK_W + DF_V_W
D_CAT = LRU_W + NA_W + DF_V_W
D_FF = 2816
ROPE_BASE = 10000.0
EPS = 1e-6
NEG_INF = -1e30
N_MOD = 6

OFF_LX = 0
OFF_LG = OFF_LX + LRU_W
OFF_NQ = OFF_LG + LRU_W
OFF_NK = OFF_NQ + NA_W
OFF_NV = OFF_NK + NA_W
OFF_DQ = OFF_NV + NA_W
OFF_DK = OFF_DQ + DF_QK_W
OFF_DV = OFF_DK + DF_QK_W

SUBLANES = 8
VMEM_LIMIT = 56 * 1024 * 1024

NA_QROWS = 4
NA_KROWS = NA_QROWS + NA_ROWS
DF_CK = 512
DF_TQ = 512
DF_VA = 128
NA_VA = DF_VA
DF_SAFE_SCORE = 50.0
DF_UNROLL = 8
LRU_RC = 32
LRU_UNROLL = 8
FFN_CF = 256
FFN_TM = 256
HALO = SUBLANES


def _cparams(*sem):
    return pltpu.CompilerParams(dimension_semantics=sem, vmem_limit_bytes=VMEM_LIMIT)


def _gelu(x):
    return 0.5 * x * (1.0 + jnp.tanh(math.sqrt(2.0 / math.pi) * (x + 0.044715 * (x * x * x))))


def _sigmoid(x):
    return 1.0 / (1.0 + jnp.exp(-x))


def _neg_expm1(x, u):
    safe = u != 1.0
    return jnp.where(safe, (1.0 - u) * x / jnp.log(jnp.where(safe, u, 0.5)), -x)


def _dot(a, b):
    return jnp.dot(a, b, preferred_element_type=F32)


def _dot_nt(a, b):
    return lax.dot_general(a, b, (((1,), (1,)), ((), ())), preferred_element_type=F32)


def _mod_kernel(c_ref, w_ref, b_ref, o_ref):
    c = c_ref[...]
    s = (c * _sigmoid(c)).astype(BF16)
    o_ref[0] = _dot(s, w_ref[0].astype(BF16)) + b_ref[0]


def _modulation(c_all, ada_w, ada_b):
    nb = 1536
    n = N_MOD * D_MODEL
    return pl.pallas_call(
        _mod_kernel,
        out_shape=jax.ShapeDtypeStruct((DEPTH, SUBLANES, n), F32),
        grid=(DEPTH, n // nb),
        in_specs=[pl.BlockSpec((SUBLANES, D_MODEL), lambda l, j: (0, 0)),
                  pl.BlockSpec((1, D_MODEL, nb), lambda l, j: (l, 0, j)),
                  pl.BlockSpec((1, 1, nb), lambda l, j: (l, 0, j))],
        out_specs=pl.BlockSpec((1, SUBLANES, nb), lambda l, j: (l, 0, j)),
        compiler_params=_cparams("arbitrary", "arbitrary"),
        name="modulation",
    )(c_all, ada_w, ada_b.reshape(DEPTH, 1, n))


def _group_rms(z, g_ref, gain):
    ms = _dot((z * z).astype(BF16), g_ref[...])
    return z * lax.rsqrt(ms + EPS) * gain


def _rope(x, cos, sin):
    w = x.shape[-1]
    lane = lax.broadcasted_iota(jnp.int32, x.shape, 1)
    partner = jnp.where((lane % 16) < 8, pltpu.roll(x, w - 8, 1), pltpu.roll(x, 8, 1))
    return x * cos + partner * sin


def _prep_kernel(latent, x_ref, sh_ref, sc_ref, g_ref, w_ref, qn_ref, kn_ref, dqn_ref, dkn_ref,
                 g64_ref, g32_ref, *rest):
    if latent:
        cos_ref, sin_ref, lx_ref, lg_ref, nq_ref, nk_ref, nv_ref, dq_ref, dk_ref, dv_ref = rest
    else:
        lx_ref, lg_ref, nq_ref, nk_ref, nv_ref, dq_ref, dk_ref, dv_ref = rest[-8:]
        nk_ref, nv_ref, dk_ref, dv_ref = nk_ref.at[0], nv_ref.at[0], dk_ref.at[0], dv_ref.at[0]
    x = x_ref[0]
    ms = jnp.mean(x * x, axis=-1, keepdims=True)
    y = x * lax.rsqrt(ms + EPS) * g_ref[...]
    h = (y * (1.0 + sc_ref[0]) + sh_ref[0]).astype(BF16)

    def seg(off, size):
        return _dot(h, w_ref[:, off:off + size])

    lx_ref[0] = seg(OFF_LX, LRU_W)
    lg_ref[0] = seg(OFF_LG, LRU_W)

    nq = _group_rms(seg(OFF_NQ, NA_W), g64_ref, qn_ref[...]) * (NA_HD ** -0.5)
    nk = _group_rms(seg(OFF_NK, NA_W), g64_ref, kn_ref[...])
    nv = seg(OFF_NV, NA_W)
    dq = _group_rms(seg(OFF_DQ, DF_QK_W), g32_ref, dqn_ref[...])
    dk = _group_rms(seg(OFF_DK, DF_QK_W), g32_ref, dkn_ref[...])
    dv = seg(OFF_DV, DF_V_W)
    if latent:
        cos = cos_ref[...]
        sin = sin_ref[...]
        dq = _rope(dq, cos, sin)
        dk = _rope(dk, cos, sin)
    dq = dq * (DF_HD ** -0.5)

    for hh in range(NA_HEADS):
        sl = slice(hh * NA_HD, (hh + 1) * NA_HD)
        nq_ref[0, hh] = nq[:, sl].astype(nq_ref.dtype)
        nk_ref[0, hh] = nk[:, sl].astype(nk_ref.dtype)
        if latent:
            dq_ref[0, hh] = dq[:, sl].astype(dq_ref.dtype)
            lane = lax.broadcasted_iota(jnp.int32, (dv.shape[0], DF_VA - DF_VD), 1)
            ones_col = jnp.where(lane == 0, 1.0, 0.0).astype(dv_ref.dtype)
            for val, ref in ((nv, nv_ref), (dv, dv_ref)):
                ref[0, hh, :, 0:DF_VD] = val[:, sl].astype(ref.dtype)
                ref[0, hh, :, DF_VD:] = ones_col
        else:
            nv_ref[0, hh] = nv[:, sl].astype(nv_ref.dtype)
            dv_ref[0, hh] = dv[:, sl].astype(dv_ref.dtype)
    if latent:
        dk_ref[0, 0] = dk.T.astype(dk_ref.dtype)
    else:
        for hc in range(2 * DF_HEADS):
            sl = slice(hc * DF_HD, (hc + 1) * DF_HD)
            dq_ref[0, hc // 2, hc % 2] = dq[:, sl].astype(dq_ref.dtype)
            dk_ref[0, hc // 2, hc % 2] = dk[:, sl].astype(dk_ref.dtype)


def _seq_split(latent, b, t, tm, dtype):
    if latent:
        assert tm * SUBLANES == t
        return (jax.ShapeDtypeStruct((b, tm, SUBLANES * LRU_W), dtype),
                pl.BlockSpec((1, tm, LRU_W), lambda i, j: (i, 0, j)))
    assert tm == t and b % SUBLANES == 0
    return (jax.ShapeDtypeStruct((b // SUBLANES, t, SUBLANES * LRU_W), dtype),
            pl.BlockSpec((1, tm, LRU_W), lambda i, j: (i // SUBLANES, 0, i % SUBLANES)))


def _group_matrix(width, group):
    idx = np.arange(width) // group
    return jnp.asarray((idx[:, None] == idx[None, :]).astype(np.float32) / group, dtype=BF16)


def _prep(latent, x, shift, scale, gain, w_in, qn, kn, dqn, dkn, rope_tabs, tm, layer=0, caches=None):
    b, t, _ = x.shape
    aliases = {}
    bm = shift.shape[0]
    mod_map = (lambda i, j: (i, 0, 0)) if bm > 1 else (lambda i, j: (0, 0, 0))
    const2 = lambda i, j: (0, 0)
    tile3 = lambda i, j: (i, j, 0)
    head4 = lambda i, j: (i, 0, j, 0)
    in_specs = [pl.BlockSpec((1, tm, D_MODEL), tile3),
                pl.BlockSpec((1, 1, D_MODEL), mod_map),
                pl.BlockSpec((1, 1, D_MODEL), mod_map),
                pl.BlockSpec((1, D_MODEL), const2),
                pl.BlockSpec((D_MODEL, D_IN), const2),
                pl.BlockSpec((1, NA_W), const2),
                pl.BlockSpec((1, NA_W), const2),
                pl.BlockSpec((1, DF_QK_W), const2),
                pl.BlockSpec((1, DF_QK_W), const2),
                pl.BlockSpec((NA_W, NA_W), const2),
                pl.BlockSpec((DF_QK_W, DF_QK_W), const2)]
    args = [x, shift, scale, gain.reshape(1, D_MODEL), w_in,
            jnp.tile(qn, NA_HEADS).reshape(1, NA_W), jnp.tile(kn, NA_HEADS).reshape(1, NA_W),
            jnp.tile(dqn, 2 * DF_HEADS).reshape(1, DF_QK_W), jnp.tile(dkn, 2 * DF_HEADS).reshape(1, DF_QK_W),
            _group_matrix(NA_W, NA_HD), _group_matrix(DF_QK_W, DF_HD)]
    head_shape = lambda dt: jax.ShapeDtypeStruct((b, NA_HEADS, t, NA_HD), dt)
    head_spec = pl.BlockSpec((1, NA_HEADS, tm, NA_HD), head4)
    seq_shape, seq_spec = _seq_split(latent, b, t, tm, F32)
    out_shape = [seq_shape, seq_shape]
    out_specs = [seq_spec, seq_spec]
    if latent:
        in_specs += [pl.BlockSpec((tm, DF_QK_W), lambda i, j: (j, 0))] * 2
        args += list(rope_tabs)
        aug_shape = jax.ShapeDtypeStruct((b, DF_HEADS, t, DF_VA), BF16)
        aug_spec = pl.BlockSpec((1, DF_HEADS, tm, DF_VA), head4)
        out_shape += [head_shape(BF16), head_shape(BF16), aug_shape,
                      head_shape(BF16), jax.ShapeDtypeStruct((b, t // tm, DF_QK_W, tm), BF16), aug_shape]
        out_specs += [head_spec, head_spec, aug_spec,
                      head_spec, pl.BlockSpec((1, 1, DF_QK_W, tm), lambda i, j: (i, j, 0, 0)), aug_spec]
    else:
        comp_shape = jax.ShapeDtypeStruct((b, DF_HEADS, 2, t, DF_HD), BF16)
        comp_spec = pl.BlockSpec((1, DF_HEADS, 2, tm, DF_HD), lambda i, j: (i, 0, 0, j, 0))
        hcache_shape = jax.ShapeDtypeStruct((b, DEPTH, NA_HEADS, t, NA_HD), F32)
        hcache_spec = pl.BlockSpec((1, 1, NA_HEADS, tm, NA_HD), lambda i, j: (i, layer, 0, j, 0))
        ccache_shape = jax.ShapeDtypeStruct((b, DEPTH, DF_HEADS, 2, t, DF_HD), F32)
        ccache_spec = pl.BlockSpec((1, 1, DF_HEADS, 2, tm, DF_HD), lambda i, j: (i, layer, 0, 0, j, 0))
        out_shape += [head_shape(BF16), hcache_shape, hcache_shape, comp_shape, ccache_shape, hcache_shape]
        out_specs += [head_spec, hcache_spec, hcache_spec, comp_spec, ccache_spec, hcache_spec]
        aliases = {len(args) + k: out_idx for k, out_idx in enumerate((3, 4, 6, 7))}
        in_specs += [pl.BlockSpec(memory_space=pl.ANY)] * 4
        args += list(caches)
    return pl.pallas_call(
        functools.partial(_prep_kernel, latent),
        out_shape=out_shape,
        grid=(b, t // tm),
        in_specs=in_specs,
        out_specs=out_specs,
        input_output_aliases=aliases,
        compiler_params=_cparams("arbitrary", "arbitrary"),
        name="prep_latent" if latent else "prep_context",
    )(*args)


def _rope_tables(t):
    half = DF_HD // 4
    freqs = ROPE_BASE ** (-jnp.arange(half, dtype=F32) / half)
    tok = jnp.arange(t)
    lane = np.arange(DF_QK_W)
    d = lane % DF_HD
    pos = jnp.where((d < DF_HD // 2)[None, :], (tok // GRID_W)[:, None], (tok % GRID_W)[:, None])
    ang = pos.astype(F32) * freqs[d % half][None, :]
    sign = np.where((d % (2 * half)) < half, -1.0, 1.0).astype(np.float32)
    return jnp.cos(ang), jnp.sin(ang) * sign[None, :]


def _lru_kernel(chained, jn, rc, lx_ref, lg_ref, cw_ref, cb_ref, wg_ref, gb_ref, lam_ref, h0_ref,
                ya_ref, st_ref, xpad_ref, af_ref, bf_ref, ab_ref, bb_ref, hf_ref):
    ct = lx_ref.shape[-1]
    row = lax.broadcasted_iota(jnp.int32, (SUBLANES, ct), 0)
    zero = jnp.zeros((SUBLANES, ct), F32)

    def from_prev_seq(v):
        return jnp.where(row >= 1, pltpu.roll(v, 1, 0), 0.0) if chained else zero

    def from_next_seq(v):
        return jnp.where(row < SUBLANES - 1, pltpu.roll(v, SUBLANES - 1, 0), 0.0) if chained else zero

    xpad_ref[0] = from_prev_seq(lx_ref[0, jn - 1])
    xpad_ref[1:jn + 1] = lx_ref[0]
    xpad_ref[jn + 1] = from_next_seq(lx_ref[0, 0])
    xpad_ref[jn + 2] = from_next_seq(lx_ref[0, 1])

    lam = -lam_ref[...]
    sp = jnp.maximum(lam, 0.0) + jnp.log1p(jnp.exp(-jnp.abs(lam)))
    cw = cw_ref[...]
    cb = cb_ref[...]
    wg = wg_ref[0]
    gb = gb_ref[0]

    def gates(i, carry):
        r0 = pl.multiple_of(i * rc, rc)
        xw = xpad_ref[pl.ds(r0, rc + LRU_CONV - 1)]
        xc = cb
        for j in range(LRU_CONV):
            xc = xc + cw[j:j + 1, :] * xw[j:j + rc]
        xc = xc.reshape(rc * SUBLANES, ct)
        g = _dot(xc.astype(BF16), wg) + gb
        for d, (a_ref, b_ref) in enumerate(((af_ref, bf_ref), (ab_ref, bb_ref))):
            r = _sigmoid(g[:, (2 * d) * ct:(2 * d + 1) * ct])
            ig = _sigmoid(g[:, (2 * d + 1) * ct:(2 * d + 2) * ct])
            log_a = (-LRU_C) * r * sp[d:d + 1, :]
            a = jnp.exp(log_a)
            b = jnp.sqrt(_neg_expm1(2.0 * log_a, a * a)) * (ig * xc)
            a_ref[pl.ds(r0, rc)] = a.reshape(rc, SUBLANES, ct)
            b_ref[pl.ds(r0, rc)] = b.reshape(rc, SUBLANES, ct)
        return carry

    lax.fori_loop(0, jn // rc, gates, 0)

    hf0 = h0_ref[0, 0]
    hb0 = h0_ref[0, 1]
    if chained:
        def summarise(j, carry):
            hf, pf, hb, pb = carry
            a = af_ref[j]
            jb = jn - 1 - j
            a2 = ab_ref[jb]
            return a * hf + bf_ref[j], a * pf, a2 * hb + bb_ref[jb], a2 * pb

        one = jnp.ones((SUBLANES, ct), F32)
        hf_end, pf_end, hb_end, pb_end = lax.fori_loop(0, jn, summarise, (zero, one, zero, one), unroll=LRU_UNROLL)
        for s in range(SUBLANES - 1):
            hf0 = jnp.where(row == s + 1, pltpu.roll(hf_end + pf_end * hf0, 1, 0), hf0)
            sb = SUBLANES - 2 - s
            hb0 = jnp.where(row == sb, pltpu.roll(hb_end + pb_end * hb0, SUBLANES - 1, 0), hb0)

    def fwd(j, h):
        h = af_ref[j] * h + bf_ref[j]
        hf_ref[j] = h
        return h

    st_ref[0, 0] = lax.fori_loop(0, jn, fwd, hf0, unroll=LRU_UNROLL)

    def bwd(j, h):
        jb = jn - 1 - j
        h = ab_ref[jb] * h + bb_ref[jb]
        hf_ref[jb] = hf_ref[jb] + h
        return h

    st_ref[0, 1] = lax.fori_loop(0, jn, bwd, hb0, unroll=LRU_UNROLL)

    def emit(i, carry):
        r0 = pl.multiple_of(i * rc, rc)
        y = (hf_ref[pl.ds(r0, rc)] * _gelu(lg_ref[0, pl.ds(r0, rc)])).reshape(rc * SUBLANES, ct)
        ya_ref[0, pl.ds(pl.multiple_of(r0 * SUBLANES, rc * SUBLANES), rc * SUBLANES), :] = y.astype(ya_ref.dtype)
        return carry

    lax.fori_loop(0, jn // rc, emit, 0)


def _lru_gate_weights(wa, wx, ba, bx, ct):
    nct = LRU_W // ct
    per = ct // LRU_BW
    eye = jnp.eye(per, dtype=F32)

    def dense(w):
        wt = w.reshape(nct, per, LRU_BW, LRU_BW)
        return jnp.einsum('cpij,pq->cpiqj', wt, eye).reshape(nct, ct, ct)

    mats = [dense(wa[0]), dense(wx[0]), dense(wa[1]), dense(wx[1])]
    bias = [ba[0], bx[0], ba[1], bx[1]]
    wg = jnp.concatenate(mats, axis=-1).astype(BF16)
    gb = jnp.concatenate([v.reshape(nct, 1, ct) for v in bias], axis=-1)
    return wg, gb


def _lru(chained, lx, lg, conv_w, conv_b, wa, wx, ba, bx, lam, h0, ct, rc):
    g, jn, _, _ = lx.shape
    nct = LRU_W // ct
    wg, gb = _lru_gate_weights(wa, wx, ba, bx, ct)
    tile = lambda i, j: (i, 0, 0, j)
    seq = (jn, SUBLANES, ct)
    return pl.pallas_call(
        functools.partial(_lru_kernel, chained, jn, rc),
        out_shape=[jax.ShapeDtypeStruct((g, jn * SUBLANES, LRU_W), BF16),
                   jax.ShapeDtypeStruct((g, 2, SUBLANES, LRU_W), F32)],
        grid=(g, nct),
        in_specs=[pl.BlockSpec((1,) + seq, tile),
                  pl.BlockSpec((1,) + seq, tile),
                  pl.BlockSpec((LRU_CONV, ct), lambda i, j: (0, j)),
                  pl.BlockSpec((1, ct), lambda i, j: (0, j)),
                  pl.BlockSpec((1, ct, 4 * ct), lambda i, j: (j, 0, 0)),
                  pl.BlockSpec((1, 1, 4 * ct), lambda i, j: (j, 0, 0)),
                  pl.BlockSpec((2, ct), lambda i, j: (0, j)),
                  pl.BlockSpec((1, 2, SUBLANES, ct), tile)],
        out_specs=[pl.BlockSpec((1, jn * SUBLANES, ct), lambda i, j: (i, 0, j)),
                   pl.BlockSpec((1, 2, SUBLANES, ct), tile)],
        scratch_shapes=[pltpu.VMEM((jn + LRU_CONV - 1, SUBLANES, ct), F32)] + [pltpu.VMEM(seq, F32)] * 5,
        compiler_params=_cparams("arbitrary", "arbitrary"),
        name="lru_latent" if chained else "lru_context",
    )(lx, lg, conv_w, conv_b.reshape(1, LRU_W), wg, gb, lam, h0)


def _df_lambda(lv_ref, lam_init):
    lv = lv_ref[...]
    l1 = jnp.sum(lv[0:1, :] * lv[1:2, :], axis=-1, keepdims=True)
    l2 = jnp.sum(lv[2:3, :] * lv[3:4, :], axis=-1, keepdims=True)
    return jnp.exp(l1) - jnp.exp(l2) + lam_init


def _diff_combine(o1, o2, lam, subln, lam_init):
    y = o1 - lam * o2
    ms = jnp.mean(y * y, axis=-1, keepdims=True)
    return y * lax.rsqrt(ms + EPS) * subln * (1.0 - lam_init)


def _ctx_attn_kernel(lam_init, nq_ref, nk_ref, nv_ref, dq_ref, dk_ref, dv_ref, lv_ref, sub_ref,
                     nqg_ref, nkg_ref, dqg_ref, dkg_ref, yb_ref, yc_ref, va_ref, safe_ref):
    nk_ref, nv_ref, dk_ref, dv_ref = nk_ref.at[0], nv_ref.at[0], dk_ref.at[0], dv_ref.at[0]
    t = nq_ref.shape[2]
    lam = _df_lambda(lv_ref, lam_init)
    sub = sub_ref[...]

    @pl.when(pl.program_id(0) == 0)
    def _():
        gmax = lambda ref: jnp.max(jnp.abs(ref[...]))
        bound = jnp.maximum(gmax(nqg_ref) * gmax(nkg_ref) * NA_HD ** 0.5, gmax(dqg_ref) * gmax(dkg_ref) * DF_HD ** 0.5)
        safe_ref[0] = (bound <= DF_SAFE_SCORE).astype(jnp.int32)
        lane = lax.broadcasted_iota(jnp.int32, (t, NA_VA - NA_HD), 1)
        va_ref[:, NA_HD:] = jnp.where(lane == 0, 1.0, 0.0).astype(va_ref.dtype)

    safe = safe_ref[0] == 1

    def heads(fast):
        def attend(q, k):
            s = _dot_nt(q, k.astype(BF16))
            if not fast:
                s = s - jnp.max(s, axis=-1, keepdims=True)
            o = _dot(jnp.exp(s).astype(BF16), va_ref[...])
            return o[:, 0:NA_HD] / o[:, NA_HD:NA_HD + 1]

        for h in range(NA_HEADS):
            va_ref[:, 0:NA_HD] = nv_ref[0, h].astype(va_ref.dtype)
            yb_ref[0, :, h * NA_HD:(h + 1) * NA_HD] = attend(nq_ref[0, h], nk_ref[0, h]).astype(yb_ref.dtype)
        for h in range(DF_HEADS):
            va_ref[:, 0:DF_VD] = dv_ref[0, h].astype(va_ref.dtype)
            o1 = attend(dq_ref[0, h, 0], dk_ref[0, h, 0])
            o2 = attend(dq_ref[0, h, 1], dk_ref[0, h, 1])
            yc_ref[0, :, h * DF_VD:(h + 1) * DF_VD] = _diff_combine(o1, o2, lam, sub, lam_init).astype(yc_ref.dtype)

    @pl.when(safe)
    def _():
        heads(True)

    @pl.when(jnp.logical_not(safe))
    def _():
        heads(False)


def _ctx_attn(nq, nk, nv, dq, dk, dv, df_lambda, df_subln, gains, lam_init, layer):
    b, _, t, _ = nq.shape
    head = pl.BlockSpec((1, NA_HEADS, t, NA_HD), lambda i: (i, 0, 0, 0))
    comp = pl.BlockSpec((1, DF_HEADS, 2, t, DF_HD), lambda i: (i, 0, 0, 0, 0))
    hcache = pl.BlockSpec((1, 1, NA_HEADS, t, NA_HD), lambda i: (i, layer, 0, 0, 0))
    ccache = pl.BlockSpec((1, 1, DF_HEADS, 2, t, DF_HD), lambda i: (i, layer, 0, 0, 0, 0))
    out = pl.BlockSpec((1, t, NA_W), lambda i: (i, 0, 0))
    return pl.pallas_call(
        functools.partial(_ctx_attn_kernel, lam_init),
        out_shape=[jax.ShapeDtypeStruct((b, t, NA_W), BF16), jax.ShapeDtypeStruct((b, t, DF_V_W), BF16)],
        grid=(b,),
        in_specs=[head, hcache, hcache, comp, ccache, hcache,
                  pl.BlockSpec((4, DF_HD), lambda i: (0, 0)),
                  pl.BlockSpec((1, DF_VD), lambda i: (0, 0))]
                 + [pl.BlockSpec((1, g.shape[0]), lambda i: (0, 0)) for g in gains],
        out_specs=[out, out],
        scratch_shapes=[pltpu.VMEM((t, NA_VA), BF16), pltpu.SMEM((1,), jnp.int32)],
        compiler_params=_cparams("arbitrary"),
        name="ctx_attn",
    )(nq, nk, nv, dq, dk, dv, df_lambda, df_subln.reshape(1, DF_VD), *[g.reshape(1, -1) for g in gains])


def _na_bias_tables(rpb, rows_n):
    n_dr, n_dc = 2 * NA_ROWS - 1, 2 * NA_COLS - 1
    qc = np.arange(GRID_W)[:, None]
    kc = np.arange(GRID_W)[None, :]
    cs = np.clip(qc - NA_COLS // 2, 0, GRID_W - NA_COLS)
    ok_col = (kc >= cs) & (kc < cs + NA_COLS)
    d_col = np.clip(kc - qc + NA_COLS - 1, 0, n_dc - 1)
    sel_col = (d_col[None] == np.arange(n_dc)[:, None, None]) & ok_col[None]
    sel_row = np.zeros((3, NA_QROWS, NA_KROWS, n_dr), np.float32)
    for kind, r0 in enumerate((0, NA_QROWS, rows_n - NA_QROWS)):
        kstart = int(np.clip(r0 - NA_ROWS // 2, 0, rows_n - NA_KROWS))
        r = r0 + np.arange(NA_QROWS)[:, None]
        kr = kstart + np.arange(NA_KROWS)[None, :]
        rs = np.clip(r - NA_ROWS // 2, 0, rows_n - NA_ROWS)
        ok_row = (kr >= rs) & (kr < rs + NA_ROWS)
        d_row = np.clip(kr - r + NA_ROWS - 1, 0, n_dr - 1)
        sel_row[kind] = (d_row[..., None] == np.arange(n_dr)) & ok_row[..., None]
    ok = (sel_row.sum(-1) > 0)[:, :, None, :, None] & ok_col[None, None, :, None, :]
    hi = lax.Precision.HIGHEST
    cols = jnp.einsum('had,dqk->haqk', rpb, jnp.asarray(sel_col, F32), precision=hi)
    tabs = jnp.einsum('tija,haqk->thiqjk', jnp.asarray(sel_row), cols, precision=hi)
    tabs = jnp.where(jnp.asarray(ok)[:, None], tabs, NEG_INF)
    return tabs.reshape(3, NA_HEADS, NA_QROWS * GRID_W, NA_KROWS * GRID_W)


def _lat_na_kernel(rows_n, q_ref, k_ref, v_ref, ck_ref, cv_ref, bias_ref, qg_ref, kg_ref, rpb_ref, o_ref, safe_ref):
    i = pl.program_id(1)
    kstart = jnp.clip(i * NA_QROWS - NA_ROWS // 2, 0, rows_n - NA_KROWS)
    k0 = pl.multiple_of(kstart * GRID_W, NA_QROWS * GRID_W)
    nk = NA_KROWS * GRID_W

    @pl.when(i == 0)
    def _():
        ck2 = jnp.zeros((1, 1), F32)
        for h in range(NA_HEADS):
            c = ck_ref[0, h].astype(F32)
            ck2 = jnp.maximum(ck2, jnp.max(jnp.sum(c * c, axis=-1, keepdims=True), axis=0, keepdims=True))
        k_max = jnp.maximum(jnp.max(jnp.abs(kg_ref[...])) * NA_HD ** 0.5, jnp.sqrt(jnp.max(ck2)))
        bound = jnp.max(jnp.abs(qg_ref[...])) * k_max + jnp.max(jnp.abs(rpb_ref[...]))
        safe_ref[0] = (bound <= DF_SAFE_SCORE).astype(jnp.int32)

    safe = safe_ref[0] == 1

    def heads(fast):
        for h in range(NA_HEADS):
            q = q_ref[0, h]
            s_loc = _dot_nt(q, k_ref[0, h, pl.ds(k0, nk), :]) + bias_ref[0, h]
            s_ctx = _dot_nt(q, ck_ref[0, h])
            if not fast:
                m = jnp.maximum(jnp.max(s_loc, axis=-1, keepdims=True), jnp.max(s_ctx, axis=-1, keepdims=True))
                s_loc = s_loc - m
                s_ctx = s_ctx - m
            o = (_dot(jnp.exp(s_loc).astype(BF16), v_ref[0, h, pl.ds(k0, nk), :])
                 + _dot(jnp.exp(s_ctx).astype(BF16), cv_ref[0, h]))
            o = o[:, 0:NA_HD] / o[:, NA_HD:NA_HD + 1]
            o_ref[0, :, h * NA_HD:(h + 1) * NA_HD] = o.astype(o_ref.dtype)

    @pl.when(safe)
    def _():
        heads(True)

    @pl.when(jnp.logical_not(safe))
    def _():
        heads(False)


def _lat_na(q, k, v, ck, cv, bias_tabs, q_gain, k_gain, rpb):
    b, _, t, _ = q.shape
    rows_n = t // GRID_W
    nt = rows_n // NA_QROWS
    tq = NA_QROWS * GRID_W
    past = ck.shape[2]
    full = lambda n, w: pl.BlockSpec((1, NA_HEADS, n, w), lambda i, j: (i, 0, 0, 0))
    const2 = lambda i, j: (0, 0)
    rpb2 = rpb.reshape(-1, rpb.shape[-1])

    def kind(i, j):
        return (jnp.where(j == 0, 0, jnp.where(j == nt - 1, 2, 1)), 0, 0, 0)

    return pl.pallas_call(
        functools.partial(_lat_na_kernel, rows_n),
        out_shape=jax.ShapeDtypeStruct((b, t, NA_W), BF16),
        grid=(b, nt),
        in_specs=[pl.BlockSpec((1, NA_HEADS, tq, NA_HD), lambda i, j: (i, 0, j, 0)),
                  full(t, NA_HD), full(t, NA_VA), full(past, NA_HD), full(past, NA_VA),
                  pl.BlockSpec((1, NA_HEADS, tq, NA_KROWS * GRID_W), kind),
                  pl.BlockSpec((1, NA_HD), const2), pl.BlockSpec((1, NA_HD), const2),
                  pl.BlockSpec(rpb2.shape, const2)],
        out_specs=pl.BlockSpec((1, tq, NA_W), lambda i, j: (i, j, 0)),
        scratch_shapes=[pltpu.SMEM((1,), jnp.int32)],
        compiler_params=_cparams("arbitrary", "arbitrary"),
        name="lat_na",
    )(q, k, v, ck, cv, bias_tabs, q_gain.reshape(1, NA_HD), k_gain.reshape(1, NA_HD), rpb2)


def _lat_df_kernel(lam_init, q_ref, ktl_ref, vl_ref, ktc_ref, vc_ref, lv_ref, sub_ref, qg_ref, kg_ref,
                   o_ref, safe_ref):
    tq = q_ref.shape[2]
    n_lat = ktl_ref.shape[1]
    n_ctx = ktc_ref.shape[1]
    lam = _df_lambda(lv_ref, lam_init)
    q = q_ref[0, 0]
    qs = [q[:, c * DF_HD:(c + 1) * DF_HD] for c in range(2)]
    rows = [slice(c * DF_HD, (c + 1) * DF_HD) for c in range(2)]

    @pl.when(pl.program_id(2) == 0)
    def _():
        ck2 = jnp.zeros((1, DF_CK), F32)
        for c in range(2):
            for j in range(n_ctx):
                k = ktc_ref[0, j, rows[c], :].astype(F32)
                ck2 = jnp.maximum(ck2, jnp.sum(k * k, axis=0, keepdims=True))
        k_max = jnp.maximum(jnp.max(jnp.abs(kg_ref[...])) * DF_HD ** 0.5, jnp.sqrt(jnp.max(ck2)))
        safe_ref[0] = (jnp.max(jnp.abs(qg_ref[...])) * k_max <= DF_SAFE_SCORE).astype(jnp.int32)

    safe = safe_ref[0] == 1

    def finish(a1, a2):
        o1 = a1[:, 0:DF_VD] / a1[:, DF_VD:DF_VD + 1]
        o2 = a2[:, 0:DF_VD] / a2[:, DF_VD:DF_VD + 1]
        o_ref[0, 0] = _diff_combine(o1, o2, lam, sub_ref[...], lam_init).astype(o_ref.dtype)

    def over_chunks(step, carry):
        def trip(t, carry):
            for u in range(DF_UNROLL):
                j = t * DF_UNROLL + u
                v = vl_ref[0, 0, pl.ds(pl.multiple_of(j * DF_CK, DF_CK), DF_CK), :]
                carry = step(carry, lambda c: ktl_ref[0, j, rows[c], :], v)
            return carry

        carry = lax.fori_loop(0, n_lat // DF_UNROLL, trip, carry)
        for j in range(n_ctx):
            carry = step(carry, lambda c: ktc_ref[0, j, rows[c], :], vc_ref[0, 0, j * DF_CK:(j + 1) * DF_CK, :])
        return carry

    @pl.when(safe)
    def _():
        def step(acc, kt, v):
            e = [jnp.exp(_dot(qs[c], kt(c))).astype(BF16) for c in range(2)]
            return acc + _dot(jnp.concatenate(e, axis=0), v)

        acc = over_chunks(step, jnp.zeros((2 * tq, DF_VA), F32))
        finish(acc[0:tq], acc[tq:])

    @pl.when(jnp.logical_not(safe))
    def _():
        def step(carry, kt, v):
            new = []
            for c in range(2):
                m, acc = carry[c]
                s = _dot(qs[c], kt(c))
                m_new = jnp.maximum(m, jnp.max(s, axis=-1, keepdims=True))
                new.append((m_new, jnp.exp(m - m_new) * acc + _dot(jnp.exp(s - m_new).astype(BF16), v)))
            return tuple(new)

        init = (jnp.full((tq, 1), NEG_INF, F32), jnp.zeros((tq, DF_VA), F32))
        (_, a1), (_, a2) = over_chunks(step, (init, init))
        finish(a1, a2)


def _lat_df(q, kt_lat, v_lat, kt_ctx, v_ctx, df_lambda, df_subln, q_gain, k_gain, lam_init, tq):
    b, _, t, _ = q.shape
    const2 = lambda i, h, j: (0, 0)
    assert kt_lat.shape[1] % DF_UNROLL == 0
    kt_spec = lambda kt: pl.BlockSpec((1, kt.shape[1], 2 * DF_HD, DF_CK), lambda i, h, j: (i, 0, h, 0))
    v_spec = lambda v: pl.BlockSpec((1, 1, v.shape[2], DF_VA), lambda i, h, j: (i, h, 0, 0))
    return pl.pallas_call(
        functools.partial(_lat_df_kernel, lam_init),
        out_shape=jax.ShapeDtypeStruct((b, DF_HEADS, t, DF_VD), BF16),
        grid=(b, DF_HEADS, t // tq),
        in_specs=[pl.BlockSpec((1, 1, tq, 2 * DF_HD), lambda i, h, j: (i, h, j, 0)),
                  kt_spec(kt_lat), v_spec(v_lat), kt_spec(kt_ctx), v_spec(v_ctx),
                  pl.BlockSpec((4, DF_HD), const2), pl.BlockSpec((1, DF_VD), const2),
                  pl.BlockSpec((1, DF_HD), const2), pl.BlockSpec((1, DF_HD), const2)],
        out_specs=pl.BlockSpec((1, 1, tq, DF_VD), lambda i, h, j: (i, h, j, 0)),
        scratch_shapes=[pltpu.SMEM((1,), jnp.int32)],
        compiler_params=_cparams("arbitrary", "arbitrary", "arbitrary"),
        name="lat_df",
    )(q, kt_lat, v_lat, kt_ctx, v_ctx, df_lambda, df_subln.reshape(1, DF_VD),
      q_gain.reshape(1, DF_HD), k_gain.reshape(1, DF_HD))


def _with_ones_column(v, width):
    pad = width - v.shape[-1] - 1
    return jnp.concatenate([v, jnp.ones(v.shape[:-1] + (1,), v.dtype), jnp.zeros(v.shape[:-1] + (pad,), v.dtype)],
                           axis=-1)


def _out_kernel(ya_ref, yb_ref, yc_ref, x_ref, ga_ref, w_ref, o_ref):
    o_ref[0] = _out_proj_rows(ya_ref, yb_ref, yc_ref, x_ref, ga_ref, w_ref)


def _out_proj_operands(latent, ya, yb, yc, x, g_a, w_out, tm):
    b, t, _ = x.shape
    _, ya_spec = _seq_split(latent, b, t, tm, BF16)
    ya = ya.reshape(ya.shape[0], ya.shape[1] // SUBLANES, SUBLANES * LRU_W)
    mod_map = (lambda i, j: (i, 0, 0)) if g_a.shape[0] > 1 else (lambda i, j: (0, 0, 0))
    tile = lambda i, j: (i, j, 0)
    if yc.ndim == 4:
        yc_spec = pl.BlockSpec((1, DF_HEADS, tm, DF_VD), lambda i, j: (i, 0, j, 0))
    else:
        yc_spec = pl.BlockSpec((1, tm, DF_V_W), tile)
    specs = [ya_spec, pl.BlockSpec((1, tm, NA_W), tile), yc_spec, pl.BlockSpec((1, tm, D_MODEL), tile),
             pl.BlockSpec((1, 1, D_MODEL), mod_map), pl.BlockSpec((D_CAT, D_MODEL), lambda i, j: (0, 0))]
    return specs, (ya, yb, yc, x, g_a, w_out)


def _out_proj(latent, ya, yb, yc, x, g_a, w_out, tm):
    b, t, _ = x.shape
    specs, args = _out_proj_operands(latent, ya, yb, yc, x, g_a, w_out, tm)
    return pl.pallas_call(
        _out_kernel,
        out_shape=jax.ShapeDtypeStruct(x.shape, F32),
        grid=(b, t // tm),
        in_specs=specs,
        out_specs=pl.BlockSpec((1, tm, D_MODEL), lambda i, j: (i, j, 0)),
        compiler_params=_cparams("arbitrary", "arbitrary"),
        name="out_proj",
    )(*args)


def _out_proj_rows(ya_ref, yb_ref, yc_ref, x_ref, ga_ref, w_ref):
    if len(yc_ref.shape) == 4:
        yc = jnp.concatenate([yc_ref[0, h] for h in range(DF_HEADS)], axis=-1)
    else:
        yc = yc_ref[0]
    y = (_dot(ya_ref[0], w_ref[0:LRU_W, :]) + _dot(yb_ref[0], w_ref[LRU_W:LRU_W + NA_W, :])
         + _dot(yc, w_ref[LRU_W + NA_W:, :]))
    return x_ref[0] + ga_ref[0] * y


def _out_ffn_kernel(ya_ref, yb_ref, yc_ref, x_ref, ga_ref, w_ref, *ffn_refs):
    x = _out_proj_rows(ya_ref, yb_ref, yc_ref, x_ref, ga_ref, w_ref)
    outside = jnp.zeros((HALO, D_MODEL), F32)
    _conv_ffn(x, outside, outside, False, False, *ffn_refs)


def _ffn_kernel(nt, xp_ref, x_ref, xn_ref, *ffn_refs):
    j = pl.program_id(1)
    _conv_ffn(x_ref[0], xp_ref[0], xn_ref[0], j > 0, j < nt - 1, *ffn_refs)


def _conv_ffn(x, x_prev, x_next, has_prev, has_next, sh_ref, sc_ref, gf_ref, g_ref, up_ref, cw_ref, cb_ref, dn_ref,
              o_ref, acc_ref):
    tm = x.shape[0]
    xw = jnp.concatenate([x_prev, x, x_next], axis=0)
    rows = xw.shape[0]
    ms = jnp.mean(xw * xw, axis=-1, keepdims=True)
    y = xw * lax.rsqrt(ms + EPS) * g_ref[...]
    h = y * (1.0 + sc_ref[0]) + sh_ref[0]
    row = lax.broadcasted_iota(jnp.int32, (rows, 1), 0)
    valid = ((row >= HALO) | has_prev) & ((row < HALO + tm) | has_next)
    h = jnp.where(valid, h, 0.0).astype(BF16)

    n_chunks = D_FF // FFN_CF

    def up_proj(c):
        return [_dot(h, up_ref[:, off:off + FFN_CF]) for off in (c * FFN_CF, D_FF + c * FFN_CF)]

    def conv(u, off):
        taps = (pltpu.roll(u, 1, 0), u, pltpu.roll(u, rows - 1, 0))
        out = cb_ref[:, off:off + FFN_CF]
        for k in range(3):
            out = out + cw_ref[k:k + 1, off:off + FFN_CF] * taps[k][HALO:HALO + tm, :]
        return out

    u_cur = up_proj(0)
    for c in range(n_chunks):
        u_next = up_proj(c + 1) if c + 1 < n_chunks else None
        act = (_gelu(conv(u_cur[1], D_FF + c * FFN_CF)) * conv(u_cur[0], c * FFN_CF)).astype(BF16)
        down = _dot(act, dn_ref[c * FFN_CF:(c + 1) * FFN_CF, :])
        if c == 0:
            acc_ref[...] = down
        else:
            acc_ref[...] += down
        u_cur = u_next
    o_ref[0] = x + gf_ref[0] * acc_ref[...]


def _ffn_operands(shift, scale, g_f, gain, up, conv_w, conv_b, down):
    mod_map = (lambda i, j: (i, 0, 0)) if shift.shape[0] > 1 else (lambda i, j: (0, 0, 0))
    const2 = lambda i, j: (0, 0)
    mod_spec = pl.BlockSpec((1, 1, D_MODEL), mod_map)
    specs = [mod_spec, mod_spec, mod_spec,
             pl.BlockSpec((1, D_MODEL), const2),
             pl.BlockSpec((D_MODEL, 2 * D_FF), const2),
             pl.BlockSpec((3, 2 * D_FF), const2),
             pl.BlockSpec((1, 2 * D_FF), const2),
             pl.BlockSpec((D_FF, D_MODEL), const2)]
    return specs, (shift, scale, g_f, gain.reshape(1, D_MODEL), up, conv_w, conv_b.reshape(1, 2 * D_FF), down)


def _ffn(x, tm, *ffn_params):
    b, t, _ = x.shape
    nt = t // tm
    hb = tm // HALO
    specs, args = _ffn_operands(*ffn_params)
    return pl.pallas_call(
        functools.partial(_ffn_kernel, nt),
        out_shape=jax.ShapeDtypeStruct(x.shape, F32),
        grid=(b, nt),
        in_specs=[pl.BlockSpec((1, HALO, D_MODEL), lambda i, j: (i, jnp.maximum(j * hb - 1, 0), 0)),
                  pl.BlockSpec((1, tm, D_MODEL), lambda i, j: (i, j, 0)),
                  pl.BlockSpec((1, HALO, D_MODEL), lambda i, j: (i, jnp.minimum((j + 1) * hb, t // HALO - 1), 0))]
                 + specs,
        out_specs=pl.BlockSpec((1, tm, D_MODEL), lambda i, j: (i, j, 0)),
        scratch_shapes=[pltpu.VMEM((tm, D_MODEL), F32)],
        compiler_params=_cparams("arbitrary", "arbitrary"),
        name="ffn",
    )(x, x, x, *args)


def _out_ffn(latent, ya, yb, yc, x, g_a, w_out, *ffn_params):
    b, t, _ = x.shape
    out_specs, out_args = _out_proj_operands(latent, ya, yb, yc, x, g_a, w_out, t)
    ffn_specs, ffn_args = _ffn_operands(*ffn_params)
    return pl.pallas_call(
        _out_ffn_kernel,
        out_shape=jax.ShapeDtypeStruct(x.shape, F32),
        grid=(b, 1),
        in_specs=out_specs + ffn_specs,
        out_specs=pl.BlockSpec((1, t, D_MODEL), lambda i, j: (i, j, 0)),
        scratch_shapes=[pltpu.VMEM((t, D_MODEL), F32)],
        compiler_params=_cparams("arbitrary", "arbitrary"),
        name="out_ffn",
    )(*out_args, *ffn_args)


def _split_mod(mod):
    return [mod[:, None, i * D_MODEL:(i + 1) * D_MODEL] for i in range(N_MOD)]


def kernel(x_prompt, x_sample, state_lru, cache_na_k, cache_na_v, cache_df_k, cache_df_v, c, c_ctx, ada_w, ada_b, norm_mix, norm_ffn, w_in, lru_conv_w, lru_conv_b, lru_wa, lru_ba, lru_wx, lru_bx, lru_lambda, na_q_norm, na_k_norm, na_rpb, df_q_norm, df_k_norm, df_lambda, df_subln, w_out, ffn_up, ffn_conv_w, ffn_conv_b, ffn_down):
    bp, tp, _ = x_prompt.shape
    bs, ts, _ = x_sample.shape
    past = cache_na_k.shape[3]
    rows_n = ts // GRID_W

    c_all = jnp.zeros((SUBLANES, D_MODEL), F32).at[0].set(c_ctx).at[1:1 + bs].set(c)
    mods = _modulation(c_all, ada_w, ada_b)
    rope_tabs = _rope_tables(ts)
    zeros_state = jnp.zeros((bp // SUBLANES, 2, SUBLANES, LRU_W), F32)

    y_p, y_s = x_prompt, x_sample
    s_lru = []
    hcache = (bp, DEPTH, NA_HEADS, tp, NA_HD)
    caches = [jnp.zeros(hcache, F32), jnp.zeros(hcache, F32),
              jnp.zeros((bp, DEPTH, DF_HEADS, 2, tp, DF_HD), F32), jnp.zeros(hcache, F32)]
    for l in range(DEPTH):
        lam_init = 0.8 - 0.6 * math.exp(-0.3 * l)
        w_in_l = w_in[l].astype(BF16)
        w_out_l = w_out[l].astype(BF16)
        up_l = ffn_up[l].astype(BF16)
        down_l = ffn_down[l].astype(BF16)
        lru_args = (lru_conv_w[l], lru_conv_b[l], lru_wa[l], lru_wx[l], lru_ba[l], lru_bx[l], lru_lambda[l])
        norm_args = (na_q_norm[l], na_k_norm[l], df_q_norm[l], df_k_norm[l])

        sh_a, sc_a, g_a, sh_f, sc_f, g_f = _split_mod(mods[l, 0:1])
        lx, lg, nq, *caches = _prep(False, y_p, sh_a, sc_a, norm_mix[l], w_in_l, *norm_args, None, tp, l, caches)
        dq = caches.pop(2)
        seq4 = lambda v: v.reshape(v.shape[0], v.shape[1], SUBLANES, LRU_W)
        ya, st = _lru(False, seq4(lx), seq4(lg), *lru_args, zeros_state, 256, LRU_RC)
        yb, yc = _ctx_attn(nq, caches[0], caches[1], dq, caches[2], caches[3], df_lambda[l], df_subln[l], norm_args,
                           lam_init, l)
        ffn_params = (norm_ffn[l], up_l, ffn_conv_w[l], ffn_conv_b[l], down_l)
        y_p = _out_ffn(False, ya, yb, yc, y_p, g_a, w_out_l, sh_f, sc_f, g_f, *ffn_params)
        s_lru.append(st.transpose(0, 2, 1, 3).reshape(bp, 2, LRU_W))

        sh_a, sc_a, g_a, sh_f, sc_f, g_f = _split_mod(mods[l, 1:1 + bs])
        lx, lg, nq, nk, nv, dq, dk, dv = _prep(True, y_s, sh_a, sc_a, norm_mix[l], w_in_l, *norm_args, rope_tabs,
                                               ts // SUBLANES)
        h0 = jnp.broadcast_to(state_lru[:, l, :, None, :], (bs, 2, SUBLANES, LRU_W))
        ya, _ = _lru(True, seq4(lx), seq4(lg), *lru_args, h0, 256, LRU_RC)
        yb = _lat_na(nq, nk, nv, cache_na_k[:, l].astype(BF16), _with_ones_column(cache_na_v[:, l].astype(BF16), NA_VA),
                     _na_bias_tables(na_rpb[l], rows_n), na_q_norm[l], na_k_norm[l], na_rpb[l])
        kt_ctx = jnp.swapaxes(cache_df_k[:, l].astype(BF16), -1, -2).reshape(bs, DF_QK_W, past // DF_CK, DF_CK)
        kt_ctx = kt_ctx.transpose(0, 2, 1, 3)
        v_ctx = _with_ones_column(cache_df_v[:, l].astype(BF16), DF_VA)
        yc = _lat_df(dq, dk, dv, kt_ctx, v_ctx, df_lambda[l], df_subln[l], df_q_norm[l], df_k_norm[l], lam_init, DF_TQ)
        x1 = _out_proj(True, ya, yb, yc, y_s, g_a, w_out_l, ts // SUBLANES)
        y_s = _ffn(x1, FFN_TM, sh_f, sc_f, g_f, *ffn_params)

    return (y_p, y_s, jnp.stack(s_lru, axis=1), *caches)
```

```python
import functools
import math

import numpy as np
import jax
import jax.numpy as jnp
from jax import lax
from jax.experimental import pallas as pl
from jax.experimental.pallas import tpu as pltpu

F32 = jnp.float32
BF16 = jnp.bfloat16

D_MODEL = 1024
DEPTH = 2
GRID_W = 64
LRU_W = 512
LRU_BLOCKS = 8
LRU_BW = LRU_W // LRU_BLOCKS
LRU_CONV = 4
LRU_C = 8.0
NA_HEADS = 4
NA_HD = 64
NA_W = NA_HEADS * NA_HD
NA_ROWS = 8
NA_COLS = 16
DF_HEADS = 4
DF_HD = 32
DF_VD = 2 * DF_HD
DF_QK_W = DF_HEADS * 2 * DF_HD
DF_V_W = DF_HEADS * DF_VD
D_IN = 2 * LRU_W + 3 * NA_W + 2 * DF_QK_W + DF_V_W
D_CAT = LRU_W + NA_W + DF_V_W
D_FF = 2816
ROPE_BASE = 10000.0
EPS = 1e-6
NEG_INF = -1e30
N_MOD = 6

OFF_LX = 0
OFF_LG = OFF_LX + LRU_W
OFF_NQ = OFF_LG + LRU_W
OFF_NK = OFF_NQ + NA_W
OFF_NV = OFF_NK + NA_W
OFF_DQ = OFF_NV + NA_W
OFF_DK = OFF_DQ + DF_QK_W
OFF_DV = OFF_DK + DF_QK_W

SUBLANES = 8
VMEM_LIMIT = 56 * 1024 * 1024

NA_QROWS = 4
NA_KROWS = NA_QROWS + NA_ROWS
DF_CK = 512
DF_TQ = 512
DF_VA = 128
NA_VA = DF_VA
DF_SAFE_SCORE = 50.0
DF_UNROLL = 8
LRU_RC = 32
LRU_UNROLL = 8
FFN_CF = 256
FFN_TM = 256
HALO = SUBLANES


def _cparams(*sem):
    return pltpu.CompilerParams(dimension_semantics=sem, vmem_limit_bytes=VMEM_LIMIT)


def _gelu(x):
    return 0.5 * x * (1.0 + jnp.tanh(math.sqrt(2.0 / math.pi) * (x + 0.044715 * (x * x * x))))


def _sigmoid(x):
    return 1.0 / (1.0 + jnp.exp(-x))


def _neg_expm1(x, u):
    safe = u != 1.0
    return jnp.where(safe, (1.0 - u) * x / jnp.log(jnp.where(safe, u, 0.5)), -x)


def _dot(a, b):
    return jnp.dot(a, b, preferred_element_type=F32)


def _dot_nt(a, b):
    return lax.dot_general(a, b, (((1,), (1,)), ((), ())), preferred_element_type=F32)


def _mod_kernel(c_ref, w_ref, b_ref, o_ref):
    c = c_ref[...]
    s = (c * _sigmoid(c)).astype(BF16)
    o_ref[0] = _dot(s, w_ref[0].astype(BF16)) + b_ref[0]


def _modulation(c_all, ada_w, ada_b):
    nb = 1536
    n = N_MOD * D_MODEL
    return pl.pallas_call(
        _mod_kernel,
        out_shape=jax.ShapeDtypeStruct((DEPTH, SUBLANES, n), F32),
        grid=(DEPTH, n // nb),
        in_specs=[pl.BlockSpec((SUBLANES, D_MODEL), lambda l, j: (0, 0)),
                  pl.BlockSpec((1, D_MODEL, nb), lambda l, j: (l, 0, j)),
                  pl.BlockSpec((1, 1, nb), lambda l, j: (l, 0, j))],
        out_specs=pl.BlockSpec((1, SUBLANES, nb), lambda l, j: (l, 0, j)),
        compiler_params=_cparams("arbitrary", "arbitrary"),
        name="modulation",
    )(c_all, ada_w, ada_b.reshape(DEPTH, 1, n))


def _group_rms(z, g_ref, gain):
    ms = _dot((z * z).astype(BF16), g_ref[...])
    return z * lax.rsqrt(ms + EPS) * gain


def _rope(x, cos, sin):
    w = x.shape[-1]
    lane = lax.broadcasted_iota(jnp.int32, x.shape, 1)
    partner = jnp.where((lane % 16) < 8, pltpu.roll(x, w - 8, 1), pltpu.roll(x, 8, 1))
    return x * cos + partner * sin


def _prep_kernel(latent, x_ref, sh_ref, sc_ref, g_ref, w_ref, qn_ref, kn_ref, dqn_ref, dkn_ref,
                 g64_ref, g32_ref, *rest):
    if latent:
        cos_ref, sin_ref, lx_ref, lg_ref, nq_ref, nk_ref, nv_ref, dq_ref, dk_ref, dv_ref = rest
    else:
        lx_ref, lg_ref, nq_ref, nk_ref, nv_ref, dq_ref, dk_ref, dv_ref = rest[-8:]
        nk_ref, nv_ref, dk_ref, dv_ref = nk_ref.at[0], nv_ref.at[0], dk_ref.at[0], dv_ref.at[0]
    x = x_ref[0]
    ms = jnp.mean(x * x, axis=-1, keepdims=True)
    y = x * lax.rsqrt(ms + EPS) * g_ref[...]
    h = (y * (1.0 + sc_ref[0]) + sh_ref[0]).astype(BF16)

    def seg(off, size):
        return _dot(h, w_ref[:, off:off + size])

    z_nq, z_nk, z_dq, z_dk = seg(OFF_NQ, NA_W), seg(OFF_NK, NA_W), seg(OFF_DQ, DF_QK_W), seg(OFF_DK, DF_QK_W)
    nv = seg(OFF_NV, NA_W)
    dv = seg(OFF_DV, DF_V_W)
    lx_ref[0] = seg(OFF_LX, LRU_W)
    lg_ref[0] = seg(OFF_LG, LRU_W)

    nq = _group_rms(z_nq, g64_ref, qn_ref[...]) * (NA_HD ** -0.5)
    nk = _group_rms(z_nk, g64_ref, kn_ref[...])
    dq = _group_rms(z_dq, g32_ref, dqn_ref[...])
    dk = _group_rms(z_dk, g32_ref, dkn_ref[...])
    if latent:
        cos = cos_ref[...]
        sin = sin_ref[...]
        dq = _rope(dq, cos, sin)
        dk = _rope(dk, cos, sin)
    dq = dq * (DF_HD ** -0.5)

    for hh in range(NA_HEADS):
        sl = slice(hh * NA_HD, (hh + 1) * NA_HD)
        nq_ref[0, hh] = nq[:, sl].astype(nq_ref.dtype)
        nk_ref[0, hh] = nk[:, sl].astype(nk_ref.dtype)
        if latent:
            dq_ref[0, hh] = dq[:, sl].astype(dq_ref.dtype)
            lane = lax.broadcasted_iota(jnp.int32, (dv.shape[0], DF_VA - DF_VD), 1)
            ones_col = jnp.where(lane == 0, 1.0, 0.0).astype(dv_ref.dtype)
            for val, ref in ((nv, nv_ref), (dv, dv_ref)):
                ref[0, hh, :, 0:DF_VD] = val[:, sl].astype(ref.dtype)
                ref[0, hh, :, DF_VD:] = ones_col
        else:
            nv_ref[0, hh] = nv[:, sl].astype(nv_ref.dtype)
            dv_ref[0, hh] = dv[:, sl].astype(dv_ref.dtype)
    if latent:
        dk_ref[0, 0] = dk.T.astype(dk_ref.dtype)
    else:
        for hc in range(2 * DF_HEADS):
            sl = slice(hc * DF_HD, (hc + 1) * DF_HD)
            dq_ref[0, hc // 2, hc % 2] = dq[:, sl].astype(dq_ref.dtype)
            dk_ref[0, hc // 2, hc % 2] = dk[:, sl].astype(dk_ref.dtype)


def _seq_split(latent, b, t, tm, dtype):
    if latent:
        assert tm * SUBLANES == t
        return (jax.ShapeDtypeStruct((b, tm, SUBLANES * LRU_W), dtype),
                pl.BlockSpec((1, tm, LRU_W), lambda i, j: (i, 0, j)))
    assert tm == t and b % SUBLANES == 0
    return (jax.ShapeDtypeStruct((b // SUBLANES, t, SUBLANES * LRU_W), dtype),
            pl.BlockSpec((1, tm, LRU_W), lambda i, j: (i // SUBLANES, 0, i % SUBLANES)))


def _group_matrix(width, group):
    idx = np.arange(width) // group
    return jnp.asarray((idx[:, None] == idx[None, :]).astype(np.float32) / group, dtype=BF16)


def _prep(latent, x, shift, scale, gain, w_in, qn, kn, dqn, dkn, rope_tabs, tm, layer=0, caches=None):
    b, t, _ = x.shape
    aliases = {}
    bm = shift.shape[0]
    mod_map = (lambda i, j: (i, 0, 0)) if bm > 1 else (lambda i, j: (0, 0, 0))
    const2 = lambda i, j: (0, 0)
    tile3 = lambda i, j: (i, j, 0)
    head4 = lambda i, j: (i, 0, j, 0)
    in_specs = [pl.BlockSpec((1, tm, D_MODEL), tile3),
                pl.BlockSpec((1, 1, D_MODEL), mod_map),
                pl.BlockSpec((1, 1, D_MODEL), mod_map),
                pl.BlockSpec((1, D_MODEL), const2),
                pl.BlockSpec((D_MODEL, D_IN), const2),
                pl.BlockSpec((1, NA_W), const2),
                pl.BlockSpec((1, NA_W), const2),
                pl.BlockSpec((1, DF_QK_W), const2),
                pl.BlockSpec((1, DF_QK_W), const2),
                pl.BlockSpec((NA_W, NA_W), const2),
                pl.BlockSpec((DF_QK_W, DF_QK_W), const2)]
    args = [x, shift, scale, gain.reshape(1, D_MODEL), w_in,
            jnp.tile(qn, NA_HEADS).reshape(1, NA_W), jnp.tile(kn, NA_HEADS).reshape(1, NA_W),
            jnp.tile(dqn, 2 * DF_HEADS).reshape(1, DF_QK_W), jnp.tile(dkn, 2 * DF_HEADS).reshape(1, DF_QK_W),
            _group_matrix(NA_W, NA_HD), _group_matrix(DF_QK_W, DF_HD)]
    head_shape = lambda dt: jax.ShapeDtypeStruct((b, NA_HEADS, t, NA_HD), dt)
    head_spec = pl.BlockSpec((1, NA_HEADS, tm, NA_HD), head4)
    seq_shape, seq_spec = _seq_split(latent, b, t, tm, F32)
    out_shape = [seq_shape, seq_shape]
    out_specs = [seq_spec, seq_spec]
    if latent:
        in_specs += [pl.BlockSpec((tm, DF_QK_W), lambda i, j: (j, 0))] * 2
        args += list(rope_tabs)
        aug_shape = jax.ShapeDtypeStruct((b, DF_HEADS, t, DF_VA), BF16)
        aug_spec = pl.BlockSpec((1, DF_HEADS, tm, DF_VA), head4)
        out_shape += [head_shape(BF16), head_shape(BF16), aug_shape,
                      head_shape(BF16), jax.ShapeDtypeStruct((b, t // tm, DF_QK_W, tm), BF16), aug_shape]
        out_specs += [head_spec, head_spec, aug_spec,
                      head_spec, pl.BlockSpec((1, 1, DF_QK_W, tm), lambda i, j: (i, j, 0, 0)), aug_spec]
    else:
        comp_shape = jax.ShapeDtypeStruct((b, DF_HEADS, 2, t, DF_HD), BF16)
        comp_spec = pl.BlockSpec((1, DF_HEADS, 2, tm, DF_HD), lambda i, j: (i, 0, 0, j, 0))
        hcache_shape = jax.ShapeDtypeStruct((b, DEPTH, NA_HEADS, t, NA_HD), F32)
        hcache_spec = pl.BlockSpec((1, 1, NA_HEADS, tm, NA_HD), lambda i, j: (i, layer, 0, j, 0))
        ccache_shape = jax.ShapeDtypeStruct((b, DEPTH, DF_HEADS, 2, t, DF_HD), F32)
        ccache_spec = pl.BlockSpec((1, 1, DF_HEADS, 2, tm, DF_HD), lambda i, j: (i, layer, 0, 0, j, 0))
        out_shape += [head_shape(BF16), hcache_shape, hcache_shape, comp_shape, ccache_shape, hcache_shape]
        out_specs += [head_spec, hcache_spec, hcache_spec, comp_spec, ccache_spec, hcache_spec]
        aliases = {len(args) + k: out_idx for k, out_idx in enumerate((3, 4, 6, 7))}
        in_specs += [pl.BlockSpec(memory_space=pl.ANY)] * 4
        args += list(caches)
    return pl.pallas_call(
        functools.partial(_prep_kernel, latent),
        out_shape=out_shape,
        grid=(b, t // tm),
        in_specs=in_specs,
        out_specs=out_specs,
        input_output_aliases=aliases,
        compiler_params=_cparams("arbitrary", "arbitrary"),
        name="prep_latent" if latent else "prep_context",
    )(*args)


def _rope_tables(t):
    half = DF_HD // 4
    freqs = ROPE_BASE ** (-jnp.arange(half, dtype=F32) / half)
    tok = jnp.arange(t)
    lane = np.arange(DF_QK_W)
    d = lane % DF_HD
    pos = jnp.where((d < DF_HD // 2)[None, :], (tok // GRID_W)[:, None], (tok % GRID_W)[:, None])
    ang = pos.astype(F32) * freqs[d % half][None, :]
    sign = np.where((d % (2 * half)) < half, -1.0, 1.0).astype(np.float32)
    return jnp.cos(ang), jnp.sin(ang) * sign[None, :]


def _lru_kernel(chained, jn, rc, lx_ref, lg_ref, cw_ref, cb_ref, wg_ref, gb_ref, lam_ref, h0_ref,
                ya_ref, st_ref, xpad_ref, af_ref, bf_ref, ab_ref, bb_ref, hf_ref):
    ct = lx_ref.shape[-1]
    row = lax.broadcasted_iota(jnp.int32, (SUBLANES, ct), 0)
    zero = jnp.zeros((SUBLANES, ct), F32)

    def from_prev_seq(v):
        return jnp.where(row >= 1, pltpu.roll(v, 1, 0), 0.0) if chained else zero

    def from_next_seq(v):
        return jnp.where(row < SUBLANES - 1, pltpu.roll(v, SUBLANES - 1, 0), 0.0) if chained else zero

    xpad_ref[0] = from_prev_seq(lx_ref[0, jn - 1])
    xpad_ref[1:jn + 1] = lx_ref[0]
    xpad_ref[jn + 1] = from_next_seq(lx_ref[0, 0])
    xpad_ref[jn + 2] = from_next_seq(lx_ref[0, 1])

    lam = -lam_ref[...]
    sp = jnp.maximum(lam, 0.0) + jnp.log1p(jnp.exp(-jnp.abs(lam)))
    cw = cw_ref[...]
    cb = cb_ref[...]
    wg = wg_ref[0]
    gb = gb_ref[0]

    def gates(i, carry):
        r0 = pl.multiple_of(i * rc, rc)
        xw = xpad_ref[pl.ds(r0, rc + LRU_CONV - 1)]
        xc = cb
        for j in range(LRU_CONV):
            xc = xc + cw[j:j + 1, :] * xw[j:j + rc]
        xc = xc.reshape(rc * SUBLANES, ct)
        g = _dot(xc.astype(BF16), wg) + gb
        for d, (a_ref, b_ref) in enumerate(((af_ref, bf_ref), (ab_ref, bb_ref))):
            r = _sigmoid(g[:, (2 * d) * ct:(2 * d + 1) * ct])
            ig = _sigmoid(g[:, (2 * d + 1) * ct:(2 * d + 2) * ct])
            log_a = (-LRU_C) * r * sp[d:d + 1, :]
            a = jnp.exp(log_a)
            b = jnp.sqrt(_neg_expm1(2.0 * log_a, a * a)) * (ig * xc)
            a_ref[pl.ds(r0, rc)] = a.reshape(rc, SUBLANES, ct)
            b_ref[pl.ds(r0, rc)] = b.reshape(rc, SUBLANES, ct)
        return carry

    lax.fori_loop(0, jn // rc, gates, 0)

    hf0 = h0_ref[0, 0]
    hb0 = h0_ref[0, 1]
    if chained:
        def summarise(j, carry):
            hf, pf, hb, pb = carry
            a = af_ref[j]
            jb = jn - 1 - j
            a2 = ab_ref[jb]
            return a * hf + bf_ref[j], a * pf, a2 * hb + bb_ref[jb], a2 * pb

        one = jnp.ones((SUBLANES, ct), F32)
        hf_end, pf_end, hb_end, pb_end = lax.fori_loop(0, jn, summarise, (zero, one, zero, one), unroll=LRU_UNROLL)
        for s in range(SUBLANES - 1):
            hf0 = jnp.where(row == s + 1, pltpu.roll(hf_end + pf_end * hf0, 1, 0), hf0)
            sb = SUBLANES - 2 - s
            hb0 = jnp.where(row == sb, pltpu.roll(hb_end + pb_end * hb0, SUBLANES - 1, 0), hb0)

    def fwd(j, h):
        h = af_ref[j] * h + bf_ref[j]
        hf_ref[j] = h
        return h

    st_ref[0, 0] = lax.fori_loop(0, jn, fwd, hf0, unroll=LRU_UNROLL)

    def bwd(j, h):
        jb = jn - 1 - j
        h = ab_ref[jb] * h + bb_ref[jb]
        hf_ref[jb] = hf_ref[jb] + h
        return h

    st_ref[0, 1] = lax.fori_loop(0, jn, bwd, hb0, unroll=LRU_UNROLL)

    def emit(i, carry):
        r0 = pl.multiple_of(i * rc, rc)
        y = (hf_ref[pl.ds(r0, rc)] * _gelu(lg_ref[0, pl.ds(r0, rc)])).reshape(rc * SUBLANES, ct)
        ya_ref[0, pl.ds(pl.multiple_of(r0 * SUBLANES, rc * SUBLANES), rc * SUBLANES), :] = y.astype(ya_ref.dtype)
        return carry

    lax.fori_loop(0, jn // rc, emit, 0)


def _lru_gate_weights(wa, wx, ba, bx, ct):
    nct = LRU_W // ct
    per = ct // LRU_BW
    eye = jnp.eye(per, dtype=F32)

    def dense(w):
        wt = w.reshape(nct, per, LRU_BW, LRU_BW)
        return jnp.einsum('cpij,pq->cpiqj', wt, eye).reshape(nct, ct, ct)

    mats = [dense(wa[0]), dense(wx[0]), dense(wa[1]), dense(wx[1])]
    bias = [ba[0], bx[0], ba[1], bx[1]]
    wg = jnp.concatenate(mats, axis=-1).astype(BF16)
    gb = jnp.concatenate([v.reshape(nct, 1, ct) for v in bias], axis=-1)
    return wg, gb


def _lru(chained, lx, lg, conv_w, conv_b, wa, wx, ba, bx, lam, h0, ct, rc):
    g, jn, _, _ = lx.shape
    nct = LRU_W // ct
    wg, gb = _lru_gate_weights(wa, wx, ba, bx, ct)
    tile = lambda i, j: (i, 0, 0, j)
    seq = (jn, SUBLANES, ct)
    return pl.pallas_call(
        functools.partial(_lru_kernel, chained, jn, rc),
        out_shape=[jax.ShapeDtypeStruct((g, jn * SUBLANES, LRU_W), BF16),
                   jax.ShapeDtypeStruct((g, 2, SUBLANES, LRU_W), F32)],
        grid=(g, nct),
        in_specs=[pl.BlockSpec((1,) + seq, tile),
                  pl.BlockSpec((1,) + seq, tile),
                  pl.BlockSpec((LRU_CONV, ct), lambda i, j: (0, j)),
                  pl.BlockSpec((1, ct), lambda i, j: (0, j)),
                  pl.BlockSpec((1, ct, 4 * ct), lambda i, j: (j, 0, 0)),
                  pl.BlockSpec((1, 1, 4 * ct), lambda i, j: (j, 0, 0)),
                  pl.BlockSpec((2, ct), lambda i, j: (0, j)),
                  pl.BlockSpec((1, 2, SUBLANES, ct), tile)],
        out_specs=[pl.BlockSpec((1, jn * SUBLANES, ct), lambda i, j: (i, 0, j)),
                   pl.BlockSpec((1, 2, SUBLANES, ct), tile)],
        scratch_shapes=[pltpu.VMEM((jn + LRU_CONV - 1, SUBLANES, ct), F32)] + [pltpu.VMEM(seq, F32)] * 5,
        compiler_params=_cparams("arbitrary", "arbitrary"),
        name="lru_latent" if chained else "lru_context",
    )(lx, lg, conv_w, conv_b.reshape(1, LRU_W), wg, gb, lam, h0)


def _df_lambda(lv_ref, lam_init):
    lv = lv_ref[...]
    l1 = jnp.sum(lv[0:1, :] * lv[1:2, :], axis=-1, keepdims=True)
    l2 = jnp.sum(lv[2:3, :] * lv[3:4, :], axis=-1, keepdims=True)
    return jnp.exp(l1) - jnp.exp(l2) + lam_init


def _diff_combine(o1, o2, lam, subln, lam_init):
    y = o1 - lam * o2
    ms = jnp.mean(y * y, axis=-1, keepdims=True)
    return y * lax.rsqrt(ms + EPS) * subln * (1.0 - lam_init)


def _ctx_attn_kernel(lam_init, nq_ref, nk_ref, nv_ref, dq_ref, dk_ref, dv_ref, lv_ref, sub_ref,
                     nqg_ref, nkg_ref, dqg_ref, dkg_ref, yb_ref, yc_ref, va_ref, safe_ref):
    nk_ref, nv_ref, dk_ref, dv_ref = nk_ref.at[0], nv_ref.at[0], dk_ref.at[0], dv_ref.at[0]
    t = nq_ref.shape[2]
    lam = _df_lambda(lv_ref, lam_init)
    sub = sub_ref[...]

    @pl.when(pl.program_id(0) == 0)
    def _():
        gmax = lambda ref: jnp.max(jnp.abs(ref[...]))
        bound = jnp.maximum(gmax(nqg_ref) * gmax(nkg_ref) * NA_HD ** 0.5, gmax(dqg_ref) * gmax(dkg_ref) * DF_HD ** 0.5)
        safe_ref[0] = (bound <= DF_SAFE_SCORE).astype(jnp.int32)
        lane = lax.broadcasted_iota(jnp.int32, (t, NA_VA - NA_HD), 1)
        va_ref[:, NA_HD:] = jnp.where(lane == 0, 1.0, 0.0).astype(va_ref.dtype)

    safe = safe_ref[0] == 1

    def heads(fast):
        def attend(q, k):
            s = _dot_nt(q, k.astype(BF16))
            if not fast:
                s = s - jnp.max(s, axis=-1, keepdims=True)
            o = _dot(jnp.exp(s).astype(BF16), va_ref[...])
            return o[:, 0:NA_HD] / o[:, NA_HD:NA_HD + 1]

        for h in range(NA_HEADS):
            va_ref[:, 0:NA_HD] = nv_ref[0, h].astype(va_ref.dtype)
            yb_ref[0, :, h * NA_HD:(h + 1) * NA_HD] = attend(nq_ref[0, h], nk_ref[0, h]).astype(yb_ref.dtype)
        for h in range(DF_HEADS):
            va_ref[:, 0:DF_VD] = dv_ref[0, h].astype(va_ref.dtype)
            o1 = attend(dq_ref[0, h, 0], dk_ref[0, h, 0])
            o2 = attend(dq_ref[0, h, 1], dk_ref[0, h, 1])
            yc_ref[0, :, h * DF_VD:(h + 1) * DF_VD] = _diff_combine(o1, o2, lam, sub, lam_init).astype(yc_ref.dtype)

    @pl.when(safe)
    def _():
        heads(True)

    @pl.when(jnp.logical_not(safe))
    def _():
        heads(False)


def _ctx_attn(nq, nk, nv, dq, dk, dv, df_lambda, df_subln, gains, lam_init, layer):
    b, _, t, _ = nq.shape
    head = pl.BlockSpec((1, NA_HEADS, t, NA_HD), lambda i: (i, 0, 0, 0))
    comp = pl.BlockSpec((1, DF_HEADS, 2, t, DF_HD), lambda i: (i, 0, 0, 0, 0))
    hcache = pl.BlockSpec((1, 1, NA_HEADS, t, NA_HD), lambda i: (i, layer, 0, 0, 0))
    ccache = pl.BlockSpec((1, 1, DF_HEADS, 2, t, DF_HD), lambda i: (i, layer, 0, 0, 0, 0))
    out = pl.BlockSpec((1, t, NA_W), lambda i: (i, 0, 0))
    return pl.pallas_call(
        functools.partial(_ctx_attn_kernel, lam_init),
        out_shape=[jax.ShapeDtypeStruct((b, t, NA_W), BF16), jax.ShapeDtypeStruct((b, t, DF_V_W), BF16)],
        grid=(b,),
        in_specs=[head, hcache, hcache, comp, ccache, hcache,
                  pl.BlockSpec((4, DF_HD), lambda i: (0, 0)),
                  pl.BlockSpec((1, DF_VD), lambda i: (0, 0))]
                 + [pl.BlockSpec((1, g.shape[0]), lambda i: (0, 0)) for g in gains],
        out_specs=[out, out],
        scratch_shapes=[pltpu.VMEM((t, NA_VA), BF16), pltpu.SMEM((1,), jnp.int32)],
        compiler_params=_cparams("arbitrary"),
        name="ctx_attn",
    )(nq, nk, nv, dq, dk, dv, df_lambda, df_subln.reshape(1, DF_VD), *[g.reshape(1, -1) for g in gains])


def _na_bias_tables(rpb, rows_n):
    n_dr, n_dc = 2 * NA_ROWS - 1, 2 * NA_COLS - 1
    qc = np.arange(GRID_W)[:, None]
    kc = np.arange(GRID_W)[None, :]
    cs = np.clip(qc - NA_COLS // 2, 0, GRID_W - NA_COLS)
    ok_col = (kc >= cs) & (kc < cs + NA_COLS)
    d_col = np.clip(kc - qc + NA_COLS - 1, 0, n_dc - 1)
    sel_col = (d_col[None] == np.arange(n_dc)[:, None, None]) & ok_col[None]
    sel_row = np.zeros((3, NA_QROWS, NA_KROWS, n_dr), np.float32)
    for kind, r0 in enumerate((0, NA_QROWS, rows_n - NA_QROWS)):
        kstart = int(np.clip(r0 - NA_ROWS // 2, 0, rows_n - NA_KROWS))
        r = r0 + np.arange(NA_QROWS)[:, None]
        kr = kstart + np.arange(NA_KROWS)[None, :]
        rs = np.clip(r - NA_ROWS // 2, 0, rows_n - NA_ROWS)
        ok_row = (kr >= rs) & (kr < rs + NA_ROWS)
        d_row = np.clip(kr - r + NA_ROWS - 1, 0, n_dr - 1)
        sel_row[kind] = (d_row[..., None] == np.arange(n_dr)) & ok_row[..., None]
    ok = (sel_row.sum(-1) > 0)[:, :, None, :, None] & ok_col[None, None, :, None, :]
    hi = lax.Precision.HIGHEST
    cols = jnp.einsum('had,dqk->haqk', rpb, jnp.asarray(sel_col, F32), precision=hi)
    tabs = jnp.einsum('tija,haqk->thiqjk', jnp.asarray(sel_row), cols, precision=hi)
    tabs = jnp.where(jnp.asarray(ok)[:, None], tabs, NEG_INF)
    return tabs.reshape(3, NA_HEADS, NA_QROWS * GRID_W, NA_KROWS * GRID_W)


def _lat_na_kernel(rows_n, q_ref, k_ref, v_ref, ck_ref, cv_ref, bias_ref, qg_ref, kg_ref, rpb_ref, o_ref, safe_ref):
    i = pl.program_id(1)
    kstart = jnp.clip(i * NA_QROWS - NA_ROWS // 2, 0, rows_n - NA_KROWS)
    k0 = pl.multiple_of(kstart * GRID_W, NA_QROWS * GRID_W)
    nk = NA_KROWS * GRID_W

    @pl.when(i == 0)
    def _():
        ck2 = jnp.zeros((1, 1), F32)
        for h in range(NA_HEADS):
            c = ck_ref[0, h].astype(F32)
            ck2 = jnp.maximum(ck2, jnp.max(jnp.sum(c * c, axis=-1, keepdims=True), axis=0, keepdims=True))
        k_max = jnp.maximum(jnp.max(jnp.abs(kg_ref[...])) * NA_HD ** 0.5, jnp.sqrt(jnp.max(ck2)))
        bound = jnp.max(jnp.abs(qg_ref[...])) * k_max + jnp.max(jnp.abs(rpb_ref[...]))
        safe_ref[0] = (bound <= DF_SAFE_SCORE).astype(jnp.int32)

    safe = safe_ref[0] == 1

    def scores(h):
        q = q_ref[0, h]
        return _dot_nt(q, k_ref[0, h, pl.ds(k0, nk), :]) + bias_ref[0, h], _dot_nt(q, ck_ref[0, h])

    def heads(fast):
        nxt = scores(0)
        for h in range(NA_HEADS):
            s_loc, s_ctx = nxt
            if h + 1 < NA_HEADS:
                nxt = scores(h + 1)
            if not fast:
                m = jnp.maximum(jnp.max(s_loc, axis=-1, keepdims=True), jnp.max(s_ctx, axis=-1, keepdims=True))
                s_loc = s_loc - m
                s_ctx = s_ctx - m
            o = (_dot(jnp.exp(s_loc).astype(BF16), v_ref[0, h, pl.ds(k0, nk), :])
                 + _dot(jnp.exp(s_ctx).astype(BF16), cv_ref[0, h]))
            o = o[:, 0:NA_HD] / o[:, NA_HD:NA_HD + 1]
            o_ref[0, :, h * NA_HD:(h + 1) * NA_HD] = o.astype(o_ref.dtype)

    @pl.when(safe)
    def _():
        heads(True)

    @pl.when(jnp.logical_not(safe))
    def _():
        heads(False)


def _lat_na(q, k, v, ck, cv, bias_tabs, q_gain, k_gain, rpb):
    b, _, t, _ = q.shape
    rows_n = t // GRID_W
    nt = rows_n // NA_QROWS
    tq = NA_QROWS * GRID_W
    past = ck.shape[2]
    full = lambda n, w: pl.BlockSpec((1, NA_HEADS, n, w), lambda i, j: (i, 0, 0, 0))
    const2 = lambda i, j: (0, 0)
    rpb2 = rpb.reshape(-1, rpb.shape[-1])

    def kind(i, j):
        return (jnp.where(j == 0, 0, jnp.where(j == nt - 1, 2, 1)), 0, 0, 0)

    return pl.pallas_call(
        functools.partial(_lat_na_kernel, rows_n),
        out_shape=jax.ShapeDtypeStruct((b, t, NA_W), BF16),
        grid=(b, nt),
        in_specs=[pl.BlockSpec((1, NA_HEADS, tq, NA_HD), lambda i, j: (i, 0, j, 0)),
                  full(t, NA_HD), full(t, NA_VA), full(past, NA_HD), full(past, NA_VA),
                  pl.BlockSpec((1, NA_HEADS, tq, NA_KROWS * GRID_W), kind),
                  pl.BlockSpec((1, NA_HD), const2), pl.BlockSpec((1, NA_HD), const2),
                  pl.BlockSpec(rpb2.shape, const2)],
        out_specs=pl.BlockSpec((1, tq, NA_W), lambda i, j: (i, j, 0)),
        scratch_shapes=[pltpu.SMEM((1,), jnp.int32)],
        compiler_params=_cparams("arbitrary", "arbitrary"),
        name="lat_na",
    )(q, k, v, ck, cv, bias_tabs, q_gain.reshape(1, NA_HD), k_gain.reshape(1, NA_HD), rpb2)


def _lat_df_kernel(lam_init, q_ref, ktl_ref, vl_ref, ktc_ref, vc_ref, lv_ref, sub_ref, qg_ref, kg_ref,
                   o_ref, safe_ref):
    tq = q_ref.shape[2]
    n_lat = ktl_ref.shape[1]
    n_ctx = ktc_ref.shape[1]
    lam = _df_lambda(lv_ref, lam_init)
    q = q_ref[0, 0]
    qs = [q[:, c * DF_HD:(c + 1) * DF_HD] for c in range(2)]
    rows = [slice(c * DF_HD, (c + 1) * DF_HD) for c in range(2)]

    @pl.when(pl.program_id(2) == 0)
    def _():
        ck2 = jnp.zeros((1, DF_CK), F32)
        for c in range(2):
            for j in range(n_ctx):
                k = ktc_ref[0, j, rows[c], :].astype(F32)
                ck2 = jnp.maximum(ck2, jnp.sum(k * k, axis=0, keepdims=True))
        k_max = jnp.maximum(jnp.max(jnp.abs(kg_ref[...])) * DF_HD ** 0.5, jnp.sqrt(jnp.max(ck2)))
        safe_ref[0] = (jnp.max(jnp.abs(qg_ref[...])) * k_max <= DF_SAFE_SCORE).astype(jnp.int32)

    safe = safe_ref[0] == 1

    def finish(a1, a2):
        o1 = a1[:, 0:DF_VD] / a1[:, DF_VD:DF_VD + 1]
        o2 = a2[:, 0:DF_VD] / a2[:, DF_VD:DF_VD + 1]
        o_ref[0, 0] = _diff_combine(o1, o2, lam, sub_ref[...], lam_init).astype(o_ref.dtype)

    def over_chunks(step, carry):
        def trip(t, carry):
            for u in range(DF_UNROLL):
                j = t * DF_UNROLL + u
                v = vl_ref[0, 0, pl.ds(pl.multiple_of(j * DF_CK, DF_CK), DF_CK), :]
                carry = step(carry, lambda c: ktl_ref[0, j, rows[c], :], v)
            return carry

        carry = lax.fori_loop(0, n_lat // DF_UNROLL, trip, carry)
        for j in range(n_ctx):
            carry = step(carry, lambda c: ktc_ref[0, j, rows[c], :], vc_ref[0, 0, j * DF_CK:(j + 1) * DF_CK, :])
        return carry

    @pl.when(safe)
    def _():
        def step(acc, kt, v):
            e = [jnp.exp(_dot(qs[c], kt(c))).astype(BF16) for c in range(2)]
            return acc + _dot(jnp.concatenate(e, axis=0), v)

        acc = over_chunks(step, jnp.zeros((2 * tq, DF_VA), F32))
        finish(acc[0:tq], acc[tq:])

    @pl.when(jnp.logical_not(safe))
    def _():
        def step(carry, kt, v):
            new = []
            for c in range(2):
                m, acc = carry[c]
                s = _dot(qs[c], kt(c))
                m_new = jnp.maximum(m, jnp.max(s, axis=-1, keepdims=True))
                new.append((m_new, jnp.exp(m - m_new) * acc + _dot(jnp.exp(s - m_new).astype(BF16), v)))
            return tuple(new)

        init = (jnp.full((tq, 1), NEG_INF, F32), jnp.zeros((tq, DF_VA), F32))
        (_, a1), (_, a2) = over_chunks(step, (init, init))
        finish(a1, a2)


def _lat_df(q, kt_lat, v_lat, kt_ctx, v_ctx, df_lambda, df_subln, q_gain, k_gain, lam_init, tq):
    b, _, t, _ = q.shape
    const2 = lambda i, h, j: (0, 0)
    assert kt_lat.shape[1] % DF_UNROLL == 0
    kt_spec = lambda kt: pl.BlockSpec((1, kt.shape[1], 2 * DF_HD, DF_CK), lambda i, h, j: (i, 0, h, 0))
    v_spec = lambda v: pl.BlockSpec((1, 1, v.shape[2], DF_VA), lambda i, h, j: (i, h, 0, 0))
    return pl.pallas_call(
        functools.partial(_lat_df_kernel, lam_init),
        out_shape=jax.ShapeDtypeStruct((b, DF_HEADS, t, DF_VD), BF16),
        grid=(b, DF_HEADS, t // tq),
        in_specs=[pl.BlockSpec((1, 1, tq, 2 * DF_HD), lambda i, h, j: (i, h, j, 0)),
                  kt_spec(kt_lat), v_spec(v_lat), kt_spec(kt_ctx), v_spec(v_ctx),
                  pl.BlockSpec((4, DF_HD), const2), pl.BlockSpec((1, DF_VD), const2),
                  pl.BlockSpec((1, DF_HD), const2), pl.BlockSpec((1, DF_HD), const2)],
        out_specs=pl.BlockSpec((1, 1, tq, DF_VD), lambda i, h, j: (i, h, j, 0)),
        scratch_shapes=[pltpu.SMEM((1,), jnp.int32)],
        compiler_params=_cparams("arbitrary", "arbitrary", "arbitrary"),
        name="lat_df",
    )(q, kt_lat, v_lat, kt_ctx, v_ctx, df_lambda, df_subln.reshape(1, DF_VD),
      q_gain.reshape(1, DF_HD), k_gain.reshape(1, DF_HD))


def _with_ones_column(v, width):
    pad = width - v.shape[-1] - 1
    return jnp.concatenate([v, jnp.ones(v.shape[:-1] + (1,), v.dtype), jnp.zeros(v.shape[:-1] + (pad,), v.dtype)],
                           axis=-1)


def _out_kernel(ya_ref, yb_ref, yc_ref, x_ref, ga_ref, w_ref, o_ref):
    o_ref[0] = _out_proj_rows(ya_ref, yb_ref, yc_ref, x_ref, ga_ref, w_ref)


def _out_proj_operands(latent, ya, yb, yc, x, g_a, w_out, tm):
    b, t, _ = x.shape
    _, ya_spec = _seq_split(latent, b, t, tm, BF16)
    ya = ya.reshape(ya.shape[0], ya.shape[1] // SUBLANES, SUBLANES * LRU_W)
    mod_map = (lambda i, j: (i, 0, 0)) if g_a.shape[0] > 1 else (lambda i, j: (0, 0, 0))
    tile = lambda i, j: (i, j, 0)
    if yc.ndim == 4:
        yc_spec = pl.BlockSpec((1, DF_HEADS, tm, DF_VD), lambda i, j: (i, 0, j, 0))
    else:
        yc_spec = pl.BlockSpec((1, tm, DF_V_W), tile)
    specs = [ya_spec, pl.BlockSpec((1, tm, NA_W), tile), yc_spec, pl.BlockSpec((1, tm, D_MODEL), tile),
             pl.BlockSpec((1, 1, D_MODEL), mod_map), pl.BlockSpec((D_CAT, D_MODEL), lambda i, j: (0, 0))]
    return specs, (ya, yb, yc, x, g_a, w_out)


def _out_proj(latent, ya, yb, yc, x, g_a, w_out, tm):
    b, t, _ = x.shape
    specs, args = _out_proj_operands(latent, ya, yb, yc, x, g_a, w_out, tm)
    return pl.pallas_call(
        _out_kernel,
        out_shape=jax.ShapeDtypeStruct(x.shape, F32),
        grid=(b, t // tm),
        in_specs=specs,
        out_specs=pl.BlockSpec((1, tm, D_MODEL), lambda i, j: (i, j, 0)),
        compiler_params=_cparams("arbitrary", "arbitrary"),
        name="out_proj",
    )(*args)


def _out_proj_rows(ya_ref, yb_ref, yc_ref, x_ref, ga_ref, w_ref):
    if len(yc_ref.shape) == 4:
        yc = jnp.concatenate([yc_ref[0, h] for h in range(DF_HEADS)], axis=-1)
    else:
        yc = yc_ref[0]
    y = (_dot(ya_ref[0], w_ref[0:LRU_W, :]) + _dot(yb_ref[0], w_ref[LRU_W:LRU_W + NA_W, :])
         + _dot(yc, w_ref[LRU_W + NA_W:, :]))
    return x_ref[0] + ga_ref[0] * y


def _out_ffn_kernel(ya_ref, yb_ref, yc_ref, x_ref, ga_ref, w_ref, *ffn_refs):
    x = _out_proj_rows(ya_ref, yb_ref, yc_ref, x_ref, ga_ref, w_ref)
    outside = jnp.zeros((HALO, D_MODEL), F32)
    _conv_ffn(x, outside, outside, False, False, *ffn_refs)


def _ffn_kernel(nt, xp_ref, x_ref, xn_ref, *ffn_refs):
    j = pl.program_id(1)
    _conv_ffn(x_ref[0], xp_ref[0], xn_ref[0], j > 0, j < nt - 1, *ffn_refs)


def _conv_ffn(x, x_prev, x_next, has_prev, has_next, sh_ref, sc_ref, gf_ref, g_ref, up_ref, cw_ref, cb_ref, dn_ref,
              o_ref, acc_ref):
    tm = x.shape[0]
    xw = jnp.concatenate([x_prev, x, x_next], axis=0)
    rows = xw.shape[0]
    ms = jnp.mean(xw * xw, axis=-1, keepdims=True)
    y = xw * lax.rsqrt(ms + EPS) * g_ref[...]
    h = y * (1.0 + sc_ref[0]) + sh_ref[0]
    row = lax.broadcasted_iota(jnp.int32, (rows, 1), 0)
    valid = ((row >= HALO) | has_prev) & ((row < HALO + tm) | has_next)
    h = jnp.where(valid, h, 0.0).astype(BF16)

    n_chunks = D_FF // FFN_CF

    def up_proj(c):
        return [_dot(h, up_ref[:, off:off + FFN_CF]) for off in (c * FFN_CF, D_FF + c * FFN_CF)]

    def conv(u, off):
        taps = (pltpu.roll(u, 1, 0), u, pltpu.roll(u, rows - 1, 0))
        out = cb_ref[:, off:off + FFN_CF]
        for k in range(3):
            out = out + cw_ref[k:k + 1, off:off + FFN_CF] * taps[k][HALO:HALO + tm, :]
        return out

    u_cur = up_proj(0)
    for c in range(n_chunks):
        u_next = up_proj(c + 1) if c + 1 < n_chunks else None
        act = (_gelu(conv(u_cur[1], D_FF + c * FFN_CF)) * conv(u_cur[0], c * FFN_CF)).astype(BF16)
        down = _dot(act, dn_ref[c * FFN_CF:(c + 1) * FFN_CF, :])
        if c == 0:
            acc_ref[...] = down
        else:
            acc_ref[...] += down
        u_cur = u_next
    o_ref[0] = x + gf_ref[0] * acc_ref[...]


def _ffn_operands(shift, scale, g_f, gain, up, conv_w, conv_b, down):
    mod_map = (lambda i, j: (i, 0, 0)) if shift.shape[0] > 1 else (lambda i, j: (0, 0, 0))
    const2 = lambda i, j: (0, 0)
    mod_spec = pl.BlockSpec((1, 1, D_MODEL), mod_map)
    specs = [mod_spec, mod_spec, mod_spec,
             pl.BlockSpec((1, D_MODEL), const2),
             pl.BlockSpec((D_MODEL, 2 * D_FF), const2),
             pl.BlockSpec((3, 2 * D_FF), const2),
             pl.BlockSpec((1, 2 * D_FF), const2),
             pl.BlockSpec((D_FF, D_MODEL), const2)]
    return specs, (shift, scale, g_f, gain.reshape(1, D_MODEL), up, conv_w, conv_b.reshape(1, 2 * D_FF), down)


def _ffn(x, tm, *ffn_params):
    b, t, _ = x.shape
    nt = t // tm
    hb = tm // HALO
    specs, args = _ffn_operands(*ffn_params)
    return pl.pallas_call(
        functools.partial(_ffn_kernel, nt),
        out_shape=jax.ShapeDtypeStruct(x.shape, F32),
        grid=(b, nt),
        in_specs=[pl.BlockSpec((1, HALO, D_MODEL), lambda i, j: (i, jnp.maximum(j * hb - 1, 0), 0)),
                  pl.BlockSpec((1, tm, D_MODEL), lambda i, j: (i, j, 0)),
                  pl.BlockSpec((1, HALO, D_MODEL), lambda i, j: (i, jnp.minimum((j + 1) * hb, t // HALO - 1), 0))]
                 + specs,
        out_specs=pl.BlockSpec((1, tm, D_MODEL), lambda i, j: (i, j, 0)),
        scratch_shapes=[pltpu.VMEM((tm, D_MODEL), F32)],
        compiler_params=_cparams("arbitrary", "arbitrary"),
        name="ffn",
    )(x, x, x, *args)


def _out_ffn(latent, ya, yb, yc, x, g_a, w_out, *ffn_params):
    b, t, _ = x.shape
    out_specs, out_args = _out_proj_operands(latent, ya, yb, yc, x, g_a, w_out, t)
    ffn_specs, ffn_args = _ffn_operands(*ffn_params)
    return pl.pallas_call(
        _out_ffn_kernel,
        out_shape=jax.ShapeDtypeStruct(x.shape, F32),
        grid=(b, 1),
        in_specs=out_specs + ffn_specs,
        out_specs=pl.BlockSpec((1, t, D_MODEL), lambda i, j: (i, j, 0)),
        scratch_shapes=[pltpu.VMEM((t, D_MODEL), F32)],
        compiler_params=_cparams("arbitrary", "arbitrary"),
        name="out_ffn",
    )(*out_args, *ffn_args)


def _split_mod(mod):
    return [mod[:, None, i * D_MODEL:(i + 1) * D_MODEL] for i in range(N_MOD)]


def kernel(x_prompt, x_sample, state_lru, cache_na_k, cache_na_v, cache_df_k, cache_df_v, c, c_ctx, ada_w, ada_b, norm_mix, norm_ffn, w_in, lru_conv_w, lru_conv_b, lru_wa, lru_ba, lru_wx, lru_bx, lru_lambda, na_q_norm, na_k_norm, na_rpb, df_q_norm, df_k_norm, df_lambda, df_subln, w_out, ffn_up, ffn_conv_w, ffn_conv_b, ffn_down):
    bp, tp, _ = x_prompt.shape
    bs, ts, _ = x_sample.shape
    past = cache_na_k.shape[3]
    rows_n = ts // GRID_W

    c_all = jnp.zeros((SUBLANES, D_MODEL), F32).at[0].set(c_ctx).at[1:1 + bs].set(c)
    mods = _modulation(c_all, ada_w, ada_b)
    rope_tabs = _rope_tables(ts)
    zeros_state = jnp.zeros((bp // SUBLANES, 2, SUBLANES, LRU_W), F32)

    y_p, y_s = x_prompt, x_sample
    s_lru = []
    hcache = (bp, DEPTH, NA_HEADS, tp, NA_HD)
    caches = [jnp.zeros(hcache, F32), jnp.zeros(hcache, F32),
              jnp.zeros((bp, DEPTH, DF_HEADS, 2, tp, DF_HD), F32), jnp.zeros(hcache, F32)]
    for l in range(DEPTH):
        lam_init = 0.8 - 0.6 * math.exp(-0.3 * l)
        w_in_l = w_in[l].astype(BF16)
        w_out_l = w_out[l].astype(BF16)
        up_l = ffn_up[l].astype(BF16)
        down_l = ffn_down[l].astype(BF16)
        lru_args = (lru_conv_w[l], lru_conv_b[l], lru_wa[l], lru_wx[l], lru_ba[l], lru_bx[l], lru_lambda[l])
        norm_args = (na_q_norm[l], na_k_norm[l], df_q_norm[l], df_k_norm[l])

        sh_a, sc_a, g_a, sh_f, sc_f, g_f = _split_mod(mods[l, 0:1])
        lx, lg, nq, *caches = _prep(False, y_p, sh_a, sc_a, norm_mix[l], w_in_l, *norm_args, None, tp, l, caches)
        dq = caches.pop(2)
        seq4 = lambda v: v.reshape(v.shape[0], v.shape[1], SUBLANES, LRU_W)
        ya, st = _lru(False, seq4(lx), seq4(lg), *lru_args, zeros_state, 256, LRU_RC)
        yb, yc = _ctx_attn(nq, caches[0], caches[1], dq, caches[2], caches[3], df_lambda[l], df_subln[l], norm_args,
                           lam_init, l)
        ffn_params = (norm_ffn[l], up_l, ffn_conv_w[l], ffn_conv_b[l], down_l)
        y_p = _out_ffn(False, ya, yb, yc, y_p, g_a, w_out_l, sh_f, sc_f, g_f, *ffn_params)
        s_lru.append(st.transpose(0, 2, 1, 3).reshape(bp, 2, LRU_W))

        sh_a, sc_a, g_a, sh_f, sc_f, g_f = _split_mod(mods[l, 1:1 + bs])
        lx, lg, nq, nk, nv, dq, dk, dv = _prep(True, y_s, sh_a, sc_a, norm_mix[l], w_in_l, *norm_args, rope_tabs,
                                               ts // SUBLANES)
        h0 = jnp.broadcast_to(state_lru[:, l, :, None, :], (bs, 2, SUBLANES, LRU_W))
        ya, _ = _lru(True, seq4(lx), seq4(lg), *lru_args, h0, 256, LRU_RC)
        yb = _lat_na(nq, nk, nv, cache_na_k[:, l].astype(BF16), _with_ones_column(cache_na_v[:, l].astype(BF16), NA_VA),
                     _na_bias_tables(na_rpb[l], rows_n), na_q_norm[l], na_k_norm[l], na_rpb[l])
        kt_ctx = jnp.swapaxes(cache_df_k[:, l].astype(BF16), -1, -2).reshape(bs, DF_QK_W, past // DF_CK, DF_CK)
        kt_ctx = kt_ctx.transpose(0, 2, 1, 3)
        v_ctx = _with_ones_column(cache_df_v[:, l].astype(BF16), DF_VA)
        yc = _lat_df(dq, dk, dv, kt_ctx, v_ctx, df_lambda[l], df_subln[l], df_q_norm[l], df_k_norm[l], lam_init, DF_TQ)
        x1 = _out_proj(True, ya, yb, yc, y_s, g_a, w_out_l, ts // SUBLANES)
        y_s = _ffn(x1, FFN_TM, sh_f, sc_f, g_f, *ffn_params)

    return (y_p, y_s, jnp.stack(s_lru, axis=1), *caches)
```

```python
import functools
import math

import numpy as np
import jax
import jax.numpy as jnp
from jax import lax
from jax.experimental import pallas as pl
from jax.experimental.pallas import tpu as pltpu

F32 = jnp.float32
BF16 = jnp.bfloat16

D_MODEL = 1024
DEPTH = 2
GRID_W = 64
LRU_W = 512
LRU_BLOCKS = 8
LRU_BW = LRU_W // LRU_BLOCKS
LRU_CONV = 4
LRU_C = 8.0
NA_HEADS = 4
NA_HD = 64
NA_W = NA_HEADS * NA_HD
NA_ROWS = 8
NA_COLS = 16
DF_HEADS = 4
DF_HD = 32
DF_VD = 2 * DF_HD
DF_QK_W = DF_HEADS * 2 * DF_HD
DF_V_W = DF_HEADS * DF_VD
D_IN = 2 * LRU_W + 3 * NA_W + 2 * DF_QK_W + DF_V_W
D_CAT = LRU_W + NA_W + DF_V_W
D_FF = 2816
ROPE_BASE = 10000.0
EPS = 1e-6
NEG_INF = -1e30
N_MOD = 6

OFF_LX = 0
OFF_LG = OFF_LX + LRU_W
OFF_NQ = OFF_LG + LRU_W
OFF_NK = OFF_NQ + NA_W
OFF_NV = OFF_NK + NA_W
OFF_DQ = OFF_NV + NA_W
OFF_DK = OFF_DQ + DF_QK_W
OFF_DV = OFF_DK + DF_QK_W

SUBLANES = 8
VMEM_LIMIT = 56 * 1024 * 1024

NA_QROWS = 4
NA_KROWS = NA_QROWS + NA_ROWS
DF_CK = 512
DF_TQ = 512
DF_VA = 128
NA_VA = DF_VA
DF_SAFE_SCORE = 50.0
DF_UNROLL = 8
LRU_RC = 32
LRU_UNROLL = 8
FFN_CF = 256
FFN_TM = 256
HALO = SUBLANES


def _cparams(*sem):
    return pltpu.CompilerParams(dimension_semantics=sem, vmem_limit_bytes=VMEM_LIMIT)


def _gelu(x):
    return 0.5 * x * (1.0 + jnp.tanh(math.sqrt(2.0 / math.pi) * (x + 0.044715 * (x * x * x))))


def _sigmoid(x):
    return 1.0 / (1.0 + jnp.exp(-x))


def _neg_expm1(x, u):
    safe = u != 1.0
    return jnp.where(safe, (1.0 - u) * x / jnp.log(jnp.where(safe, u, 0.5)), -x)


def _dot(a, b):
    return jnp.dot(a, b, preferred_element_type=F32)


def _dot_nt(a, b):
    return lax.dot_general(a, b, (((1,), (1,)), ((), ())), preferred_element_type=F32)


def _mod_kernel(c_ref, w_ref, b_ref, o_ref):
    c = c_ref[...]
    s = (c * _sigmoid(c)).astype(BF16)
    o_ref[0] = _dot(s, w_ref[0].astype(BF16)) + b_ref[0]


def _modulation(c_all, ada_w, ada_b):
    nb = 1536
    n = N_MOD * D_MODEL
    return pl.pallas_call(
        _mod_kernel,
        out_shape=jax.ShapeDtypeStruct((DEPTH, SUBLANES, n), F32),
        grid=(DEPTH, n // nb),
        in_specs=[pl.BlockSpec((SUBLANES, D_MODEL), lambda l, j: (0, 0)),
                  pl.BlockSpec((1, D_MODEL, nb), lambda l, j: (l, 0, j)),
                  pl.BlockSpec((1, 1, nb), lambda l, j: (l, 0, j))],
        out_specs=pl.BlockSpec((1, SUBLANES, nb), lambda l, j: (l, 0, j)),
        compiler_params=_cparams("arbitrary", "arbitrary"),
        name="modulation",
    )(c_all, ada_w, ada_b.reshape(DEPTH, 1, n))


def _group_rms(z, g_ref, gain):
    ms = _dot((z * z).astype(BF16), g_ref[...])
    return z * lax.rsqrt(ms + EPS) * gain


def _rope(x, cos, sin):
    w = x.shape[-1]
    lane = lax.broadcasted_iota(jnp.int32, x.shape, 1)
    partner = jnp.where((lane % 16) < 8, pltpu.roll(x, w - 8, 1), pltpu.roll(x, 8, 1))
    return x * cos + partner * sin


def _prep_kernel(latent, x_ref, sh_ref, sc_ref, g_ref, w_ref, qn_ref, kn_ref, dqn_ref, dkn_ref,
                 g64_ref, g32_ref, *rest):
    if latent:
        cos_ref, sin_ref, lx_ref, lg_ref, nq_ref, nk_ref, nv_ref, dq_ref, dk_ref, dv_ref = rest
    else:
        lx_ref, lg_ref, nq_ref, nk_ref, nv_ref, dq_ref, dk_ref, dv_ref = rest[-8:]
        nk_ref, nv_ref, dk_ref, dv_ref = nk_ref.at[0], nv_ref.at[0], dk_ref.at[0], dv_ref.at[0]
    x = x_ref[0]
    ms = jnp.mean(x * x, axis=-1, keepdims=True)
    y = x * lax.rsqrt(ms + EPS) * g_ref[...]
    h = (y * (1.0 + sc_ref[0]) + sh_ref[0]).astype(BF16)

    def seg(off, size):
        return _dot(h, w_ref[:, off:off + size])

    z_nq, z_nk, z_dq, z_dk = seg(OFF_NQ, NA_W), seg(OFF_NK, NA_W), seg(OFF_DQ, DF_QK_W), seg(OFF_DK, DF_QK_W)
    nv = seg(OFF_NV, NA_W)
    dv = seg(OFF_DV, DF_V_W)
    lx_ref[0] = seg(OFF_LX, LRU_W)
    lg_ref[0] = seg(OFF_LG, LRU_W)

    nq = _group_rms(z_nq, g64_ref, qn_ref[...]) * (NA_HD ** -0.5)
    nk = _group_rms(z_nk, g64_ref, kn_ref[...])
    dq = _group_rms(z_dq, g32_ref, dqn_ref[...])
    dk = _group_rms(z_dk, g32_ref, dkn_ref[...])
    if latent:
        cos = cos_ref[...]
        sin = sin_ref[...]
        dq = _rope(dq, cos, sin)
        dk = _rope(dk, cos, sin)
    dq = dq * (DF_HD ** -0.5)

    for hh in range(NA_HEADS):
        sl = slice(hh * NA_HD, (hh + 1) * NA_HD)
        nq_ref[0, hh] = nq[:, sl].astype(nq_ref.dtype)
        nk_ref[0, hh] = nk[:, sl].astype(nk_ref.dtype)
        if latent:
            dq_ref[0, hh] = dq[:, sl].astype(dq_ref.dtype)
            lane = lax.broadcasted_iota(jnp.int32, (dv.shape[0], DF_VA - DF_VD), 1)
            ones_col = jnp.where(lane == 0, 1.0, 0.0).astype(dv_ref.dtype)
            for val, ref in ((nv, nv_ref), (dv, dv_ref)):
                ref[0, hh, :, 0:DF_VD] = val[:, sl].astype(ref.dtype)
                ref[0, hh, :, DF_VD:] = ones_col
        else:
            nv_ref[0, hh] = nv[:, sl].astype(nv_ref.dtype)
            dv_ref[0, hh] = dv[:, sl].astype(dv_ref.dtype)
    if latent:
        dk_ref[0, 0] = dk.T.astype(dk_ref.dtype)
    else:
        for hc in range(2 * DF_HEADS):
            sl = slice(hc * DF_HD, (hc + 1) * DF_HD)
            dq_ref[0, hc // 2, hc % 2] = dq[:, sl].astype(dq_ref.dtype)
            dk_ref[0, hc // 2, hc % 2] = dk[:, sl].astype(dk_ref.dtype)


def _seq_split(latent, b, t, tm, dtype):
    if latent:
        assert tm * SUBLANES == t
        return (jax.ShapeDtypeStruct((b, tm, SUBLANES * LRU_W), dtype),
                pl.BlockSpec((1, tm, LRU_W), lambda i, j: (i, 0, j)))
    assert tm == t and b % SUBLANES == 0
    return (jax.ShapeDtypeStruct((b // SUBLANES, t, SUBLANES * LRU_W), dtype),
            pl.BlockSpec((1, tm, LRU_W), lambda i, j: (i // SUBLANES, 0, i % SUBLANES)))


def _group_matrix(width, group):
    idx = np.arange(width) // group
    return jnp.asarray((idx[:, None] == idx[None, :]).astype(np.float32) / group, dtype=BF16)


def _prep(latent, x, shift, scale, gain, w_in, qn, kn, dqn, dkn, rope_tabs, tm, layer=0, caches=None):
    b, t, _ = x.shape
    aliases = {}
    bm = shift.shape[0]
    mod_map = (lambda i, j: (i, 0, 0)) if bm > 1 else (lambda i, j: (0, 0, 0))
    const2 = lambda i, j: (0, 0)
    tile3 = lambda i, j: (i, j, 0)
    head4 = lambda i, j: (i, 0, j, 0)
    in_specs = [pl.BlockSpec((1, tm, D_MODEL), tile3),
                pl.BlockSpec((1, 1, D_MODEL), mod_map),
                pl.BlockSpec((1, 1, D_MODEL), mod_map),
                pl.BlockSpec((1, D_MODEL), const2),
                pl.BlockSpec((D_MODEL, D_IN), const2),
                pl.BlockSpec((1, NA_W), const2),
                pl.BlockSpec((1, NA_W), const2),
                pl.BlockSpec((1, DF_QK_W), const2),
                pl.BlockSpec((1, DF_QK_W), const2),
                pl.BlockSpec((NA_W, NA_W), const2),
                pl.BlockSpec((DF_QK_W, DF_QK_W), const2)]
    args = [x, shift, scale, gain.reshape(1, D_MODEL), w_in,
            jnp.tile(qn, NA_HEADS).reshape(1, NA_W), jnp.tile(kn, NA_HEADS).reshape(1, NA_W),
            jnp.tile(dqn, 2 * DF_HEADS).reshape(1, DF_QK_W), jnp.tile(dkn, 2 * DF_HEADS).reshape(1, DF_QK_W),
            _group_matrix(NA_W, NA_HD), _group_matrix(DF_QK_W, DF_HD)]
    head_shape = lambda dt: jax.ShapeDtypeStruct((b, NA_HEADS, t, NA_HD), dt)
    head_spec = pl.BlockSpec((1, NA_HEADS, tm, NA_HD), head4)
    seq_shape, seq_spec = _seq_split(latent, b, t, tm, F32)
    out_shape = [seq_shape, seq_shape]
    out_specs = [seq_spec, seq_spec]
    if latent:
        in_specs += [pl.BlockSpec((tm, DF_QK_W), lambda i, j: (j, 0))] * 2
        args += list(rope_tabs)
        aug_shape = jax.ShapeDtypeStruct((b, DF_HEADS, t, DF_VA), BF16)
        aug_spec = pl.BlockSpec((1, DF_HEADS, tm, DF_VA), head4)
        out_shape += [head_shape(BF16), head_shape(BF16), aug_shape,
                      head_shape(BF16), jax.ShapeDtypeStruct((b, t // tm, DF_QK_W, tm), BF16), aug_shape]
        out_specs += [head_spec, head_spec, aug_spec,
                      head_spec, pl.BlockSpec((1, 1, DF_QK_W, tm), lambda i, j: (i, j, 0, 0)), aug_spec]
    else:
        comp_shape = jax.ShapeDtypeStruct((b, DF_HEADS, 2, t, DF_HD), BF16)
        comp_spec = pl.BlockSpec((1, DF_HEADS, 2, tm, DF_HD), lambda i, j: (i, 0, 0, j, 0))
        hcache_shape = jax.ShapeDtypeStruct((b, DEPTH, NA_HEADS, t, NA_HD), F32)
        hcache_spec = pl.BlockSpec((1, 1, NA_HEADS, tm, NA_HD), lambda i, j: (i, layer, 0, j, 0))
        ccache_shape = jax.ShapeDtypeStruct((b, DEPTH, DF_HEADS, 2, t, DF_HD), F32)
        ccache_spec = pl.BlockSpec((1, 1, DF_HEADS, 2, tm, DF_HD), lambda i, j: (i, layer, 0, 0, j, 0))
        out_shape += [head_shape(BF16), hcache_shape, hcache_shape, comp_shape, ccache_shape, hcache_shape]
        out_specs += [head_spec, hcache_spec, hcache_spec, comp_spec, ccache_spec, hcache_spec]
        aliases = {len(args) + k: out_idx for k, out_idx in enumerate((3, 4, 6, 7))}
        in_specs += [pl.BlockSpec(memory_space=pl.ANY)] * 4
        args += list(caches)
    return pl.pallas_call(
        functools.partial(_prep_kernel, latent),
        out_shape=out_shape,
        grid=(b, t // tm),
        in_specs=in_specs,
        out_specs=out_specs,
        input_output_aliases=aliases,
        compiler_params=_cparams("arbitrary", "arbitrary"),
        name="prep_latent" if latent else "prep_context",
    )(*args)


def _rope_tables(t):
    half = DF_HD // 4
    freqs = ROPE_BASE ** (-jnp.arange(half, dtype=F32) / half)
    tok = jnp.arange(t)
    lane = np.arange(DF_QK_W)
    d = lane % DF_HD
    pos = jnp.where((d < DF_HD // 2)[None, :], (tok // GRID_W)[:, None], (tok % GRID_W)[:, None])
    ang = pos.astype(F32) * freqs[d % half][None, :]
    sign = np.where((d % (2 * half)) < half, -1.0, 1.0).astype(np.float32)
    return jnp.cos(ang), jnp.sin(ang) * sign[None, :]


def _lru_kernel(chained, jn, rc, lx_ref, lg_ref, cw_ref, cb_ref, wg_ref, gb_ref, lam_ref, h0_ref,
                ya_ref, st_ref, xpad_ref, af_ref, bf_ref, ab_ref, bb_ref, hf_ref):
    ct = lx_ref.shape[-1]
    row = lax.broadcasted_iota(jnp.int32, (SUBLANES, ct), 0)
    zero = jnp.zeros((SUBLANES, ct), F32)

    def from_prev_seq(v):
        return jnp.where(row >= 1, pltpu.roll(v, 1, 0), 0.0) if chained else zero

    def from_next_seq(v):
        return jnp.where(row < SUBLANES - 1, pltpu.roll(v, SUBLANES - 1, 0), 0.0) if chained else zero

    xpad_ref[0] = from_prev_seq(lx_ref[0, jn - 1])
    xpad_ref[1:jn + 1] = lx_ref[0]
    xpad_ref[jn + 1] = from_next_seq(lx_ref[0, 0])
    xpad_ref[jn + 2] = from_next_seq(lx_ref[0, 1])

    lam = -lam_ref[...]
    sp = jnp.maximum(lam, 0.0) + jnp.log1p(jnp.exp(-jnp.abs(lam)))
    cw = cw_ref[...]
    cb = cb_ref[...]
    wg = wg_ref[0]
    gb = gb_ref[0]

    def gates(i, carry):
        r0 = pl.multiple_of(i * rc, rc)
        xw = xpad_ref[pl.ds(r0, rc + LRU_CONV - 1)]
        xc = cb
        for j in range(LRU_CONV):
            xc = xc + cw[j:j + 1, :] * xw[j:j + rc]
        xc = xc.reshape(rc * SUBLANES, ct)
        g = _dot(xc.astype(BF16), wg) + gb
        for d, (a_ref, b_ref) in enumerate(((af_ref, bf_ref), (ab_ref, bb_ref))):
            r = _sigmoid(g[:, (2 * d) * ct:(2 * d + 1) * ct])
            ig = _sigmoid(g[:, (2 * d + 1) * ct:(2 * d + 2) * ct])
            log_a = (-LRU_C) * r * sp[d:d + 1, :]
            a = jnp.exp(log_a)
            b = jnp.sqrt(_neg_expm1(2.0 * log_a, a * a)) * (ig * xc)
            a_ref[pl.ds(r0, rc)] = a.reshape(rc, SUBLANES, ct)
            b_ref[pl.ds(r0, rc)] = b.reshape(rc, SUBLANES, ct)
        return carry

    lax.fori_loop(0, jn // rc, gates, 0)

    hf0 = h0_ref[0, 0]
    hb0 = h0_ref[0, 1]
    if chained:
        def summarise(j, carry):
            hf, pf, hb, pb = carry
            a = af_ref[j]
            jb = jn - 1 - j
            a2 = ab_ref[jb]
            return a * hf + bf_ref[j], a * pf, a2 * hb + bb_ref[jb], a2 * pb

        one = jnp.ones((SUBLANES, ct), F32)
        hf_end, pf_end, hb_end, pb_end = lax.fori_loop(0, jn, summarise, (zero, one, zero, one), unroll=LRU_UNROLL)
        for s in range(SUBLANES - 1):
            hf0 = jnp.where(row == s + 1, pltpu.roll(hf_end + pf_end * hf0, 1, 0), hf0)
            sb = SUBLANES - 2 - s
            hb0 = jnp.where(row == sb, pltpu.roll(hb_end + pb_end * hb0, SUBLANES - 1, 0), hb0)

    def fwd(j, h):
        h = af_ref[j] * h + bf_ref[j]
        hf_ref[j] = h
        return h

    st_ref[0, 0] = lax.fori_loop(0, jn, fwd, hf0, unroll=LRU_UNROLL)

    def bwd(j, h):
        jb = jn - 1 - j
        h = ab_ref[jb] * h + bb_ref[jb]
        hf_ref[jb] = hf_ref[jb] + h
        return h

    st_ref[0, 1] = lax.fori_loop(0, jn, bwd, hb0, unroll=LRU_UNROLL)

    def emit(i, carry):
        r0 = pl.multiple_of(i * rc, rc)
        y = (hf_ref[pl.ds(r0, rc)] * _gelu(lg_ref[0, pl.ds(r0, rc)])).reshape(rc * SUBLANES, ct)
        ya_ref[0, pl.ds(pl.multiple_of(r0 * SUBLANES, rc * SUBLANES), rc * SUBLANES), :] = y.astype(ya_ref.dtype)
        return carry

    lax.fori_loop(0, jn // rc, emit, 0)


def _lru_gate_weights(wa, wx, ba, bx, ct):
    nct = LRU_W // ct
    per = ct // LRU_BW
    eye = jnp.eye(per, dtype=F32)

    def dense(w):
        wt = w.reshape(nct, per, LRU_BW, LRU_BW)
        return jnp.einsum('cpij,pq->cpiqj', wt, eye).reshape(nct, ct, ct)

    mats = [dense(wa[0]), dense(wx[0]), dense(wa[1]), dense(wx[1])]
    bias = [ba[0], bx[0], ba[1], bx[1]]
    wg = jnp.concatenate(mats, axis=-1).astype(BF16)
    gb = jnp.concatenate([v.reshape(nct, 1, ct) for v in bias], axis=-1)
    return wg, gb


def _lru(chained, lx, lg, conv_w, conv_b, wa, wx, ba, bx, lam, h0, ct, rc):
    g, jn, _, _ = lx.shape
    nct = LRU_W // ct
    wg, gb = _lru_gate_weights(wa, wx, ba, bx, ct)
    tile = lambda i, j: (i, 0, 0, j)
    seq = (jn, SUBLANES, ct)
    return pl.pallas_call(
        functools.partial(_lru_kernel, chained, jn, rc),
        out_shape=[jax.ShapeDtypeStruct((g, jn * SUBLANES, LRU_W), BF16),
                   jax.ShapeDtypeStruct((g, 2, SUBLANES, LRU_W), F32)],
        grid=(g, nct),
        in_specs=[pl.BlockSpec((1,) + seq, tile),
                  pl.BlockSpec((1,) + seq, tile),
                  pl.BlockSpec((LRU_CONV, ct), lambda i, j: (0, j)),
                  pl.BlockSpec((1, ct), lambda i, j: (0, j)),
                  pl.BlockSpec((1, ct, 4 * ct), lambda i, j: (j, 0, 0)),
                  pl.BlockSpec((1, 1, 4 * ct), lambda i, j: (j, 0, 0)),
                  pl.BlockSpec((2, ct), lambda i, j: (0, j)),
                  pl.BlockSpec((1, 2, SUBLANES, ct), tile)],
        out_specs=[pl.BlockSpec((1, jn * SUBLANES, ct), lambda i, j: (i, 0, j)),
                   pl.BlockSpec((1, 2, SUBLANES, ct), tile)],
        scratch_shapes=[pltpu.VMEM((jn + LRU_CONV - 1, SUBLANES, ct), F32)] + [pltpu.VMEM(seq, F32)] * 5,
        compiler_params=_cparams("arbitrary", "arbitrary"),
        name="lru_latent" if chained else "lru_context",
    )(lx, lg, conv_w, conv_b.reshape(1, LRU_W), wg, gb, lam, h0)


def _df_lambda(lv_ref, lam_init):
    lv = lv_ref[...]
    l1 = jnp.sum(lv[0:1, :] * lv[1:2, :], axis=-1, keepdims=True)
    l2 = jnp.sum(lv[2:3, :] * lv[3:4, :], axis=-1, keepdims=True)
    return jnp.exp(l1) - jnp.exp(l2) + lam_init


def _diff_combine(o1, o2, lam, subln, lam_init):
    y = o1 - lam * o2
    ms = jnp.mean(y * y, axis=-1, keepdims=True)
    return y * lax.rsqrt(ms + EPS) * subln * (1.0 - lam_init)


def _ctx_attn_kernel(lam_init, nq_ref, nk_ref, nv_ref, dq_ref, dk_ref, dv_ref, lv_ref, sub_ref,
                     nqg_ref, nkg_ref, dqg_ref, dkg_ref, yb_ref, yc_ref, va_ref, safe_ref):
    nk_ref, nv_ref, dk_ref, dv_ref = nk_ref.at[0], nv_ref.at[0], dk_ref.at[0], dv_ref.at[0]
    t = nq_ref.shape[2]
    lam = _df_lambda(lv_ref, lam_init)
    sub = sub_ref[...]

    @pl.when(pl.program_id(0) == 0)
    def _():
        gmax = lambda ref: jnp.max(jnp.abs(ref[...]))
        bound = jnp.maximum(gmax(nqg_ref) * gmax(nkg_ref) * NA_HD ** 0.5, gmax(dqg_ref) * gmax(dkg_ref) * DF_HD ** 0.5)
        safe_ref[0] = (bound <= DF_SAFE_SCORE).astype(jnp.int32)
        lane = lax.broadcasted_iota(jnp.int32, (t, NA_VA - NA_HD), 1)
        va_ref[:, NA_HD:] = jnp.where(lane == 0, 1.0, 0.0).astype(va_ref.dtype)

    safe = safe_ref[0] == 1

    jobs = ([lambda h=h: _dot_nt(nq_ref[0, h], nk_ref[0, h].astype(BF16)) for h in range(NA_HEADS)]
            + [lambda h=h, c=c: _dot_nt(dq_ref[0, h, c], dk_ref[0, h, c].astype(BF16))
               for h in range(DF_HEADS) for c in range(2)])

    def heads(fast):
        state = {"next": jobs[0](), "i": 0}

        def attend():
            s = state["next"]
            state["i"] += 1
            if state["i"] < len(jobs):
                state["next"] = jobs[state["i"]]()
            if not fast:
                s = s - jnp.max(s, axis=-1, keepdims=True)
            o = _dot(jnp.exp(s).astype(BF16), va_ref[...])
            return o[:, 0:NA_HD] / o[:, NA_HD:NA_HD + 1]

        for h in range(NA_HEADS):
            va_ref[:, 0:NA_HD] = nv_ref[0, h].astype(va_ref.dtype)
            yb_ref[0, :, h * NA_HD:(h + 1) * NA_HD] = attend().astype(yb_ref.dtype)
        for h in range(DF_HEADS):
            va_ref[:, 0:DF_VD] = dv_ref[0, h].astype(va_ref.dtype)
            o1 = attend()
            o2 = attend()
            yc_ref[0, :, h * DF_VD:(h + 1) * DF_VD] = _diff_combine(o1, o2, lam, sub, lam_init).astype(yc_ref.dtype)

    @pl.when(safe)
    def _():
        heads(True)

    @pl.when(jnp.logical_not(safe))
    def _():
        heads(False)


def _ctx_attn(nq, nk, nv, dq, dk, dv, df_lambda, df_subln, gains, lam_init, layer):
    b, _, t, _ = nq.shape
    head = pl.BlockSpec((1, NA_HEADS, t, NA_HD), lambda i: (i, 0, 0, 0))
    comp = pl.BlockSpec((1, DF_HEADS, 2, t, DF_HD), lambda i: (i, 0, 0, 0, 0))
    hcache = pl.BlockSpec((1, 1, NA_HEADS, t, NA_HD), lambda i: (i, layer, 0, 0, 0))
    ccache = pl.BlockSpec((1, 1, DF_HEADS, 2, t, DF_HD), lambda i: (i, layer, 0, 0, 0, 0))
    out = pl.BlockSpec((1, t, NA_W), lambda i: (i, 0, 0))
    return pl.pallas_call(
        functools.partial(_ctx_attn_kernel, lam_init),
        out_shape=[jax.ShapeDtypeStruct((b, t, NA_W), BF16), jax.ShapeDtypeStruct((b, t, DF_V_W), BF16)],
        grid=(b,),
        in_specs=[head, hcache, hcache, comp, ccache, hcache,
                  pl.BlockSpec((4, DF_HD), lambda i: (0, 0)),
                  pl.BlockSpec((1, DF_VD), lambda i: (0, 0))]
                 + [pl.BlockSpec((1, g.shape[0]), lambda i: (0, 0)) for g in gains],
        out_specs=[out, out],
        scratch_shapes=[pltpu.VMEM((t, NA_VA), BF16), pltpu.SMEM((1,), jnp.int32)],
        compiler_params=_cparams("arbitrary"),
        name="ctx_attn",
    )(nq, nk, nv, dq, dk, dv, df_lambda, df_subln.reshape(1, DF_VD), *[g.reshape(1, -1) for g in gains])


def _na_bias_tables(rpb, rows_n):
    n_dr, n_dc = 2 * NA_ROWS - 1, 2 * NA_COLS - 1
    qc = np.arange(GRID_W)[:, None]
    kc = np.arange(GRID_W)[None, :]
    cs = np.clip(qc - NA_COLS // 2, 0, GRID_W - NA_COLS)
    ok_col = (kc >= cs) & (kc < cs + NA_COLS)
    d_col = np.clip(kc - qc + NA_COLS - 1, 0, n_dc - 1)
    sel_col = (d_col[None] == np.arange(n_dc)[:, None, None]) & ok_col[None]
    sel_row = np.zeros((3, NA_QROWS, NA_KROWS, n_dr), np.float32)
    for kind, r0 in enumerate((0, NA_QROWS, rows_n - NA_QROWS)):
        kstart = int(np.clip(r0 - NA_ROWS // 2, 0, rows_n - NA_KROWS))
        r = r0 + np.arange(NA_QROWS)[:, None]
        kr = kstart + np.arange(NA_KROWS)[None, :]
        rs = np.clip(r - NA_ROWS // 2, 0, rows_n - NA_ROWS)
        ok_row = (kr >= rs) & (kr < rs + NA_ROWS)
        d_row = np.clip(kr - r + NA_ROWS - 1, 0, n_dr - 1)
        sel_row[kind] = (d_row[..., None] == np.arange(n_dr)) & ok_row[..., None]
    ok = (sel_row.sum(-1) > 0)[:, :, None, :, None] & ok_col[None, None, :, None, :]
    hi = lax.Precision.HIGHEST
    cols = jnp.einsum('had,dqk->haqk', rpb, jnp.asarray(sel_col, F32), precision=hi)
    tabs = jnp.einsum('tija,haqk->thiqjk', jnp.asarray(sel_row), cols, precision=hi)
    tabs = jnp.where(jnp.asarray(ok)[:, None], tabs, NEG_INF)
    return tabs.reshape(3, NA_HEADS, NA_QROWS * GRID_W, NA_KROWS * GRID_W)


def _lat_na_kernel(rows_n, q_ref, k_ref, v_ref, ck_ref, cv_ref, bias_ref, qg_ref, kg_ref, rpb_ref, o_ref, safe_ref):
    i = pl.program_id(1)
    kstart = jnp.clip(i * NA_QROWS - NA_ROWS // 2, 0, rows_n - NA_KROWS)
    k0 = pl.multiple_of(kstart * GRID_W, NA_QROWS * GRID_W)
    nk = NA_KROWS * GRID_W

    @pl.when(i == 0)
    def _():
        ck2 = jnp.zeros((1, 1), F32)
        for h in range(NA_HEADS):
            c = ck_ref[0, h].astype(F32)
            ck2 = jnp.maximum(ck2, jnp.max(jnp.sum(c * c, axis=-1, keepdims=True), axis=0, keepdims=True))
        k_max = jnp.maximum(jnp.max(jnp.abs(kg_ref[...])) * NA_HD ** 0.5, jnp.sqrt(jnp.max(ck2)))
        bound = jnp.max(jnp.abs(qg_ref[...])) * k_max + jnp.max(jnp.abs(rpb_ref[...]))
        safe_ref[0] = (bound <= DF_SAFE_SCORE).astype(jnp.int32)

    safe = safe_ref[0] == 1

    def scores(h):
        q = q_ref[0, h]
        return _dot_nt(q, k_ref[0, h, pl.ds(k0, nk), :]) + bias_ref[0, h], _dot_nt(q, ck_ref[0, h])

    def heads(fast):
        nxt = scores(0)
        for h in range(NA_HEADS):
            s_loc, s_ctx = nxt
            if h + 1 < NA_HEADS:
                nxt = scores(h + 1)
            if not fast:
                m = jnp.maximum(jnp.max(s_loc, axis=-1, keepdims=True), jnp.max(s_ctx, axis=-1, keepdims=True))
                s_loc = s_loc - m
                s_ctx = s_ctx - m
            o = (_dot(jnp.exp(s_loc).astype(BF16), v_ref[0, h, pl.ds(k0, nk), :])
                 + _dot(jnp.exp(s_ctx).astype(BF16), cv_ref[0, h]))
            o = o[:, 0:NA_HD] / o[:, NA_HD:NA_HD + 1]
            o_ref[0, :, h * NA_HD:(h + 1) * NA_HD] = o.astype(o_ref.dtype)

    @pl.when(safe)
    def _():
        heads(True)

    @pl.when(jnp.logical_not(safe))
    def _():
        heads(False)


def _lat_na(q, k, v, ck, cv, bias_tabs, q_gain, k_gain, rpb):
    b, _, t, _ = q.shape
    rows_n = t // GRID_W
    nt = rows_n // NA_QROWS
    tq = NA_QROWS * GRID_W
    past = ck.shape[2]
    full = lambda n, w: pl.BlockSpec((1, NA_HEADS, n, w), lambda i, j: (i, 0, 0, 0))
    const2 = lambda i, j: (0, 0)
    rpb2 = rpb.reshape(-1, rpb.shape[-1])

    def kind(i, j):
        return (jnp.where(j == 0, 0, jnp.where(j == nt - 1, 2, 1)), 0, 0, 0)

    return pl.pallas_call(
        functools.partial(_lat_na_kernel, rows_n),
        out_shape=jax.ShapeDtypeStruct((b, t, NA_W), BF16),
        grid=(b, nt),
        in_specs=[pl.BlockSpec((1, NA_HEADS, tq, NA_HD), lambda i, j: (i, 0, j, 0)),
                  full(t, NA_HD), full(t, NA_VA), full(past, NA_HD), full(past, NA_VA),
                  pl.BlockSpec((1, NA_HEADS, tq, NA_KROWS * GRID_W), kind),
                  pl.BlockSpec((1, NA_HD), const2), pl.BlockSpec((1, NA_HD), const2),
                  pl.BlockSpec(rpb2.shape, const2)],
        out_specs=pl.BlockSpec((1, tq, NA_W), lambda i, j: (i, j, 0)),
        scratch_shapes=[pltpu.SMEM((1,), jnp.int32)],
        compiler_params=_cparams("arbitrary", "arbitrary"),
        name="lat_na",
    )(q, k, v, ck, cv, bias_tabs, q_gain.reshape(1, NA_HD), k_gain.reshape(1, NA_HD), rpb2)


def _lat_df_kernel(lam_init, q_ref, ktl_ref, vl_ref, ktc_ref, vc_ref, lv_ref, sub_ref, qg_ref, kg_ref,
                   o_ref, safe_ref):
    tq = q_ref.shape[2]
    n_lat = ktl_ref.shape[1]
    n_ctx = ktc_ref.shape[1]
    lam = _df_lambda(lv_ref, lam_init)
    q = q_ref[0, 0]
    qs = [q[:, c * DF_HD:(c + 1) * DF_HD] for c in range(2)]
    rows = [slice(c * DF_HD, (c + 1) * DF_HD) for c in range(2)]

    @pl.when(pl.program_id(2) == 0)
    def _():
        ck2 = jnp.zeros((1, DF_CK), F32)
        for c in range(2):
            for j in range(n_ctx):
                k = ktc_ref[0, j, rows[c], :].astype(F32)
                ck2 = jnp.maximum(ck2, jnp.sum(k * k, axis=0, keepdims=True))
        k_max = jnp.maximum(jnp.max(jnp.abs(kg_ref[...])) * DF_HD ** 0.5, jnp.sqrt(jnp.max(ck2)))
        safe_ref[0] = (jnp.max(jnp.abs(qg_ref[...])) * k_max <= DF_SAFE_SCORE).astype(jnp.int32)

    safe = safe_ref[0] == 1

    def finish(a1, a2):
        o1 = a1[:, 0:DF_VD] / a1[:, DF_VD:DF_VD + 1]
        o2 = a2[:, 0:DF_VD] / a2[:, DF_VD:DF_VD + 1]
        o_ref[0, 0] = _diff_combine(o1, o2, lam, sub_ref[...], lam_init).astype(o_ref.dtype)

    def over_chunks(step, carry):
        def trip(t, carry):
            for u in range(DF_UNROLL):
                j = t * DF_UNROLL + u
                v = vl_ref[0, 0, pl.ds(pl.multiple_of(j * DF_CK, DF_CK), DF_CK), :]
                carry = step(carry, lambda c: ktl_ref[0, j, rows[c], :], v)
            return carry

        carry = lax.fori_loop(0, n_lat // DF_UNROLL, trip, carry)
        for j in range(n_ctx):
            carry = step(carry, lambda c: ktc_ref[0, j, rows[c], :], vc_ref[0, 0, j * DF_CK:(j + 1) * DF_CK, :])
        return carry

    @pl.when(safe)
    def _():
        def step(acc, kt, v):
            e = [jnp.exp(_dot(qs[c], kt(c))).astype(BF16) for c in range(2)]
            return acc + _dot(jnp.concatenate(e, axis=0), v)

        acc = over_chunks(step, jnp.zeros((2 * tq, DF_VA), F32))
        finish(acc[0:tq], acc[tq:])

    @pl.when(jnp.logical_not(safe))
    def _():
        def step(carry, kt, v):
            new = []
            for c in range(2):
                m, acc = carry[c]
                s = _dot(qs[c], kt(c))
                m_new = jnp.maximum(m, jnp.max(s, axis=-1, keepdims=True))
                new.append((m_new, jnp.exp(m - m_new) * acc + _dot(jnp.exp(s - m_new).astype(BF16), v)))
            return tuple(new)

        init = (jnp.full((tq, 1), NEG_INF, F32), jnp.zeros((tq, DF_VA), F32))
        (_, a1), (_, a2) = over_chunks(step, (init, init))
        finish(a1, a2)


def _lat_df(q, kt_lat, v_lat, kt_ctx, v_ctx, df_lambda, df_subln, q_gain, k_gain, lam_init, tq):
    b, _, t, _ = q.shape
    const2 = lambda i, h, j: (0, 0)
    assert kt_lat.shape[1] % DF_UNROLL == 0
    kt_spec = lambda kt: pl.BlockSpec((1, kt.shape[1], 2 * DF_HD, DF_CK), lambda i, h, j: (i, 0, h, 0))
    v_spec = lambda v: pl.BlockSpec((1, 1, v.shape[2], DF_VA), lambda i, h, j: (i, h, 0, 0))
    return pl.pallas_call(
        functools.partial(_lat_df_kernel, lam_init),
        out_shape=jax.ShapeDtypeStruct((b, DF_HEADS, t, DF_VD), BF16),
        grid=(b, DF_HEADS, t // tq),
        in_specs=[pl.BlockSpec((1, 1, tq, 2 * DF_HD), lambda i, h, j: (i, h, j, 0)),
                  kt_spec(kt_lat), v_spec(v_lat), kt_spec(kt_ctx), v_spec(v_ctx),
                  pl.BlockSpec((4, DF_HD), const2), pl.BlockSpec((1, DF_VD), const2),
                  pl.BlockSpec((1, DF_HD), const2), pl.BlockSpec((1, DF_HD), const2)],
        out_specs=pl.BlockSpec((1, 1, tq, DF_VD), lambda i, h, j: (i, h, j, 0)),
        scratch_shapes=[pltpu.SMEM((1,), jnp.int32)],
        compiler_params=_cparams("arbitrary", "arbitrary", "arbitrary"),
        name="lat_df",
    )(q, kt_lat, v_lat, kt_ctx, v_ctx, df_lambda, df_subln.reshape(1, DF_VD),
      q_gain.reshape(1, DF_HD), k_gain.reshape(1, DF_HD))


def _with_ones_column(v, width):
    pad = width - v.shape[-1] - 1
    return jnp.concatenate([v, jnp.ones(v.shape[:-1] + (1,), v.dtype), jnp.zeros(v.shape[:-1] + (pad,), v.dtype)],
                           axis=-1)


def _out_kernel(ya_ref, yb_ref, yc_ref, x_ref, ga_ref, w_ref, o_ref):
    o_ref[0] = _out_proj_rows(ya_ref, yb_ref, yc_ref, x_ref, ga_ref, w_ref)


def _out_proj_operands(latent, ya, yb, yc, x, g_a, w_out, tm):
    b, t, _ = x.shape
    _, ya_spec = _seq_split(latent, b, t, tm, BF16)
    ya = ya.reshape(ya.shape[0], ya.shape[1] // SUBLANES, SUBLANES * LRU_W)
    mod_map = (lambda i, j: (i, 0, 0)) if g_a.shape[0] > 1 else (lambda i, j: (0, 0, 0))
    tile = lambda i, j: (i, j, 0)
    if yc.ndim == 4:
        yc_spec = pl.BlockSpec((1, DF_HEADS, tm, DF_VD), lambda i, j: (i, 0, j, 0))
    else:
        yc_spec = pl.BlockSpec((1, tm, DF_V_W), tile)
    specs = [ya_spec, pl.BlockSpec((1, tm, NA_W), tile), yc_spec, pl.BlockSpec((1, tm, D_MODEL), tile),
             pl.BlockSpec((1, 1, D_MODEL), mod_map), pl.BlockSpec((D_CAT, D_MODEL), lambda i, j: (0, 0))]
    return specs, (ya, yb, yc, x, g_a, w_out)


def _out_proj(latent, ya, yb, yc, x, g_a, w_out, tm):
    b, t, _ = x.shape
    specs, args = _out_proj_operands(latent, ya, yb, yc, x, g_a, w_out, tm)
    return pl.pallas_call(
        _out_kernel,
        out_shape=jax.ShapeDtypeStruct(x.shape, F32),
        grid=(b, t // tm),
        in_specs=specs,
        out_specs=pl.BlockSpec((1, tm, D_MODEL), lambda i, j: (i, j, 0)),
        compiler_params=_cparams("arbitrary", "arbitrary"),
        name="out_proj",
    )(*args)


def _out_proj_rows(ya_ref, yb_ref, yc_ref, x_ref, ga_ref, w_ref):
    if len(yc_ref.shape) == 4:
        yc = jnp.concatenate([yc_ref[0, h] for h in range(DF_HEADS)], axis=-1)
    else:
        yc = yc_ref[0]
    y = (_dot(ya_ref[0], w_ref[0:LRU_W, :]) + _dot(yb_ref[0], w_ref[LRU_W:LRU_W + NA_W, :])
         + _dot(yc, w_ref[LRU_W + NA_W:, :]))
    return x_ref[0] + ga_ref[0] * y


def _out_ffn_kernel(ya_ref, yb_ref, yc_ref, x_ref, ga_ref, w_ref, *ffn_refs):
    x = _out_proj_rows(ya_ref, yb_ref, yc_ref, x_ref, ga_ref, w_ref)
    outside = jnp.zeros((HALO, D_MODEL), F32)
    _conv_ffn(x, outside, outside, False, False, *ffn_refs)


def _ffn_kernel(nt, xp_ref, x_ref, xn_ref, *ffn_refs):
    j = pl.program_id(1)
    _conv_ffn(x_ref[0], xp_ref[0], xn_ref[0], j > 0, j < nt - 1, *ffn_refs)


def _conv_ffn(x, x_prev, x_next, has_prev, has_next, sh_ref, sc_ref, gf_ref, g_ref, up_ref, cw_ref, cb_ref, dn_ref,
              o_ref, acc_ref):
    tm = x.shape[0]
    xw = jnp.concatenate([x_prev, x, x_next], axis=0)
    rows = xw.shape[0]
    ms = jnp.mean(xw * xw, axis=-1, keepdims=True)
    y = xw * lax.rsqrt(ms + EPS) * g_ref[...]
    h = y * (1.0 + sc_ref[0]) + sh_ref[0]
    row = lax.broadcasted_iota(jnp.int32, (rows, 1), 0)
    valid = ((row >= HALO) | has_prev) & ((row < HALO + tm) | has_next)
    h = jnp.where(valid, h, 0.0).astype(BF16)

    n_chunks = D_FF // FFN_CF

    def up_proj(c):
        return [_dot(h, up_ref[:, off:off + FFN_CF]) for off in (c * FFN_CF, D_FF + c * FFN_CF)]

    def conv(u, off):
        taps = (pltpu.roll(u, 1, 0), u, pltpu.roll(u, rows - 1, 0))
        out = cb_ref[:, off:off + FFN_CF]
        for k in range(3):
            out = out + cw_ref[k:k + 1, off:off + FFN_CF] * taps[k][HALO:HALO + tm, :]
        return out

    u_cur = up_proj(0)
    for c in range(n_chunks):
        u_next = up_proj(c + 1) if c + 1 < n_chunks else None
        act = (_gelu(conv(u_cur[1], D_FF + c * FFN_CF)) * conv(u_cur[0], c * FFN_CF)).astype(BF16)
        down = _dot(act, dn_ref[c * FFN_CF:(c + 1) * FFN_CF, :])
        if c == 0:
            acc_ref[...] = down
        else:
            acc_ref[...] += down
        u_cur = u_next
    o_ref[0] = x + gf_ref[0] * acc_ref[...]


def _ffn_operands(shift, scale, g_f, gain, up, conv_w, conv_b, down):
    mod_map = (lambda i, j: (i, 0, 0)) if shift.shape[0] > 1 else (lambda i, j: (0, 0, 0))
    const2 = lambda i, j: (0, 0)
    mod_spec = pl.BlockSpec((1, 1, D_MODEL), mod_map)
    specs = [mod_spec, mod_spec, mod_spec,
             pl.BlockSpec((1, D_MODEL), const2),
             pl.BlockSpec((D_MODEL, 2 * D_FF), const2),
             pl.BlockSpec((3, 2 * D_FF), const2),
             pl.BlockSpec((1, 2 * D_FF), const2),
             pl.BlockSpec((D_FF, D_MODEL), const2)]
    return specs, (shift, scale, g_f, gain.reshape(1, D_MODEL), up, conv_w, conv_b.reshape(1, 2 * D_FF), down)


def _ffn(x, tm, *ffn_params):
    b, t, _ = x.shape
    nt = t // tm
    hb = tm // HALO
    specs, args = _ffn_operands(*ffn_params)
    return pl.pallas_call(
        functools.partial(_ffn_kernel, nt),
        out_shape=jax.ShapeDtypeStruct(x.shape, F32),
        grid=(b, nt),
        in_specs=[pl.BlockSpec((1, HALO, D_MODEL), lambda i, j: (i, jnp.maximum(j * hb - 1, 0), 0)),
                  pl.BlockSpec((1, tm, D_MODEL), lambda i, j: (i, j, 0)),
                  pl.BlockSpec((1, HALO, D_MODEL), lambda i, j: (i, jnp.minimum((j + 1) * hb, t // HALO - 1), 0))]
                 + specs,
        out_specs=pl.BlockSpec((1, tm, D_MODEL), lambda i, j: (i, j, 0)),
        scratch_shapes=[pltpu.VMEM((tm, D_MODEL), F32)],
        compiler_params=_cparams("arbitrary", "arbitrary"),
        name="ffn",
    )(x, x, x, *args)


def _out_ffn(latent, ya, yb, yc, x, g_a, w_out, *ffn_params):
    b, t, _ = x.shape
    out_specs, out_args = _out_proj_operands(latent, ya, yb, yc, x, g_a, w_out, t)
    ffn_specs, ffn_args = _ffn_operands(*ffn_params)
    return pl.pallas_call(
        _out_ffn_kernel,
        out_shape=jax.ShapeDtypeStruct(x.shape, F32),
        grid=(b, 1),
        in_specs=out_specs + ffn_specs,
        out_specs=pl.BlockSpec((1, t, D_MODEL), lambda i, j: (i, j, 0)),
        scratch_shapes=[pltpu.VMEM((t, D_MODEL), F32)],
        compiler_params=_cparams("arbitrary", "arbitrary"),
        name="out_ffn",
    )(*out_args, *ffn_args)


def _split_mod(mod):
    return [mod[:, None, i * D_MODEL:(i + 1) * D_MODEL] for i in range(N_MOD)]


def kernel(x_prompt, x_sample, state_lru, cache_na_k, cache_na_v, cache_df_k, cache_df_v, c, c_ctx, ada_w, ada_b, norm_mix, norm_ffn, w_in, lru_conv_w, lru_conv_b, lru_wa, lru_ba, lru_wx, lru_bx, lru_lambda, na_q_norm, na_k_norm, na_rpb, df_q_norm, df_k_norm, df_lambda, df_subln, w_out, ffn_up, ffn_conv_w, ffn_conv_b, ffn_down):
    bp, tp, _ = x_prompt.shape
    bs, ts, _ = x_sample.shape
    past = cache_na_k.shape[3]
    rows_n = ts // GRID_W

    c_all = jnp.zeros((SUBLANES, D_MODEL), F32).at[0].set(c_ctx).at[1:1 + bs].set(c)
    mods = _modulation(c_all, ada_w, ada_b)
    rope_tabs = _rope_tables(ts)
    zeros_state = jnp.zeros((bp // SUBLANES, 2, SUBLANES, LRU_W), F32)

    y_p, y_s = x_prompt, x_sample
    s_lru = []
    hcache = (bp, DEPTH, NA_HEADS, tp, NA_HD)
    caches = [jnp.zeros(hcache, F32), jnp.zeros(hcache, F32),
              jnp.zeros((bp, DEPTH, DF_HEADS, 2, tp, DF_HD), F32), jnp.zeros(hcache, F32)]
    for l in range(DEPTH):
        lam_init = 0.8 - 0.6 * math.exp(-0.3 * l)
        w_in_l = w_in[l].astype(BF16)
        w_out_l = w_out[l].astype(BF16)
        up_l = ffn_up[l].astype(BF16)
        down_l = ffn_down[l].astype(BF16)
        lru_args = (lru_conv_w[l], lru_conv_b[l], lru_wa[l], lru_wx[l], lru_ba[l], lru_bx[l], lru_lambda[l])
        norm_args = (na_q_norm[l], na_k_norm[l], df_q_norm[l], df_k_norm[l])

        sh_a, sc_a, g_a, sh_f, sc_f, g_f = _split_mod(mods[l, 0:1])
        lx, lg, nq, *caches = _prep(False, y_p, sh_a, sc_a, norm_mix[l], w_in_l, *norm_args, None, tp, l, caches)
        dq = caches.pop(2)
        seq4 = lambda v: v.reshape(v.shape[0], v.shape[1], SUBLANES, LRU_W)
        ya, st = _lru(False, seq4(lx), seq4(lg), *lru_args, zeros_state, 256, LRU_RC)
        yb, yc = _ctx_attn(nq, caches[0], caches[1], dq, caches[2], caches[3], df_lambda[l], df_subln[l], norm_args,
                           lam_init, l)
        ffn_params = (norm_ffn[l], up_l, ffn_conv_w[l], ffn_conv_b[l], down_l)
        y_p = _out_ffn(False, ya, yb, yc, y_p, g_a, w_out_l, sh_f, sc_f, g_f, *ffn_params)
        s_lru.append(st.transpose(0, 2, 1, 3).reshape(bp, 2, LRU_W))

        sh_a, sc_a, g_a, sh_f, sc_f, g_f = _split_mod(mods[l, 1:1 + bs])
        lx, lg, nq, nk, nv, dq, dk, dv = _prep(True, y_s, sh_a, sc_a, norm_mix[l], w_in_l, *norm_args, rope_tabs,
                                               ts // SUBLANES)
        h0 = jnp.broadcast_to(state_lru[:, l, :, None, :], (bs, 2, SUBLANES, LRU_W))
        ya, _ = _lru(True, seq4(lx), seq4(lg), *lru_args, h0, 256, LRU_RC)
        yb = _lat_na(nq, nk, nv, cache_na_k[:, l].astype(BF16), _with_ones_column(cache_na_v[:, l].astype(BF16), NA_VA),
                     _na_bias_tables(na_rpb[l], rows_n), na_q_norm[l], na_k_norm[l], na_rpb[l])
        kt_ctx = jnp.swapaxes(cache_df_k[:, l].astype(BF16), -1, -2).reshape(bs, DF_QK_W, past // DF_CK, DF_CK)
        kt_ctx = kt_ctx.transpose(0, 2, 1, 3)
        v_ctx = _with_ones_column(cache_df_v[:, l].astype(BF16), DF_VA)
        yc = _lat_df(dq, dk, dv, kt_ctx, v_ctx, df_lambda[l], df_subln[l], df_q_norm[l], df_k_norm[l], lam_init, DF_TQ)
        x1 = _out_proj(True, ya, yb, yc, y_s, g_a, w_out_l, ts // SUBLANES)
        y_s = _ffn(x1, FFN_TM, sh_f, sc_f, g_f, *ffn_params)

    return (y_p, y_s, jnp.stack(s_lru, axis=1), *caches)
```
